```python
import jax, jax.numpy as jnp
from jax import lax
import numpy as np

D_MODEL = 1024
BATCH = 4
SEQ = 8192
DEPTH = 2

HEAD_DIM = 64
GROUP_HEADS = 4
GROUP_WIDTH = GROUP_HEADS * HEAD_DIM
N_GROUPS = 4
MIX_WIDTH = N_GROUPS * GROUP_WIDTH
ROPE_THETA = 10000.0
EPS = 1e-6
Q_BLOCK = 128
NEG = -1e30

MLA_Q_RANK = 192
MLA_KV_RANK = 128
MLA_NOPE = 64
MLA_ROPE = 32
MLA_V = 64
MLA_QK = MLA_NOPE + MLA_ROPE
SWA_KV_HEADS = 2
WINDOW = 128
MOBA_BLOCK = 256
MOBA_TOPK = 3
D_FF = 2816
CONV_WIDTH = 3
PLE_DIM = 256

SB_COLS = 3 * GROUP_WIDTH
MLA_COLS = MLA_Q_RANK + MLA_KV_RANK + MLA_ROPE
SWA_COLS = GROUP_WIDTH + 2 * SWA_KV_HEADS * HEAD_DIM
MOBA_COLS = 3 * GROUP_WIDTH
OFF_SB = 0
OFF_MLA = OFF_SB + SB_COLS
OFF_SWA = OFF_MLA + MLA_COLS
OFF_MOBA = OFF_SWA + SWA_COLS
N_IN = OFF_MOBA + MOBA_COLS

kernel_name = 'hybrid_parallel_head_groups_block'


def rms_norm(x, g):
    xf = x.astype(jnp.float32)
    y = xf * lax.rsqrt(jnp.mean(xf * xf, axis=-1, keepdims=True) + EPS)
    return (y * g.astype(jnp.float32)).astype(x.dtype)


def rope(x, pos):
    half = x.shape[-1] // 2
    inv = ROPE_THETA ** (-jnp.arange(half, dtype=jnp.float32) / half)
    ang = pos.astype(jnp.float32)[..., None] * inv
    cos = jnp.cos(ang)[:, :, None, :]
    sin = jnp.sin(ang)[:, :, None, :]
    xf = x.astype(jnp.float32)
    x1, x2 = xf[..., :half], xf[..., half:]
    return jnp.concatenate([x1 * cos - x2 * sin, x2 * cos + x1 * sin], axis=-1).astype(x.dtype)


def map_query_blocks(fn, q):
    B, S, H, d = q.shape
    nq = S // Q_BLOCK
    qb = q.reshape(B, nq, Q_BLOCK, H, d).transpose(1, 0, 2, 3, 4)
    out = lax.map(fn, (qb, jnp.arange(nq)))
    return out.transpose(1, 0, 2, 3, 4).reshape(B, S, H, out.shape[-1])


def stick_breaking_attn(q, k, v):
    S = q.shape[1]
    scale = q.shape[-1] ** -0.5
    kpos = jnp.arange(S)

    def block(args):
        qb, i = args
        qpos = i * Q_BLOCK + jnp.arange(Q_BLOCK)
        z = jnp.einsum('bqhd,bshd->bhqs', qb, k, preferred_element_type=jnp.float32) * scale
        mask = kpos[None, :] < qpos[:, None]
        log_beta = jax.nn.log_sigmoid(z)
        log_keep = jnp.where(mask, jax.nn.log_sigmoid(-z), 0.0)
        shifted = jnp.concatenate([log_keep[..., 1:], jnp.zeros_like(log_keep[..., :1])], axis=-1)
        tail = lax.cumsum(shifted, axis=3, reverse=True)
        w = jnp.where(mask, jnp.exp(log_beta + tail), 0.0).astype(v.dtype)
        return jnp.einsum('bhqs,bshd->bqhd', w, v)

    return map_query_blocks(block, q)


def causal_softmax_attn(q, k, v):
    S = q.shape[1]
    scale = q.shape[-1] ** -0.5
    kpos = jnp.arange(S)

    def block(args):
        qb, i = args
        qpos = i * Q_BLOCK + jnp.arange(Q_BLOCK)
        s = jnp.einsum('bqhd,bshd->bhqs', qb, k, preferred_element_type=jnp.float32) * scale
        s = jnp.where(kpos[None, :] <= qpos[:, None], s, NEG)
        pr = jax.nn.softmax(s, axis=-1).astype(v.dtype)
        return jnp.einsum('bhqs,bshd->bqhd', pr, v)

    return map_query_blocks(block, q)


def sliding_window_sink_attn(q, k, v, sinks):
    B, S, Hq, d = q.shape
    Hkv = k.shape[2]
    G = Hq // Hkv
    nb = S // WINDOW
    scale = d ** -0.5
    qb = q.reshape(B, nb, WINDOW, Hkv, G, d)
    kb = k.reshape(B, nb, WINDOW, Hkv, d)
    vb = v.reshape(B, nb, WINDOW, Hkv, d)
    pad = ((0, 0), (1, 0), (0, 0), (0, 0), (0, 0))
    kcat = jnp.concatenate([jnp.pad(kb, pad)[:, :-1], kb], axis=2)
    vcat = jnp.concatenate([jnp.pad(vb, pad)[:, :-1], vb], axis=2)
    s = jnp.einsum('bnqkgd,bnskd->bnkgqs', qb, kcat, preferred_element_type=jnp.float32) * scale
    qi = jnp.arange(WINDOW)[:, None] + WINDOW
    si = jnp.arange(2 * WINDOW)[None, :]
    band = (si <= qi) & (qi - si < WINDOW)
    has_prev = (jnp.arange(nb)[:, None, None] > 0) | (si[None] >= WINDOW)
    mask = band[None] & has_prev
    s = jnp.where(mask[None, :, None, None], s, NEG)
    sink = sinks.astype(jnp.float32).reshape(Hkv, G)[None, None, :, :, None, None]
    sink = jnp.broadcast_to(sink, s.shape[:-1] + (1,))
    pr = jax.nn.softmax(jnp.concatenate([s, sink], axis=-1), axis=-1)[..., :-1].astype(v.dtype)
    o = jnp.einsum('bnkgqs,bnskd->bnqkgd', pr, vcat)
    return o.reshape(B, S, Hq, d)


def moba_attn(q, k, v):
    B, S, H, d = q.shape
    nblk = -(-S // MOBA_BLOCK)
    s_pad = nblk * MOBA_BLOCK
    padw = ((0, 0), (0, s_pad - S), (0, 0), (0, 0))
    kblk = jnp.pad(k, padw).reshape(B, nblk, MOBA_BLOCK, H, d).transpose(0, 3, 1, 2, 4)
    vblk = jnp.pad(v, padw).reshape(B, nblk, MOBA_BLOCK, H, d).transpose(0, 3, 1, 2, 4)
    kmean = jnp.mean(kblk.astype(jnp.float32), axis=3)
    top = min(MOBA_TOPK, nblk)
    scale = d ** -0.5
    b_idx = jnp.arange(B)[:, None, None, None]
    h_idx = jnp.arange(H)[None, :, None, None]
    blk_ids = jnp.arange(nblk)

    def block(args):
        qb, i = args
        qh = qb.transpose(0, 2, 1, 3)
        own = (i * Q_BLOCK) // MOBA_BLOCK
        qpos = i * Q_BLOCK + jnp.arange(Q_BLOCK)
        gate = jnp.einsum('bhqd,bhnd->bhqn', qh.astype(jnp.float32), kmean)
        gate = jnp.where(blk_ids < own, gate, NEG)
        _, top_idx = lax.top_k(gate, top)
        valid = jnp.arange(top) < own
        ksel = kblk[b_idx, h_idx, top_idx]
        vsel = vblk[b_idx, h_idx, top_idx]
        s_sel = jnp.einsum('bhqd,bhqjsd->bhqjs', qh, ksel, preferred_element_type=jnp.float32) * scale
        s_sel = jnp.where(valid[:, None], s_sel, NEG)
        k_own = lax.dynamic_index_in_dim(kblk, own, axis=2, keepdims=False)
        v_own = lax.dynamic_index_in_dim(vblk, own, axis=2, keepdims=False)
        s_own = jnp.einsum('bhqd,bhsd->bhqs', qh, k_own, preferred_element_type=jnp.float32) * scale
        own_pos = own * MOBA_BLOCK + jnp.arange(MOBA_BLOCK)
        s_own = jnp.where(own_pos[None, :] <= qpos[:, None], s_own, NEG)
        n_sel = top * MOBA_BLOCK
        logits = jnp.concatenate([s_sel.reshape(B, H, Q_BLOCK, n_sel), s_own], axis=-1)
        pr = jax.nn.softmax(logits, axis=-1).astype(v.dtype)
        p_sel = pr[..., :n_sel].reshape(B, H, Q_BLOCK, top, MOBA_BLOCK)
        o = (jnp.einsum('bhqjs,bhqjsd->bhqd', p_sel, vsel)
             + jnp.einsum('bhqs,bhsd->bhqd', pr[..., n_sel:], v_own))
        return o.transpose(0, 2, 1, 3)

    return map_query_blocks(block, q)


def causal_depthwise_conv(u, w, b):
    C = u.shape[-1]
    y = lax.conv_general_dilated(u, w[:, None, :].astype(u.dtype), window_strides=(1,),
                                 padding=[(CONV_WIDTH - 1, 0)],
                                 dimension_numbers=('NWC', 'WIO', 'NWC'),
                                 feature_group_count=C)
    return y + b.astype(u.dtype)


def setup_inputs(seed: int = 0) -> dict:
    key = jax.random.key(seed)
    ks = jax.random.split(key, 32)
    f32 = jnp.float32

    def nrm(k, shape, fan_in):
        return jax.random.normal(k, shape, f32) * (fan_in ** -0.5)

    def gain(k, shape):
        return 1.0 + 0.05 * jax.random.normal(k, shape, f32)

    L = DEPTH
    return {
        'x': jax.random.normal(ks[0], (BATCH, SEQ, D_MODEL), f32),
        'p': jax.random.normal(ks[1], (DEPTH, BATCH, SEQ, PLE_DIM), f32),
        'positions': jnp.tile(jnp.arange(SEQ, dtype=jnp.int32)[None, :], (BATCH, 1)),
        'attn_norm': gain(ks[2], (L, D_MODEL)),
        'w_in': nrm(ks[3], (L, D_MODEL, N_IN), D_MODEL),
        'mla_q_norm': gain(ks[4], (L, MLA_Q_RANK)),
        'mla_w_uq': nrm(ks[5], (L, MLA_Q_RANK, GROUP_HEADS * MLA_QK), MLA_Q_RANK),
        'mla_kv_norm': gain(ks[6], (L, MLA_KV_RANK)),
        'mla_w_ukv': nrm(ks[7], (L, MLA_KV_RANK, GROUP_HEADS * (MLA_NOPE + MLA_V)), MLA_KV_RANK),
        'mla_q_gain': gain(ks[8], (L, MLA_QK)),
        'mla_k_gain': gain(ks[9], (L, MLA_QK)),
        'swa_q_gain': gain(ks[10], (L, HEAD_DIM)),
        'swa_k_gain': gain(ks[11], (L, HEAD_DIM)),
        'swa_sinks': 0.5 * jax.random.normal(ks[12], (L, GROUP_HEADS), f32),
        'moba_q_gain': gain(ks[13], (L, HEAD_DIM)),
        'moba_k_gain': gain(ks[14], (L, HEAD_DIM)),
        'group_norm': gain(ks[15], (L, MIX_WIDTH)),
        'w_o': nrm(ks[16], (L, MIX_WIDTH, D_MODEL), MIX_WIDTH * 2 * DEPTH),
        'ffn_norm': gain(ks[17], (L, D_MODEL)),
        'w_up': nrm(ks[18], (L, D_MODEL, 2 * D_FF), D_MODEL),
        'conv_w': nrm(ks[19], (L, CONV_WIDTH, 2 * D_FF), CONV_WIDTH),
        'conv_b': 0.01 * jax.random.normal(ks[20], (L, 2 * D_FF), f32),
        'w_down': nrm(ks[21], (L, D_FF, D_MODEL), D_FF * 2 * DEPTH),
        'ple_proj': nrm(ks[22], (L, PLE_DIM, D_MODEL), PLE_DIM * 2 * DEPTH),
        'ple_gate': nrm(ks[23], (L, D_MODEL, D_MODEL), D_MODEL),
    }


def reference(x, p, positions, attn_norm, w_in, mla_q_norm, mla_w_uq, mla_kv_norm, mla_w_ukv,
              mla_q_gain, mla_k_gain, swa_q_gain, swa_k_gain, swa_sinks, moba_q_gain, moba_k_gain,
              group_norm, w_o, ffn_norm, w_up, conv_w, conv_b, w_down, ple_proj, ple_gate):
    B, S, _ = x.shape
    H = GROUP_HEADS
    for i in range(DEPTH):
        h = rms_norm(x, attn_norm[i])
        proj = h @ w_in[i]

        sb = proj[..., OFF_SB:OFF_SB + SB_COLS].reshape(B, S, 3, H, HEAD_DIM)
        o_a = stick_breaking_attn(sb[:, :, 0], sb[:, :, 1], sb[:, :, 2])

        mla = proj[..., OFF_MLA:OFF_MLA + MLA_COLS]
        c_q = rms_norm(mla[..., :MLA_Q_RANK], mla_q_norm[i])
        c_kv = rms_norm(mla[..., MLA_Q_RANK:MLA_Q_RANK + MLA_KV_RANK], mla_kv_norm[i])
        k_pe = mla[..., MLA_Q_RANK + MLA_KV_RANK:]
        q_m = (c_q @ mla_w_uq[i]).reshape(B, S, H, MLA_QK)
        kv_m = (c_kv @ mla_w_ukv[i]).reshape(B, S, H, MLA_NOPE + MLA_V)
        k_m = jnp.concatenate([kv_m[..., :MLA_NOPE],
                               jnp.broadcast_to(k_pe[:, :, None, :], (B, S, H, MLA_ROPE))], axis=-1)
        v_m = kv_m[..., MLA_NOPE:]
        q_m = rms_norm(q_m, mla_q_gain[i])
        k_m = rms_norm(k_m, mla_k_gain[i])
        q_m = jnp.concatenate([q_m[..., :MLA_NOPE], rope(q_m[..., MLA_NOPE:], positions)], axis=-1)
        k_m = jnp.concatenate([k_m[..., :MLA_NOPE], rope(k_m[..., MLA_NOPE:], positions)], axis=-1)
        o_b = causal_softmax_attn(q_m, k_m, v_m)

        swa = proj[..., OFF_SWA:OFF_SWA + SWA_COLS]
        kvw = SWA_KV_HEADS * HEAD_DIM
        q_c = swa[..., :GROUP_WIDTH].reshape(B, S, H, HEAD_DIM)
        k_c = swa[..., GROUP_WIDTH:GROUP_WIDTH + kvw].reshape(B, S, SWA_KV_HEADS, HEAD_DIM)
        v_c = swa[..., GROUP_WIDTH + kvw:].reshape(B, S, SWA_KV_HEADS, HEAD_DIM)
        q_c = rope(rms_norm(q_c, swa_q_gain[i]), positions)
        k_c = rope(rms_norm(k_c, swa_k_gain[i]), positions)
        o_c = sliding_window_sink_attn(q_c, k_c, v_c, swa_sinks[i])

        mb = proj[..., OFF_MOBA:OFF_MOBA + MOBA_COLS].reshape(B, S, 3, H, HEAD_DIM)
        q_d = rope(rms_norm(mb[:, :, 0], moba_q_gain[i]), positions)
        k_d = rope(rms_norm(mb[:, :, 1], moba_k_gain[i]), positions)
        o_d = moba_attn(q_d, k_d, mb[:, :, 2])

        groups = jnp.stack([o_a.reshape(B, S, GROUP_WIDTH), o_b.reshape(B, S, GROUP_WIDTH),
                            o_c.reshape(B, S, GROUP_WIDTH), o_d.reshape(B, S, GROUP_WIDTH)], axis=2)
        groups = rms_norm(groups, group_norm[i].reshape(N_GROUPS, GROUP_WIDTH))
        x = x + groups.reshape(B, S, MIX_WIDTH) @ w_o[i]

        h = rms_norm(x, ffn_norm[i])
        u = causal_depthwise_conv(h @ w_up[i], conv_w[i], conv_b[i])
        x = x + (jax.nn.silu(u[..., :D_FF]) * u[..., D_FF:]) @ w_down[i]

        x = x + (p[i] @ ple_proj[i]) * jax.nn.sigmoid(x @ ple_gate[i])
    return x
```

```python
import functools

import jax
import jax.numpy as jnp
from jax import lax
from jax.experimental import pallas as pl
from jax.experimental.pallas import tpu as pltpu

F32 = jnp.float32
BF16 = jnp.bfloat16

D_MODEL = 1024
HEAD_DIM = 64
GROUP_HEADS = 4
GROUP_WIDTH = GROUP_HEADS * HEAD_DIM
N_GROUPS = 4
ROPE_THETA = 10000.0
EPS = 1e-6
NEG = -1e30

MLA_Q_RANK = 192
MLA_KV_RANK = 128
MLA_NOPE = 64
MLA_ROPE = 32
MLA_V = 64
MLA_QK = MLA_NOPE + MLA_ROPE
SWA_KV_HEADS = 2
WINDOW = 128
MOBA_BLOCK = 256
MOBA_TOPK = 3
D_FF = 2816
CONV_WIDTH = 3
PLE_DIM = 256

SB_COLS = 3 * GROUP_WIDTH
MLA_COLS = MLA_Q_RANK + MLA_KV_RANK + MLA_ROPE
SWA_COLS = GROUP_WIDTH + 2 * SWA_KV_HEADS * HEAD_DIM
MOBA_COLS = 3 * GROUP_WIDTH
OFF_SB = 0
OFF_MLA = OFF_SB + SB_COLS
OFF_SWA = OFF_MLA + MLA_COLS
OFF_MOBA = OFF_SWA + SWA_COLS

LANES = 128
SEG = LANES
Q_PAD = GROUP_HEADS * SEG
N_IN_PAD = 2560
VMEM_LIMIT = 56 * 1024 * 1024

PREP_TS = 256
OUT_TS = 512
FFN_TS = 512
FFN_FC = 256
SB_T = 128
SB_EXIT = -45.0
MLA_T = 256
HALO = 8


def _dot(a, b):
    return jnp.dot(a, b, preferred_element_type=F32)


def _dot_nt(a, b):
    return lax.dot_general(a, b, (((1,), (1,)), ((), ())), preferred_element_type=F32)


def _split_bf16(x):
    hi = x.astype(BF16)
    lo = (x - hi.astype(F32)).astype(BF16)
    return hi, lo


def _cparams(sem):
    return pltpu.CompilerParams(dimension_semantics=sem, vmem_limit_bytes=VMEM_LIMIT)


def _const_spec(shape):
    nd = len(shape)
    return pl.BlockSpec(shape, lambda *_: (0,) * nd)


def _rope_tables_kernel(pos_ref, inv_ref, c64_ref, s64_ref, cm_ref, sm_ref):
    pos = pos_ref[0].astype(F32)
    lane = lax.broadcasted_iota(jnp.int32, (1, LANES), 1)
    ang = pos * inv_ref[0:1, :]
    first = (lane & (HEAD_DIM - 1)) < (HEAD_DIM // 2)
    c64_ref[0] = jnp.cos(ang)
    s64_ref[0] = jnp.where(first, -jnp.sin(ang), jnp.sin(ang))
    angm = pos * inv_ref[1:2, :]
    in_rope = (lane >= MLA_NOPE) & (lane < MLA_QK)
    rope_first = lane < MLA_NOPE + MLA_ROPE // 2
    cm_ref[0] = jnp.where(in_rope, jnp.cos(angm), 1.0)
    sm_ref[0] = jnp.where(in_rope, jnp.where(rope_first, -jnp.sin(angm), jnp.sin(angm)), 0.0)


def _rope_tables(positions):
    B, S = positions.shape
    ts = 512
    half64 = HEAD_DIM // 2
    halfm = MLA_ROPE // 2
    inv64 = ROPE_THETA ** (-jnp.arange(half64, dtype=F32) / half64)
    invm = ROPE_THETA ** (-jnp.arange(halfm, dtype=F32) / halfm)
    lane = jnp.arange(LANES)
    row0 = inv64[lane % half64]
    row1 = jnp.where((lane >= MLA_NOPE) & (lane < MLA_QK), invm[(lane - MLA_NOPE) % halfm], 0.0)
    inv = jnp.zeros((8, LANES), F32).at[0].set(row0).at[1].set(row1)
    tab = jax.ShapeDtypeStruct((B, S, LANES), F32)
    spec = pl.BlockSpec((1, ts, LANES), lambda b, i: (b, i, 0))
    return pl.pallas_call(
        _rope_tables_kernel,
        grid=(B, S // ts),
        in_specs=[pl.BlockSpec((1, ts, 1), lambda b, i: (b, i, 0)), _const_spec((8, LANES))],
        out_specs=[spec] * 4,
        out_shape=[tab] * 4,
        compiler_params=_cparams(("parallel", "parallel")),
        name="rope_tables",
    )(positions.reshape(B, S, 1), inv)


def _rope64(x, cos, sin_signed, lane):
    first = (lane & (HEAD_DIM - 1)) < (HEAD_DIM // 2)
    partner = jnp.where(first, pltpu.roll(x, LANES - HEAD_DIM // 2, 1), pltpu.roll(x, HEAD_DIM // 2, 1))
    return x * cos + partner * sin_signed


def _rope_mla(x, cos, sin_signed, lane):
    first = lane < MLA_NOPE + MLA_ROPE // 2
    partner = jnp.where(first, pltpu.roll(x, LANES - MLA_ROPE // 2, 1), pltpu.roll(x, MLA_ROPE // 2, 1))
    return x * cos + partner * sin_signed


def _ms64(x, bsel):
    hi, lo = _split_bf16(x * x)
    return (_dot(hi, bsel) + _dot(lo, bsel)) * (1.0 / HEAD_DIM)


def _prep_kernel(x_ref, an_ref, win_ref, c64_ref, s64_ref, cm_ref, sm_ref,
                 qn_ref, wuq_ref, kvn_ref, wukv_ref, qg_ref, kg_ref, g64_ref,
                 qa_ref, ka_ref, va_ref, qm_ref, km_ref, vm_ref, qc_ref, kc_ref, vc_ref,
                 qd_ref, qd32_ref, kd_ref, vd_ref, kmean_ref):
    ts = x_ref.shape[1]
    x = x_ref[0]
    ms = jnp.mean(x * x, axis=-1, keepdims=True)
    h = (x * lax.rsqrt(ms + EPS) * an_ref[...]).astype(BF16)
    proj = _dot(h, win_ref[...])

    lane = lax.broadcasted_iota(jnp.int32, (1, LANES), 1)
    lo_half = lane < HEAD_DIM
    r = lax.broadcasted_iota(jnp.int32, (LANES, LANES), 0) // HEAD_DIM
    c = lax.broadcasted_iota(jnp.int32, (LANES, LANES), 1) // HEAD_DIM
    bsel = jnp.where(r == c, 1.0, 0.0).astype(BF16)
    c64, s64, cm, sm = c64_ref[0], s64_ref[0], cm_ref[0], sm_ref[0]
    scale64 = HEAD_DIM ** -0.5

    def seg(off, j):
        return proj[:, off + j * SEG: off + (j + 1) * SEG]

    def store_pair(ref, val, seg_lo, seg_hi):
        ref[0, :, seg_lo * SEG:(seg_lo + 1) * SEG] = jnp.where(lo_half, val, 0.0).astype(BF16)
        ref[0, :, seg_hi * SEG:(seg_hi + 1) * SEG] = jnp.where(lo_half, 0.0, val).astype(BF16)

    for p in range(2):
        store_pair(qa_ref, seg(0, p) * scale64, 2 * p, 2 * p + 1)
    ka_ref[0] = proj[:, 256:512].astype(BF16)
    va_ref[0] = proj[:, 512:768].astype(BF16)

    cq = proj[:, 768:1024]
    cqn = (cq * lax.rsqrt(jnp.sum(cq * cq, axis=-1, keepdims=True) * (1.0 / MLA_Q_RANK) + EPS)
           * qn_ref[...]).astype(BF16)
    qm_raw = _dot(cqn, wuq_ref[...])
    ckv = proj[:, 1024:1152]
    ckvn = (ckv * lax.rsqrt(jnp.mean(ckv * ckv, axis=-1, keepdims=True) + EPS) * kvn_ref[...]).astype(BF16)
    kv = _dot(ckvn, wukv_ref[...])
    kpe = proj[:, 1152:1280]
    vm_ref[0] = kv[:, 512:768].astype(BF16)
    scale_m = MLA_QK ** -0.5
    for hh in range(GROUP_HEADS):
        sl = slice(hh * SEG, (hh + 1) * SEG)
        qs = qm_raw[:, sl]
        qs = qs * lax.rsqrt(jnp.sum(qs * qs, axis=-1, keepdims=True) * (1.0 / MLA_QK) + EPS) * qg_ref[...]
        qm_ref[0, :, sl] = (_rope_mla(qs, cm, sm, lane) * scale_m).astype(BF16)
        ks = kv[:, sl] + kpe
        ks = ks * lax.rsqrt(jnp.sum(ks * ks, axis=-1, keepdims=True) * (1.0 / MLA_QK) + EPS) * kg_ref[...]
        km_ref[0, :, sl] = _rope_mla(ks, cm, sm, lane).astype(BF16)

    def norm_rope64(v, gain_row):
        v = v * lax.rsqrt(_ms64(v, bsel) + EPS) * g64_ref[gain_row:gain_row + 1, :]
        return _rope64(v, c64, s64, lane)

    qa_pair = norm_rope64(seg(1280, 0), 0) * scale64
    qb_pair = norm_rope64(seg(1280, 1), 0) * scale64
    store_pair(qc_ref, qa_pair, 0, 2)
    store_pair(qc_ref, qb_pair, 1, 3)
    kc_ref[0] = norm_rope64(seg(1536, 0), 1).astype(BF16)
    vc_ref[0] = seg(1664, 0).astype(BF16)

    for p in range(2):
        qd = norm_rope64(seg(1792, p), 2)
        qd32_ref[0, :, p * SEG:(p + 1) * SEG] = qd
        store_pair(qd_ref, qd * scale64, 2 * p, 2 * p + 1)
        kd = norm_rope64(seg(2048, p), 3)
        kd_ref[0, :, p * SEG:(p + 1) * SEG] = kd.astype(BF16)
        for blk in range(ts // MOBA_BLOCK):
            kmean_ref[0, blk, :, p * SEG:(p + 1) * SEG] = jnp.mean(
                kd[blk * MOBA_BLOCK:(blk + 1) * MOBA_BLOCK], axis=0, keepdims=True)
    vd_ref[0] = proj[:, 2304:2560].astype(BF16)


def _prep_weights(w_in, mla_q_norm, mla_w_uq, mla_kv_norm, mla_w_ukv, mla_q_gain, mla_k_gain,
                  swa_q_gain, swa_k_gain, moba_q_gain, moba_k_gain):
    D = w_in.shape[0]
    z = lambda n: jnp.zeros((D, n), F32)
    mla = w_in[:, OFF_MLA:OFF_MLA + MLA_COLS]
    swa = w_in[:, OFF_SWA:OFF_SWA + SWA_COLS]
    swa_q = swa[:, :GROUP_WIDTH].reshape(D, GROUP_HEADS, HEAD_DIM)[:, jnp.array([0, 2, 1, 3])].reshape(D, GROUP_WIDTH)
    win = jnp.concatenate([
        w_in[:, OFF_SB:OFF_SB + SB_COLS],
        mla[:, :MLA_Q_RANK], z(256 - MLA_Q_RANK),
        mla[:, MLA_Q_RANK:MLA_Q_RANK + MLA_KV_RANK],
        z(MLA_NOPE), mla[:, MLA_Q_RANK + MLA_KV_RANK:], z(SEG - MLA_QK),
        swa_q, swa[:, GROUP_WIDTH:],
        w_in[:, OFF_MOBA:OFF_MOBA + MOBA_COLS],
    ], axis=1).astype(BF16)
    assert win.shape[1] == N_IN_PAD
    qn = jnp.pad(mla_q_norm, (0, 256 - MLA_Q_RANK)).reshape(1, 256)
    wuq = mla_w_uq.reshape(MLA_Q_RANK, GROUP_HEADS, MLA_QK)
    wuq = jnp.pad(wuq, ((0, 256 - MLA_Q_RANK), (0, 0), (0, SEG - MLA_QK))).reshape(256, Q_PAD).astype(BF16)
    wukv = mla_w_ukv.reshape(MLA_KV_RANK, GROUP_HEADS, MLA_NOPE + MLA_V)
    wkn = jnp.pad(wukv[:, :, :MLA_NOPE], ((0, 0), (0, 0), (0, SEG - MLA_NOPE))).reshape(MLA_KV_RANK, Q_PAD)
    wv = wukv[:, :, MLA_NOPE:].reshape(MLA_KV_RANK, GROUP_WIDTH)
    wukv_r = jnp.concatenate([wkn, wv], axis=1).astype(BF16)
    qg = jnp.pad(mla_q_gain, (0, SEG - MLA_QK)).reshape(1, SEG)
    kg = jnp.pad(mla_k_gain, (0, SEG - MLA_QK)).reshape(1, SEG)
    g64 = jnp.zeros((8, LANES), F32)
    for row, g in enumerate((swa_q_gain, swa_k_gain, moba_q_gain, moba_k_gain)):
        g64 = g64.at[row].set(jnp.tile(g, LANES // HEAD_DIM))
    return win, qn, wuq, mla_kv_norm.reshape(1, MLA_KV_RANK), wukv_r, qg, kg, g64


def _prep(x, attn_norm, tables, weights):
    B, S, D = x.shape
    ts = PREP_TS
    win, qn, wuq, kvn, wukv, qg, kg, g64 = weights
    tok = lambda w: pl.BlockSpec((1, ts, w), lambda b, i: (b, i, 0))
    nb_t = ts // MOBA_BLOCK
    out_widths = [(Q_PAD, BF16), (256, BF16), (256, BF16),
                  (Q_PAD, BF16), (Q_PAD, BF16), (256, BF16),
                  (Q_PAD, BF16), (128, BF16), (128, BF16),
                  (Q_PAD, BF16), (256, F32), (256, BF16), (256, BF16)]
    out_shape = [jax.ShapeDtypeStruct((B, S, w), dt) for w, dt in out_widths]
    out_specs = [tok(w) for w, _ in out_widths]
    out_shape.append(jax.ShapeDtypeStruct((B, S // MOBA_BLOCK, 1, 256), F32))
    out_specs.append(pl.BlockSpec((1, nb_t, 1, 256), lambda b, i: (b, i, 0, 0)))
    consts = [attn_norm.reshape(1, D), win]
    tail = [qn, wuq, kvn, wukv, qg, kg, g64]
    in_specs = ([tok(D)] + [_const_spec(a.shape) for a in consts] + [tok(LANES)] * 4
                + [_const_spec(a.shape) for a in tail])
    return pl.pallas_call(
        _prep_kernel,
        grid=(B, S // ts),
        in_specs=in_specs,
        out_specs=out_specs,
        out_shape=out_shape,
        compiler_params=_cparams(("parallel", "parallel")),
        name="prep",
    )(x, *consts, *tables, *tail)


def _sb_kernel(q_ref, k_ref, v_ref, o_ref, acc_ref, carry_ref):
    T = SB_T
    i = pl.program_id(1)
    row = lax.broadcasted_iota(jnp.int32, (T, T), 0)
    col = lax.broadcasted_iota(jnp.int32, (T, T), 1)
    upper = jnp.where(row > col, 1.0, 0.0).astype(BF16)
    causal = col < row
    lane = lax.broadcasted_iota(jnp.int32, (1, LANES), 1)

    def block(qh, pair, j, masked):
        ks = pl.ds(pl.multiple_of(j * T, T), T)
        k = k_ref[0, ks, pair * SEG:(pair + 1) * SEG]
        v = v_ref[0, ks, pair * SEG:(pair + 1) * SEG]
        z = _dot_nt(qh, k)
        sp = jnp.log(1.0 + jnp.exp(-jnp.abs(z)))
        log_beta = jnp.minimum(z, 0.0) - sp
        log_keep = log_beta - z
        if masked:
            log_keep = jnp.where(causal, log_keep, 0.0)
        hi, lo = _split_bf16(log_keep)
        tail = _dot(hi, upper) + _dot(lo, upper)
        return log_beta, tail, jnp.sum(log_keep, axis=1, keepdims=True), v

    for hh in range(GROUP_HEADS):
        pair = hh // 2
        qh = q_ref[0, :, hh * SEG:(hh + 1) * SEG]
        log_beta, tail, rowsum, v = block(qh, pair, i, True)
        w = jnp.where(causal, jnp.exp(log_beta + tail), 0.0).astype(BF16)
        acc_ref[hh] = _dot(w, v)
        carry_ref[...] = rowsum

        def cond(st):
            j, mx = st
            return jnp.logical_and(j >= 0, mx > SB_EXIT)

        def body(st, qh=qh, pair=pair, hh=hh):
            j, _ = st
            log_beta, tail, rowsum, v = block(qh, pair, j, False)
            carry = carry_ref[...]
            w = jnp.exp(log_beta + (tail + carry)).astype(BF16)
            acc_ref[hh] += _dot(w, v)
            carry = carry + rowsum
            carry_ref[...] = carry
            return j - 1, jnp.max(carry)

        lax.while_loop(cond, body, (i - 1, jnp.max(rowsum)))

    lo_half = lane < HEAD_DIM
    for p in range(2):
        o_ref[0, :, p * SEG:(p + 1) * SEG] = jnp.where(lo_half, acc_ref[2 * p], acc_ref[2 * p + 1]).astype(o_ref.dtype)


def _sb_attn(q, k, v):
    B, S, _ = k.shape
    T = SB_T
    full = pl.BlockSpec((1, S, GROUP_WIDTH), lambda b, i: (b, 0, 0))
    return pl.pallas_call(
        _sb_kernel,
        grid=(B, S // T),
        in_specs=[pl.BlockSpec((1, T, Q_PAD), lambda b, i: (b, i, 0)), full, full],
        out_specs=pl.BlockSpec((1, T, GROUP_WIDTH), lambda b, i: (b, i, 0)),
        out_shape=jax.ShapeDtypeStruct((B, S, GROUP_WIDTH), BF16),
        scratch_shapes=[pltpu.VMEM((GROUP_HEADS, T, SEG), F32), pltpu.VMEM((T, 1), F32)],
        compiler_params=_cparams(("parallel", "arbitrary")),
        name="sb_attn",
    )(q, k, v)


def _mla_kernel(q_ref, k_ref, v_ref, o_ref, acc_ref, m_ref, l_ref):
    T = MLA_T
    i = pl.program_id(1)
    row = lax.broadcasted_iota(jnp.int32, (T, T), 0)
    col = lax.broadcasted_iota(jnp.int32, (T, T), 1)
    causal = col <= row
    lane = lax.broadcasted_iota(jnp.int32, (1, LANES), 1)

    def step(qh, hh, j, masked):
        ks = pl.ds(pl.multiple_of(j * T, T), T)
        k = k_ref[0, ks, hh * SEG:(hh + 1) * SEG]
        v = v_ref[0, ks, (hh // 2) * SEG:(hh // 2 + 1) * SEG]
        s = _dot_nt(qh, k)
        if masked:
            s = jnp.where(causal, s, NEG)
        m_old = m_ref[...]
        m_new = jnp.maximum(m_old, jnp.max(s, axis=1, keepdims=True))
        alpha = jnp.exp(m_old - m_new)
        p = jnp.exp(s - m_new)
        l_ref[...] = alpha * l_ref[...] + jnp.sum(p, axis=1, keepdims=True)
        acc_ref[hh] = alpha * acc_ref[hh] + _dot(p.astype(BF16), v)
        m_ref[...] = m_new

    for hh in range(GROUP_HEADS):
        qh = q_ref[0, :, hh * SEG:(hh + 1) * SEG]
        m_ref[...] = jnp.full(m_ref.shape, NEG, F32)
        l_ref[...] = jnp.zeros(l_ref.shape, F32)
        acc_ref[hh] = jnp.zeros((T, SEG), F32)

        def body(j, carry, qh=qh, hh=hh):
            step(qh, hh, j, False)
            return carry

        lax.fori_loop(0, i, body, 0)
        step(qh, hh, i, True)
        acc_ref[hh] = acc_ref[hh] / l_ref[...]

    lo_half = lane < HEAD_DIM
    for p in range(2):
        o_ref[0, :, p * SEG:(p + 1) * SEG] = jnp.where(lo_half, acc_ref[2 * p], acc_ref[2 * p + 1]).astype(o_ref.dtype)


def _mla_attn(q, k, v):
    B, S, _ = k.shape
    T = MLA_T
    return pl.pallas_call(
        _mla_kernel,
        grid=(B, S // T),
        in_specs=[pl.BlockSpec((1, T, Q_PAD), lambda b, i: (b, i, 0)),
                  pl.BlockSpec((1, S, Q_PAD), lambda b, i: (b, 0, 0)),
                  pl.BlockSpec((1, S, GROUP_WIDTH), lambda b, i: (b, 0, 0))],
        out_specs=pl.BlockSpec((1, T, GROUP_WIDTH), lambda b, i: (b, i, 0)),
        out_shape=jax.ShapeDtypeStruct((B, S, GROUP_WIDTH), BF16),
        scratch_shapes=[pltpu.VMEM((GROUP_HEADS, T, SEG), F32), pltpu.VMEM((T, 1), F32), pltpu.VMEM((T, 1), F32)],
        compiler_params=_cparams(("parallel", "arbitrary")),
        name="mla_attn",
    )(q, k, v)


def _swa_kernel(sink_ref, q_ref, k_ref, v_ref, o_ref):
    W = WINDOW
    i = pl.program_id(1)
    start = pl.multiple_of(jnp.maximum(i - 1, 0) * W, W)
    k = k_ref[0, pl.ds(start, 2 * W), :]
    v = v_ref[0, pl.ds(start, 2 * W), :]
    qpos = i * W + lax.broadcasted_iota(jnp.int32, (W, 2 * W), 0)
    kpos = start + lax.broadcasted_iota(jnp.int32, (W, 2 * W), 1)
    band = jnp.logical_and(kpos <= qpos, qpos - kpos < W)
    lane = lax.broadcasted_iota(jnp.int32, (1, LANES), 1)
    lo_half = lane < HEAD_DIM
    outs = []
    for hh in range(GROUP_HEADS):
        s = _dot_nt(q_ref[0, :, hh * SEG:(hh + 1) * SEG], k)
        s = jnp.where(band, s, NEG)
        sink = sink_ref[hh]
        m = jnp.maximum(jnp.max(s, axis=1, keepdims=True), sink)
        p = jnp.exp(s - m)
        denom = jnp.sum(p, axis=1, keepdims=True) + jnp.exp(sink - m)
        outs.append(_dot(p.astype(BF16), v) / denom)
    o_ref[0, :, 0:SEG] = jnp.where(lo_half, outs[0], pltpu.roll(outs[1], HEAD_DIM, 1)).astype(o_ref.dtype)
    o_ref[0, :, SEG:2 * SEG] = jnp.where(lo_half, pltpu.roll(outs[2], HEAD_DIM, 1), outs[3]).astype(o_ref.dtype)


def _swa_attn(q, k, v, sinks):
    B, S, _ = k.shape
    W = WINDOW
    full = pl.BlockSpec((1, S, 2 * HEAD_DIM), lambda b, i: (b, 0, 0))
    return pl.pallas_call(
        _swa_kernel,
        grid=(B, S // W),
        in_specs=[pl.BlockSpec(memory_space=pltpu.SMEM),
                  pl.BlockSpec((1, W, Q_PAD), lambda b, i: (b, i, 0)), full, full],
        out_specs=pl.BlockSpec((1, W, GROUP_WIDTH), lambda b, i: (b, i, 0)),
        out_shape=jax.ShapeDtypeStruct((B, S, GROUP_WIDTH), BF16),
        compiler_params=_cparams(("parallel", "parallel")),
        name="swa_attn",
    )(sinks, q, k, v)


def _moba_kernel(q_ref, q32_ref, k_ref, v_ref, kmean_ref, o_ref, acc_ref, m_ref, l_ref, sel_ref):
    T = MOBA_BLOCK
    own = pl.program_id(1)
    nb = kmean_ref.shape[1]
    row = lax.broadcasted_iota(jnp.int32, (T, T), 0)
    col = lax.broadcasted_iota(jnp.int32, (T, T), 1)
    causal = col <= row
    lane = lax.broadcasted_iota(jnp.int32, (1, LANES), 1)
    lo_half = lane < HEAD_DIM
    blk = lax.broadcasted_iota(jnp.int32, (T, nb), 1).astype(F32)
    erow = lax.broadcasted_iota(jnp.int32, (nb, T), 0)
    own_f = own.astype(F32)

    def update(hh, s, v):
        m_old = m_ref[...]
        m_new = jnp.maximum(m_old, jnp.max(s, axis=1, keepdims=True))
        alpha = jnp.exp(m_old - m_new)
        p = jnp.exp(s - m_new)
        l_ref[...] = alpha * l_ref[...] + jnp.sum(p, axis=1, keepdims=True)
        acc_ref[hh] = alpha * acc_ref[hh] + _dot(p.astype(BF16), v)
        m_ref[...] = m_new

    for hh in range(GROUP_HEADS):
        pair = hh // 2
        psl = slice(pair * SEG, (pair + 1) * SEG)
        head_lanes = lo_half if hh % 2 == 0 else jnp.logical_not(lo_half)

        q32 = jnp.where(head_lanes, q32_ref[0, :, psl], 0.0)
        qh_hi, qh_lo = _split_bf16(q32)
        km_hi, km_lo = _split_bf16(kmean_ref[0, :, psl])
        gate = _dot_nt(qh_hi, km_hi) + (_dot_nt(qh_hi, km_lo) + _dot_nt(qh_lo, km_hi))
        g = jnp.where(blk < own_f, gate, NEG)
        sel = jnp.zeros((T, nb), F32)
        for t in range(MOBA_TOPK):
            mx = jnp.max(g, axis=1, keepdims=True)
            first = jnp.min(jnp.where(g == mx, blk, float(nb)), axis=1, keepdims=True)
            hit = blk == first
            sel = jnp.where(jnp.logical_and(hit, t < own), 1.0, sel)
            g = jnp.where(hit, -jnp.inf, g)
        sel_ref[...] = sel.astype(BF16)

        qh = q_ref[0, :, hh * SEG:(hh + 1) * SEG]
        ks = pl.ds(pl.multiple_of(own * T, T), T)
        s = jnp.where(causal, _dot_nt(qh, k_ref[0, ks, psl]), NEG)
        m_ref[...] = jnp.full(m_ref.shape, NEG, F32)
        l_ref[...] = jnp.zeros(l_ref.shape, F32)
        acc_ref[hh] = jnp.zeros((T, SEG), F32)
        update(hh, s, v_ref[0, ks, psl])

        def body(n, carry, qh=qh, hh=hh, psl=psl):
            ks = pl.ds(pl.multiple_of(n * T, T), T)
            s = _dot_nt(qh, k_ref[0, ks, psl])
            onehot = jnp.where(erow == n, 1.0, 0.0).astype(BF16)
            chosen = _dot(sel_ref[...], onehot)
            update(hh, jnp.where(chosen > 0.5, s, NEG), v_ref[0, ks, psl])
            return carry

        lax.fori_loop(0, own, body, 0)
        acc_ref[hh] = acc_ref[hh] / l_ref[...]

    for p in range(2):
        o_ref[0, :, p * SEG:(p + 1) * SEG] = jnp.where(lo_half, acc_ref[2 * p], acc_ref[2 * p + 1]).astype(o_ref.dtype)


def _moba_attn(q, q32, k, v, kmean):
    B, S, _ = k.shape
    T = MOBA_BLOCK
    assert S // T <= LANES
    nb = LANES
    kmean = jnp.pad(kmean.reshape(B, S // T, GROUP_WIDTH), ((0, 0), (0, nb - S // T), (0, 0)))
    full = pl.BlockSpec((1, S, GROUP_WIDTH), lambda b, i: (b, 0, 0))
    return pl.pallas_call(
        _moba_kernel,
        grid=(B, S // T),
        in_specs=[pl.BlockSpec((1, T, Q_PAD), lambda b, i: (b, i, 0)),
                  pl.BlockSpec((1, T, GROUP_WIDTH), lambda b, i: (b, i, 0)),
                  full, full,
                  pl.BlockSpec((1, nb, GROUP_WIDTH), lambda b, i: (b, 0, 0))],
        out_specs=pl.BlockSpec((1, T, GROUP_WIDTH), lambda b, i: (b, i, 0)),
        out_shape=jax.ShapeDtypeStruct((B, S, GROUP_WIDTH), BF16),
        scratch_shapes=[pltpu.VMEM((GROUP_HEADS, T, SEG), F32), pltpu.VMEM((T, 1), F32), pltpu.VMEM((T, 1), F32),
                        pltpu.VMEM((T, nb), BF16)],
        compiler_params=_cparams(("parallel", "arbitrary")),
        name="moba_attn",
    )(q, q32, k, v, kmean)


def _oproj_kernel(oa_ref, ob_ref, oc_ref, od_ref, gn_ref, wo_ref, x_ref, y_ref):
    acc = x_ref[0]
    for g, ref in enumerate((oa_ref, ob_ref, oc_ref, od_ref)):
        o = ref[0].astype(F32)
        ms = jnp.mean(o * o, axis=-1, keepdims=True)
        n = (o * lax.rsqrt(ms + EPS) * gn_ref[:, g * GROUP_WIDTH:(g + 1) * GROUP_WIDTH]).astype(BF16)
        acc = acc + _dot(n, wo_ref[g * GROUP_WIDTH:(g + 1) * GROUP_WIDTH, :])
    y_ref[0] = acc


def _oproj(groups, group_norm, w_o, x):
    B, S, D = x.shape
    ts = OUT_TS
    tok = lambda w: pl.BlockSpec((1, ts, w), lambda b, i: (b, i, 0))
    return pl.pallas_call(
        _oproj_kernel,
        grid=(B, S // ts),
        in_specs=[tok(GROUP_WIDTH)] * 4 + [_const_spec((1, D)), _const_spec(w_o.shape), tok(D)],
        out_specs=tok(D),
        out_shape=jax.ShapeDtypeStruct((B, S, D), F32),
        compiler_params=_cparams(("parallel", "parallel")),
        name="oproj",
    )(*groups, group_norm.reshape(1, D), w_o, x)


def _ffn_kernel(x_ref, p_ref, fn_ref, wup_ref, cw_ref, cb_ref, wdn_ref, pproj_ref, pgate_ref, y_ref,
                halo_ref, ubuf_ref, acc_ref):
    ts = x_ref.shape[1]
    fc2 = wup_ref.shape[2]
    fc = fc2 // 2
    n_chunks = wup_ref.shape[0]
    i = pl.program_id(1)

    @pl.when(i == 0)
    def _():
        halo_ref[...] = jnp.zeros(halo_ref.shape, F32)

    x = x_ref[0]
    ms = jnp.mean(x * x, axis=-1, keepdims=True)
    h = (x * lax.rsqrt(ms + EPS) * fn_ref[...]).astype(BF16)
    acc_ref[...] = x

    def chunk(c, carry):
        u = _dot(h, wup_ref[c])
        ubuf_ref[0:HALO, :] = halo_ref[c]
        ubuf_ref[HALO:HALO + ts, :] = u
        halo_ref[c] = u[ts - HALO:ts, :]
        cw = cw_ref[c]
        y = (cw[0:1, :] * ubuf_ref[HALO - 2:HALO - 2 + ts, :] + cw[1:2, :] * ubuf_ref[HALO - 1:HALO - 1 + ts, :]
             + cw[2:3, :] * u + cb_ref[c])
        ya = y[:, :fc]
        g = ya * (1.0 / (1.0 + jnp.exp(-ya))) * y[:, fc:]
        acc_ref[...] += _dot(g.astype(BF16), wdn_ref[c])
        return carry

    lax.fori_loop(0, n_chunks, chunk, 0)
    x2 = acc_ref[...]
    gate = _dot(x2.astype(BF16), pgate_ref[...])
    emb = _dot(p_ref[0].astype(BF16), pproj_ref[...])
    y_ref[0] = x2 + emb * (1.0 / (1.0 + jnp.exp(-gate)))


def _ffn_weights(ffn_norm, w_up, conv_w, conv_b, w_down, ple_proj, ple_gate):
    fc = FFN_FC
    nc = D_FF // fc
    D = w_up.shape[0]
    pair = lambda a: jnp.concatenate([a[..., :D_FF].reshape(a.shape[:-1] + (nc, fc)),
                                      a[..., D_FF:].reshape(a.shape[:-1] + (nc, fc))], axis=-1)
    wup = jnp.moveaxis(pair(w_up), 1, 0).astype(BF16)
    cw = jnp.pad(jnp.moveaxis(pair(conv_w), 1, 0), ((0, 0), (0, 8 - CONV_WIDTH), (0, 0)))
    cb = jnp.moveaxis(pair(conv_b.reshape(1, -1)), 1, 0)
    wdn = w_down.reshape(nc, fc, D).astype(BF16)
    return ffn_norm.reshape(1, D), wup, cw, cb, wdn, ple_proj.astype(BF16), ple_gate.astype(BF16)


def _ffn(x, p, weights):
    B, S, D = x.shape
    ts = FFN_TS
    fn, wup, cw, cb, wdn, pproj, pgate = weights
    nc, _, fc2 = wup.shape
    tok = lambda w: pl.BlockSpec((1, ts, w), lambda b, i: (b, i, 0))
    once = lambda a: pl.BlockSpec(a.shape, lambda *_: (0,) * a.ndim, pipeline_mode=pl.Buffered(1))
    return pl.pallas_call(
        _ffn_kernel,
        grid=(B, S // ts),
        in_specs=[tok(D), tok(PLE_DIM)] + [once(a) for a in (fn, wup, cw, cb, wdn, pproj, pgate)],
        out_specs=tok(D),
        out_shape=jax.ShapeDtypeStruct((B, S, D), F32),
        scratch_shapes=[pltpu.VMEM((nc, HALO, fc2), F32), pltpu.VMEM((ts + HALO, fc2), F32),
                        pltpu.VMEM((ts, D), F32)],
        compiler_params=_cparams(("arbitrary", "arbitrary")),
        name="ffn_ple",
    )(x, p, fn, wup, cw, cb, wdn, pproj, pgate)


def kernel(x, p, positions, attn_norm, w_in, mla_q_norm, mla_w_uq, mla_kv_norm, mla_w_ukv, mla_q_gain, mla_k_gain, swa_q_gain, swa_k_gain, swa_sinks, moba_q_gain, moba_k_gain, group_norm, w_o, ffn_norm, w_up, conv_w, conv_b, w_down, ple_proj, ple_gate):
    B, S, D = x.shape
    depth = w_in.shape[0]
    assert D == D_MODEL and S % 512 == 0
    tables = _rope_tables(positions)
    for i in range(depth):
        pw = _prep_weights(w_in[i], mla_q_norm[i], mla_w_uq[i], mla_kv_norm[i], mla_w_ukv[i], mla_q_gain[i],
                           mla_k_gain[i], swa_q_gain[i], swa_k_gain[i], moba_q_gain[i], moba_k_gain[i])
        (qa, ka, va, qm, km, vm, qc, kc, vc, qd, qd32, kd, vd, kmean) = _prep(x, attn_norm[i], tables, pw)
        o_a = _sb_attn(qa, ka, va)
        o_b = _mla_attn(qm, km, vm)
        o_c = _swa_attn(qc, kc, vc, swa_sinks[i])
        o_d = _moba_attn(qd, qd32, kd, vd, kmean)
        x = _oproj((o_a, o_b, o_c, o_d), group_norm[i], w_o[i].astype(BF16), x)
        fw = _ffn_weights(ffn_norm[i], w_up[i], conv_w[i], conv_b[i], w_down[i], ple_proj[i], ple_gate[i])
        x = _ffn(x, p[i], fw)
    return x
```

```python
import functools

import jax
import jax.numpy as jnp
from jax import lax
from jax.experimental import pallas as pl
from jax.experimental.pallas import tpu as pltpu

F32 = jnp.float32
BF16 = jnp.bfloat16

D_MODEL = 1024
HEAD_DIM = 64
GROUP_HEADS = 4
GROUP_WIDTH = GROUP_HEADS * HEAD_DIM
N_GROUPS = 4
ROPE_THETA = 10000.0
EPS = 1e-6
NEG = -1e30

MLA_Q_RANK = 192
MLA_KV_RANK = 128
MLA_NOPE = 64
MLA_ROPE = 32
MLA_V = 64
MLA_QK = MLA_NOPE + MLA_ROPE
SWA_KV_HEADS = 2
WINDOW = 128
MOBA_BLOCK = 256
MOBA_TOPK = 3
D_FF = 2816
CONV_WIDTH = 3
PLE_DIM = 256

SB_COLS = 3 * GROUP_WIDTH
MLA_COLS = MLA_Q_RANK + MLA_KV_RANK + MLA_ROPE
SWA_COLS = GROUP_WIDTH + 2 * SWA_KV_HEADS * HEAD_DIM
MOBA_COLS = 3 * GROUP_WIDTH
OFF_SB = 0
OFF_MLA = OFF_SB + SB_COLS
OFF_SWA = OFF_MLA + MLA_COLS
OFF_MOBA = OFF_SWA + SWA_COLS

LANES = 128
SEG = LANES
Q_PAD = GROUP_HEADS * SEG
N_IN_PAD = 2560
VMEM_LIMIT = 56 * 1024 * 1024

PREP_TS = 256
OUT_TS = 512
FFN_TS = 512
FFN_FC = 256
SB_T = 128
SB_EXIT = -45.0
MLA_T = 256
HALO = 8


def _dot(a, b):
    return jnp.dot(a, b, preferred_element_type=F32)


def _dot_nt(a, b):
    return lax.dot_general(a, b, (((1,), (1,)), ((), ())), preferred_element_type=F32)


def _split_bf16(x):
    hi = x.astype(BF16)
    lo = (x - hi.astype(F32)).astype(BF16)
    return hi, lo


def _cparams(sem):
    return pltpu.CompilerParams(dimension_semantics=sem, vmem_limit_bytes=VMEM_LIMIT)


def _const_spec(shape):
    nd = len(shape)
    return pl.BlockSpec(shape, lambda *_: (0,) * nd)


def _rope_tables_kernel(pos_ref, inv_ref, c64_ref, s64_ref, cm_ref, sm_ref):
    pos = pos_ref[0].astype(F32)
    lane = lax.broadcasted_iota(jnp.int32, (1, LANES), 1)
    ang = pos * inv_ref[0:1, :]
    first = (lane & (HEAD_DIM - 1)) < (HEAD_DIM // 2)
    c64_ref[0] = jnp.cos(ang)
    s64_ref[0] = jnp.where(first, -jnp.sin(ang), jnp.sin(ang))
    angm = pos * inv_ref[1:2, :]
    in_rope = (lane >= MLA_NOPE) & (lane < MLA_QK)
    rope_first = lane < MLA_NOPE + MLA_ROPE // 2
    cm_ref[0] = jnp.where(in_rope, jnp.cos(angm), 1.0)
    sm_ref[0] = jnp.where(in_rope, jnp.where(rope_first, -jnp.sin(angm), jnp.sin(angm)), 0.0)


def _rope_tables(positions):
    B, S = positions.shape
    ts = 512
    half64 = HEAD_DIM // 2
    halfm = MLA_ROPE // 2
    inv64 = ROPE_THETA ** (-jnp.arange(half64, dtype=F32) / half64)
    invm = ROPE_THETA ** (-jnp.arange(halfm, dtype=F32) / halfm)
    lane = jnp.arange(LANES)
    row0 = inv64[lane % half64]
    row1 = jnp.where((lane >= MLA_NOPE) & (lane < MLA_QK), invm[(lane - MLA_NOPE) % halfm], 0.0)
    inv = jnp.zeros((8, LANES), F32).at[0].set(row0).at[1].set(row1)
    tab = jax.ShapeDtypeStruct((B, S, LANES), F32)
    spec = pl.BlockSpec((1, ts, LANES), lambda b, i: (b, i, 0))
    return pl.pallas_call(
        _rope_tables_kernel,
        grid=(B, S // ts),
        in_specs=[pl.BlockSpec((1, ts, 1), lambda b, i: (b, i, 0)), _const_spec((8, LANES))],
        out_specs=[spec] * 4,
        out_shape=[tab] * 4,
        compiler_params=_cparams(("parallel", "parallel")),
        name="rope_tables",
    )(positions.reshape(B, S, 1), inv)


def _rope64(x, cos, sin_signed, lane):
    first = (lane & (HEAD_DIM - 1)) < (HEAD_DIM // 2)
    partner = jnp.where(first, pltpu.roll(x, LANES - HEAD_DIM // 2, 1), pltpu.roll(x, HEAD_DIM // 2, 1))
    return x * cos + partner * sin_signed


def _rope_mla(x, cos, sin_signed, lane):
    first = lane < MLA_NOPE + MLA_ROPE // 2
    partner = jnp.where(first, pltpu.roll(x, LANES - MLA_ROPE // 2, 1), pltpu.roll(x, MLA_ROPE // 2, 1))
    return x * cos + partner * sin_signed


def _ms64(x, bsel):
    hi, lo = _split_bf16(x * x)
    return (_dot(hi, bsel) + _dot(lo, bsel)) * (1.0 / HEAD_DIM)


def _prep_kernel(x_ref, an_ref, win_ref, c64_ref, s64_ref, cm_ref, sm_ref,
                 qn_ref, wuq_ref, kvn_ref, wukv_ref, qg_ref, kg_ref, g64_ref,
                 qa_ref, ka_ref, va_ref, qm_ref, km_ref, vm_ref, qc_ref, kc_ref, vc_ref,
                 qd_ref, qd32_ref, kd_ref, vd_ref, kmean_ref):
    ts = x_ref.shape[1]
    x = x_ref[0]
    ms = jnp.mean(x * x, axis=-1, keepdims=True)
    h = (x * lax.rsqrt(ms + EPS) * an_ref[...]).astype(BF16)
    proj = _dot(h, win_ref[...])

    lane = lax.broadcasted_iota(jnp.int32, (1, LANES), 1)
    lo_half = lane < HEAD_DIM
    r = lax.broadcasted_iota(jnp.int32, (LANES, LANES), 0) // HEAD_DIM
    c = lax.broadcasted_iota(jnp.int32, (LANES, LANES), 1) // HEAD_DIM
    bsel = jnp.where(r == c, 1.0, 0.0).astype(BF16)
    c64, s64, cm, sm = c64_ref[0], s64_ref[0], cm_ref[0], sm_ref[0]
    scale64 = HEAD_DIM ** -0.5

    def seg(off, j):
        return proj[:, off + j * SEG: off + (j + 1) * SEG]

    def store_pair(ref, val, seg_lo, seg_hi):
        ref[0, :, seg_lo * SEG:(seg_lo + 1) * SEG] = jnp.where(lo_half, val, 0.0).astype(BF16)
        ref[0, :, seg_hi * SEG:(seg_hi + 1) * SEG] = jnp.where(lo_half, 0.0, val).astype(BF16)

    for p in range(2):
        store_pair(qa_ref, seg(0, p) * scale64, 2 * p, 2 * p + 1)
    ka_ref[0] = proj[:, 256:512].astype(BF16)
    va_ref[0] = proj[:, 512:768].astype(BF16)

    cq = proj[:, 768:1024]
    cqn = (cq * lax.rsqrt(jnp.sum(cq * cq, axis=-1, keepdims=True) * (1.0 / MLA_Q_RANK) + EPS)
           * qn_ref[...]).astype(BF16)
    qm_raw = _dot(cqn, wuq_ref[...])
    ckv = proj[:, 1024:1152]
    ckvn = (ckv * lax.rsqrt(jnp.mean(ckv * ckv, axis=-1, keepdims=True) + EPS) * kvn_ref[...]).astype(BF16)
    kv = _dot(ckvn, wukv_ref[...])
    kpe = proj[:, 1152:1280]
    vm_ref[0] = kv[:, 512:768].astype(BF16)
    scale_m = MLA_QK ** -0.5
    for hh in range(GROUP_HEADS):
        sl = slice(hh * SEG, (hh + 1) * SEG)
        qs = qm_raw[:, sl]
        qs = qs * lax.rsqrt(jnp.sum(qs * qs, axis=-1, keepdims=True) * (1.0 / MLA_QK) + EPS) * qg_ref[...]
        qm_ref[0, :, sl] = (_rope_mla(qs, cm, sm, lane) * scale_m).astype(BF16)
        ks = kv[:, sl] + kpe
        ks = ks * lax.rsqrt(jnp.sum(ks * ks, axis=-1, keepdims=True) * (1.0 / MLA_QK) + EPS) * kg_ref[...]
        km_ref[0, :, sl] = _rope_mla(ks, cm, sm, lane).astype(BF16)

    def norm_rope64(v, gain_row):
        v = v * lax.rsqrt(_ms64(v, bsel) + EPS) * g64_ref[gain_row:gain_row + 1, :]
        return _rope64(v, c64, s64, lane)

    qa_pair = norm_rope64(seg(1280, 0), 0) * scale64
    qb_pair = norm_rope64(seg(1280, 1), 0) * scale64
    store_pair(qc_ref, qa_pair, 0, 2)
    store_pair(qc_ref, qb_pair, 1, 3)
    kc_ref[0] = norm_rope64(seg(1536, 0), 1).astype(BF16)
    vc_ref[0] = seg(1664, 0).astype(BF16)

    for p in range(2):
        qd = norm_rope64(seg(1792, p), 2)
        qd32_ref[0, :, p * SEG:(p + 1) * SEG] = qd
        store_pair(qd_ref, qd * scale64, 2 * p, 2 * p + 1)
        kd = norm_rope64(seg(2048, p), 3)
        kd_ref[0, :, p * SEG:(p + 1) * SEG] = kd.astype(BF16)
        for blk in range(ts // MOBA_BLOCK):
            kmean_ref[0, blk, :, p * SEG:(p + 1) * SEG] = jnp.mean(
                kd[blk * MOBA_BLOCK:(blk + 1) * MOBA_BLOCK], axis=0, keepdims=True)
    vd_ref[0] = proj[:, 2304:2560].astype(BF16)


def _prep_weights(w_in, mla_q_norm, mla_w_uq, mla_kv_norm, mla_w_ukv, mla_q_gain, mla_k_gain,
                  swa_q_gain, swa_k_gain, moba_q_gain, moba_k_gain):
    D = w_in.shape[0]
    z = lambda n: jnp.zeros((D, n), F32)
    mla = w_in[:, OFF_MLA:OFF_MLA + MLA_COLS]
    swa = w_in[:, OFF_SWA:OFF_SWA + SWA_COLS]
    swa_q = swa[:, :GROUP_WIDTH].reshape(D, GROUP_HEADS, HEAD_DIM)[:, jnp.array([0, 2, 1, 3])].reshape(D, GROUP_WIDTH)
    win = jnp.concatenate([
        w_in[:, OFF_SB:OFF_SB + SB_COLS],
        mla[:, :MLA_Q_RANK], z(256 - MLA_Q_RANK),
        mla[:, MLA_Q_RANK:MLA_Q_RANK + MLA_KV_RANK],
        z(MLA_NOPE), mla[:, MLA_Q_RANK + MLA_KV_RANK:], z(SEG - MLA_QK),
        swa_q, swa[:, GROUP_WIDTH:],
        w_in[:, OFF_MOBA:OFF_MOBA + MOBA_COLS],
    ], axis=1).astype(BF16)
    assert win.shape[1] == N_IN_PAD
    qn = jnp.pad(mla_q_norm, (0, 256 - MLA_Q_RANK)).reshape(1, 256)
    wuq = mla_w_uq.reshape(MLA_Q_RANK, GROUP_HEADS, MLA_QK)
    wuq = jnp.pad(wuq, ((0, 256 - MLA_Q_RANK), (0, 0), (0, SEG - MLA_QK))).reshape(256, Q_PAD).astype(BF16)
    wukv = mla_w_ukv.reshape(MLA_KV_RANK, GROUP_HEADS, MLA_NOPE + MLA_V)
    wkn = jnp.pad(wukv[:, :, :MLA_NOPE], ((0, 0), (0, 0), (0, SEG - MLA_NOPE))).reshape(MLA_KV_RANK, Q_PAD)
    wv = wukv[:, :, MLA_NOPE:].reshape(MLA_KV_RANK, GROUP_WIDTH)
    wukv_r = jnp.concatenate([wkn, wv], axis=1).astype(BF16)
    qg = jnp.pad(mla_q_gain, (0, SEG - MLA_QK)).reshape(1, SEG)
    kg = jnp.pad(mla_k_gain, (0, SEG - MLA_QK)).reshape(1, SEG)
    g64 = jnp.zeros((8, LANES), F32)
    for row, g in enumerate((swa_q_gain, swa_k_gain, moba_q_gain, moba_k_gain)):
        g64 = g64.at[row].set(jnp.tile(g, LANES // HEAD_DIM))
    return win, qn, wuq, mla_kv_norm.reshape(1, MLA_KV_RANK), wukv_r, qg, kg, g64


def _prep(x, attn_norm, tables, weights):
    B, S, D = x.shape
    ts = PREP_TS
    win, qn, wuq, kvn, wukv, qg, kg, g64 = weights
    tok = lambda w: pl.BlockSpec((1, ts, w), lambda b, i: (b, i, 0))
    nb_t = ts // MOBA_BLOCK
    out_widths = [(Q_PAD, BF16), (256, BF16), (256, BF16),
                  (Q_PAD, BF16), (Q_PAD, BF16), (256, BF16),
                  (Q_PAD, BF16), (128, BF16), (128, BF16),
                  (Q_PAD, BF16), (256, F32), (256, BF16), (256, BF16)]
    out_shape = [jax.ShapeDtypeStruct((B, S, w), dt) for w, dt in out_widths]
    out_specs = [tok(w) for w, _ in out_widths]
    out_shape.append(jax.ShapeDtypeStruct((B, S // MOBA_BLOCK, 1, 256), F32))
    out_specs.append(pl.BlockSpec((1, nb_t, 1, 256), lambda b, i: (b, i, 0, 0)))
    consts = [attn_norm.reshape(1, D), win]
    tail = [qn, wuq, kvn, wukv, qg, kg, g64]
    in_specs = ([tok(D)] + [_const_spec(a.shape) for a in consts] + [tok(LANES)] * 4
                + [_const_spec(a.shape) for a in tail])
    return pl.pallas_call(
        _prep_kernel,
        grid=(B, S // ts),
        in_specs=in_specs,
        out_specs=out_specs,
        out_shape=out_shape,
        compiler_params=_cparams(("parallel", "parallel")),
        name="prep",
    )(x, *consts, *tables, *tail)


def _sb_kernel(q_ref, k_ref, v_ref, o_ref, acc_ref, carry_ref):
    T = SB_T
    i = pl.program_id(1)
    row = lax.broadcasted_iota(jnp.int32, (T, T), 0)
    col = lax.broadcasted_iota(jnp.int32, (T, T), 1)
    upper = jnp.where(row > col, 1.0, 0.0).astype(BF16)
    causal = col < row
    lane = lax.broadcasted_iota(jnp.int32, (1, LANES), 1)

    def block(qh, pair, j, masked):
        ks = pl.ds(pl.multiple_of(j * T, T), T)
        k = k_ref[0, ks, pair * SEG:(pair + 1) * SEG]
        v = v_ref[0, ks, pair * SEG:(pair + 1) * SEG]
        z = _dot_nt(qh, k)
        sp = jnp.log(1.0 + jnp.exp(-jnp.abs(z)))
        log_beta = jnp.minimum(z, 0.0) - sp
        log_keep = log_beta - z
        if masked:
            log_keep = jnp.where(causal, log_keep, 0.0)
        hi, lo = _split_bf16(log_keep)
        tail = _dot(hi, upper) + _dot(lo, upper)
        return log_beta, tail, jnp.sum(log_keep, axis=1, keepdims=True), v

    for hh in range(GROUP_HEADS):
        pair = hh // 2
        qh = q_ref[0, :, hh * SEG:(hh + 1) * SEG]
        log_beta, tail, rowsum, v = block(qh, pair, i, True)
        w = jnp.where(causal, jnp.exp(log_beta + tail), 0.0).astype(BF16)
        acc_ref[hh] = _dot(w, v)
        carry_ref[...] = rowsum

        def cond(st):
            j, mx = st
            return jnp.logical_and(j >= 0, mx > SB_EXIT)

        def body(st, qh=qh, pair=pair, hh=hh):
            j, _ = st
            log_beta, tail, rowsum, v = block(qh, pair, j, False)
            carry = carry_ref[...]
            w = jnp.exp(log_beta + (tail + carry)).astype(BF16)
            acc_ref[hh] += _dot(w, v)
            carry = carry + rowsum
            carry_ref[...] = carry
            return j - 1, jnp.max(carry)

        lax.while_loop(cond, body, (i - 1, jnp.max(rowsum)))

    lo_half = lane < HEAD_DIM
    for p in range(2):
        o_ref[0, :, p * SEG:(p + 1) * SEG] = jnp.where(lo_half, acc_ref[2 * p], acc_ref[2 * p + 1]).astype(o_ref.dtype)


def _sb_attn(q, k, v):
    B, S, _ = k.shape
    T = SB_T
    full = pl.BlockSpec((1, S, GROUP_WIDTH), lambda b, i: (b, 0, 0))
    return pl.pallas_call(
        _sb_kernel,
        grid=(B, S // T),
        in_specs=[pl.BlockSpec((1, T, Q_PAD), lambda b, i: (b, i, 0)), full, full],
        out_specs=pl.BlockSpec((1, T, GROUP_WIDTH), lambda b, i: (b, i, 0)),
        out_shape=jax.ShapeDtypeStruct((B, S, GROUP_WIDTH), BF16),
        scratch_shapes=[pltpu.VMEM((GROUP_HEADS, T, SEG), F32), pltpu.VMEM((T, 1), F32)],
        compiler_params=_cparams(("parallel", "arbitrary")),
        name="sb_attn",
    )(q, k, v)


def _softmax_update(acc_ref, m_ref, hh, s, v_aug):
    tk = s.shape[1]
    m_old = m_ref[hh]
    m_new = jnp.maximum(m_old, jnp.max(s, axis=1, keepdims=True))
    alpha = jnp.exp(m_old - m_new)
    p = jnp.concatenate([jnp.exp(s[:, c * LANES:(c + 1) * LANES] - m_new) for c in range(tk // LANES)], axis=1)
    acc_ref[hh] = alpha * acc_ref[hh] + _dot(p.astype(BF16), v_aug)
    m_ref[hh] = m_new


def _augment_v(v_pair, hh, lo_half):
    one = jnp.ones((), v_pair.dtype)
    return jnp.where(lo_half, v_pair, one) if hh % 2 == 0 else jnp.where(lo_half, one, v_pair)


def _softmax_finish(acc_ref, o_ref, lo_half):
    outs = []
    for hh in range(GROUP_HEADS):
        r = acc_ref[hh]
        outs.append(r / pltpu.roll(r, HEAD_DIM, 1))
    for p in range(2):
        o_ref[0, :, p * SEG:(p + 1) * SEG] = jnp.where(lo_half, outs[2 * p], outs[2 * p + 1]).astype(o_ref.dtype)


def _mla_kernel(q_ref, k_ref, v_ref, o_ref, acc_ref, m_ref):
    T = MLA_T
    i = pl.program_id(1)
    row = lax.broadcasted_iota(jnp.int32, (T, T), 0)
    col = lax.broadcasted_iota(jnp.int32, (T, T), 1)
    causal = col <= row
    lo_half = lax.broadcasted_iota(jnp.int32, (1, LANES), 1) < HEAD_DIM

    def step(j, masked):
        ks = pl.ds(pl.multiple_of(j * T, T), T)
        for hh in range(GROUP_HEADS):
            s = _dot_nt(q_ref[0, :, hh * SEG:(hh + 1) * SEG], k_ref[0, ks, hh * SEG:(hh + 1) * SEG])
            if masked:
                s = jnp.where(causal, s, NEG)
            v = _augment_v(v_ref[0, ks, (hh // 2) * SEG:(hh // 2 + 1) * SEG], hh, lo_half)
            _softmax_update(acc_ref, m_ref, hh, s, v)

    m_ref[...] = jnp.full(m_ref.shape, NEG, F32)
    acc_ref[...] = jnp.zeros(acc_ref.shape, F32)

    def body(j, carry):
        step(j, False)
        return carry

    lax.fori_loop(0, i, body, 0)
    step(i, True)
    _softmax_finish(acc_ref, o_ref, lo_half)


def _mla_attn(q, k, v):
    B, S, _ = k.shape
    T = MLA_T
    return pl.pallas_call(
        _mla_kernel,
        grid=(B, S // T),
        in_specs=[pl.BlockSpec((1, T, Q_PAD), lambda b, i: (b, i, 0)),
                  pl.BlockSpec((1, S, Q_PAD), lambda b, i: (b, 0, 0)),
                  pl.BlockSpec((1, S, GROUP_WIDTH), lambda b, i: (b, 0, 0))],
        out_specs=pl.BlockSpec((1, T, GROUP_WIDTH), lambda b, i: (b, i, 0)),
        out_shape=jax.ShapeDtypeStruct((B, S, GROUP_WIDTH), BF16),
        scratch_shapes=[pltpu.VMEM((GROUP_HEADS, T, SEG), F32), pltpu.VMEM((GROUP_HEADS, T, LANES), F32)],
        compiler_params=_cparams(("parallel", "arbitrary")),
        name="mla_attn",
    )(q, k, v)


def _swa_kernel(sink_ref, q_ref, k_ref, v_ref, o_ref):
    W = WINDOW
    i = pl.program_id(1)
    start = pl.multiple_of(jnp.maximum(i - 1, 0) * W, W)
    k = k_ref[0, pl.ds(start, 2 * W), :]
    v = v_ref[0, pl.ds(start, 2 * W), :]
    qpos = i * W + lax.broadcasted_iota(jnp.int32, (W, 2 * W), 0)
    kpos = start + lax.broadcasted_iota(jnp.int32, (W, 2 * W), 1)
    band = jnp.logical_and(kpos <= qpos, qpos - kpos < W)
    lane = lax.broadcasted_iota(jnp.int32, (1, LANES), 1)
    lo_half = lane < HEAD_DIM
    outs = []
    for hh in range(GROUP_HEADS):
        s = _dot_nt(q_ref[0, :, hh * SEG:(hh + 1) * SEG], k)
        s = jnp.where(band, s, NEG)
        sink = sink_ref[hh]
        m = jnp.maximum(jnp.max(s, axis=1, keepdims=True), sink)
        p = jnp.exp(s - m)
        denom = jnp.sum(p, axis=1, keepdims=True) + jnp.exp(sink - m)
        outs.append(_dot(p.astype(BF16), v) / denom)
    o_ref[0, :, 0:SEG] = jnp.where(lo_half, outs[0], pltpu.roll(outs[1], HEAD_DIM, 1)).astype(o_ref.dtype)
    o_ref[0, :, SEG:2 * SEG] = jnp.where(lo_half, pltpu.roll(outs[2], HEAD_DIM, 1), outs[3]).astype(o_ref.dtype)


def _swa_attn(q, k, v, sinks):
    B, S, _ = k.shape
    W = WINDOW
    full = pl.BlockSpec((1, S, 2 * HEAD_DIM), lambda b, i: (b, 0, 0))
    return pl.pallas_call(
        _swa_kernel,
        grid=(B, S // W),
        in_specs=[pl.BlockSpec(memory_space=pltpu.SMEM),
                  pl.BlockSpec((1, W, Q_PAD), lambda b, i: (b, i, 0)), full, full],
        out_specs=pl.BlockSpec((1, W, GROUP_WIDTH), lambda b, i: (b, i, 0)),
        out_shape=jax.ShapeDtypeStruct((B, S, GROUP_WIDTH), BF16),
        compiler_params=_cparams(("parallel", "parallel")),
        name="swa_attn",
    )(sinks, q, k, v)


def _moba_kernel(q_ref, q32_ref, k_ref, v_ref, kmean_ref, o_ref, acc_ref, m_ref, qaug_ref):
    T = MOBA_BLOCK
    own = pl.program_id(1)
    nb = kmean_ref.shape[1]
    row = lax.broadcasted_iota(jnp.int32, (T, T), 0)
    col = lax.broadcasted_iota(jnp.int32, (T, T), 1)
    causal = col <= row
    lane = lax.broadcasted_iota(jnp.int32, (1, LANES), 1)
    lo_half = lane < HEAD_DIM
    blk = lax.broadcasted_iota(jnp.int32, (T, nb), 1).astype(F32)
    kblk = lax.broadcasted_iota(jnp.int32, (T, nb), 1)
    own_f = own.astype(F32)

    m_ref[...] = jnp.full(m_ref.shape, NEG, F32)
    acc_ref[...] = jnp.zeros(acc_ref.shape, F32)
    own_ks = pl.ds(pl.multiple_of(own * T, T), T)

    for hh in range(GROUP_HEADS):
        psl = slice((hh // 2) * SEG, (hh // 2 + 1) * SEG)
        head_lanes = lo_half if hh % 2 == 0 else jnp.logical_not(lo_half)

        q32 = jnp.where(head_lanes, q32_ref[0, :, psl], 0.0)
        qh_hi, qh_lo = _split_bf16(q32)
        km_hi, km_lo = _split_bf16(kmean_ref[0, :, psl])
        gate = _dot_nt(qh_hi, km_hi) + (_dot_nt(qh_hi, km_lo) + _dot_nt(qh_lo, km_hi))
        g = jnp.where(blk < own_f, gate, NEG)
        sel = jnp.zeros((T, nb), F32)
        for t in range(MOBA_TOPK):
            mx = jnp.max(g, axis=1, keepdims=True)
            first = jnp.min(jnp.where(g == mx, blk, float(nb)), axis=1, keepdims=True)
            hit = blk == first
            sel = jnp.where(jnp.logical_and(hit, t < own), 1.0, sel)
            g = jnp.where(hit, -jnp.inf, g)
        qh = q_ref[0, :, hh * SEG:(hh + 1) * SEG]
        qaug_ref[hh, :, 0:SEG] = qh
        qaug_ref[hh, :, SEG:2 * SEG] = ((1.0 - sel) * NEG).astype(BF16)

        s = jnp.where(causal, _dot_nt(qh, k_ref[0, own_ks, psl]), NEG)
        _softmax_update(acc_ref, m_ref, hh, s, _augment_v(v_ref[0, own_ks, psl], hh, lo_half))

    def body(n, carry):
        ks = pl.ds(pl.multiple_of(n * T, T), T)
        onehot = jnp.where(kblk == n, 1.0, 0.0).astype(BF16)
        for pair in range(2):
            psl = slice(pair * SEG, (pair + 1) * SEG)
            k_aug = jnp.concatenate([k_ref[0, ks, psl], onehot], axis=1)
            v_pair = v_ref[0, ks, psl]
            for hh in (2 * pair, 2 * pair + 1):
                s = _dot_nt(qaug_ref[hh], k_aug)
                _softmax_update(acc_ref, m_ref, hh, s, _augment_v(v_pair, hh, lo_half))
        return carry

    lax.fori_loop(0, own, body, 0)
    _softmax_finish(acc_ref, o_ref, lo_half)


def _moba_attn(q, q32, k, v, kmean):
    B, S, _ = k.shape
    T = MOBA_BLOCK
    assert S // T <= LANES
    nb = LANES
    kmean = jnp.pad(kmean.reshape(B, S // T, GROUP_WIDTH), ((0, 0), (0, nb - S // T), (0, 0)))
    full = pl.BlockSpec((1, S, GROUP_WIDTH), lambda b, i: (b, 0, 0))
    return pl.pallas_call(
        _moba_kernel,
        grid=(B, S // T),
        in_specs=[pl.BlockSpec((1, T, Q_PAD), lambda b, i: (b, i, 0)),
                  pl.BlockSpec((1, T, GROUP_WIDTH), lambda b, i: (b, i, 0)),
                  full, full,
                  pl.BlockSpec((1, nb, GROUP_WIDTH), lambda b, i: (b, 0, 0))],
        out_specs=pl.BlockSpec((1, T, GROUP_WIDTH), lambda b, i: (b, i, 0)),
        out_shape=jax.ShapeDtypeStruct((B, S, GROUP_WIDTH), BF16),
        scratch_shapes=[pltpu.VMEM((GROUP_HEADS, T, SEG), F32), pltpu.VMEM((GROUP_HEADS, T, LANES), F32),
                        pltpu.VMEM((GROUP_HEADS, T, SEG + nb), BF16)],
        compiler_params=_cparams(("parallel", "arbitrary")),
        name="moba_attn",
    )(q, q32, k, v, kmean)


def _oproj_kernel(oa_ref, ob_ref, oc_ref, od_ref, gn_ref, wo_ref, x_ref, y_ref):
    acc = x_ref[0]
    for g, ref in enumerate((oa_ref, ob_ref, oc_ref, od_ref)):
        o = ref[0].astype(F32)
        ms = jnp.mean(o * o, axis=-1, keepdims=True)
        n = (o * lax.rsqrt(ms + EPS) * gn_ref[:, g * GROUP_WIDTH:(g + 1) * GROUP_WIDTH]).astype(BF16)
        acc = acc + _dot(n, wo_ref[g * GROUP_WIDTH:(g + 1) * GROUP_WIDTH, :])
    y_ref[0] = acc


def _oproj(groups, group_norm, w_o, x):
    B, S, D = x.shape
    ts = OUT_TS
    tok = lambda w: pl.BlockSpec((1, ts, w), lambda b, i: (b, i, 0))
    return pl.pallas_call(
        _oproj_kernel,
        grid=(B, S // ts),
        in_specs=[tok(GROUP_WIDTH)] * 4 + [_const_spec((1, D)), _const_spec(w_o.shape), tok(D)],
        out_specs=tok(D),
        out_shape=jax.ShapeDtypeStruct((B, S, D), F32),
        compiler_params=_cparams(("parallel", "parallel")),
        name="oproj",
    )(*groups, group_norm.reshape(1, D), w_o, x)


def _ffn_kernel(x_ref, p_ref, fn_ref, wup_ref, cw_ref, cb_ref, wdn_ref, pproj_ref, pgate_ref, y_ref,
                halo_ref, ubuf_ref, acc_ref):
    ts = x_ref.shape[1]
    fc2 = wup_ref.shape[2]
    fc = fc2 // 2
    n_chunks = wup_ref.shape[0]
    i = pl.program_id(1)

    @pl.when(i == 0)
    def _():
        halo_ref[...] = jnp.zeros(halo_ref.shape, F32)

    x = x_ref[0]
    ms = jnp.mean(x * x, axis=-1, keepdims=True)
    h = (x * lax.rsqrt(ms + EPS) * fn_ref[...]).astype(BF16)
    acc_ref[...] = x

    def chunk(c, carry):
        u = _dot(h, wup_ref[c])
        ubuf_ref[0:HALO, :] = halo_ref[c]
        ubuf_ref[HALO:HALO + ts, :] = u
        halo_ref[c] = u[ts - HALO:ts, :]
        cw = cw_ref[c]
        y = (cw[0:1, :] * ubuf_ref[HALO - 2:HALO - 2 + ts, :] + cw[1:2, :] * ubuf_ref[HALO - 1:HALO - 1 + ts, :]
             + cw[2:3, :] * u + cb_ref[c])
        ya = y[:, :fc]
        g = ya * (1.0 / (1.0 + jnp.exp(-ya))) * y[:, fc:]
        acc_ref[...] += _dot(g.astype(BF16), wdn_ref[c])
        return carry

    lax.fori_loop(0, n_chunks, chunk, 0)
    x2 = acc_ref[...]
    gate = _dot(x2.astype(BF16), pgate_ref[...])
    emb = _dot(p_ref[0].astype(BF16), pproj_ref[...])
    y_ref[0] = x2 + emb * (1.0 / (1.0 + jnp.exp(-gate)))


def _ffn_weights(ffn_norm, w_up, conv_w, conv_b, w_down, ple_proj, ple_gate):
    fc = FFN_FC
    nc = D_FF // fc
    D = w_up.shape[0]
    pair = lambda a: jnp.concatenate([a[..., :D_FF].reshape(a.shape[:-1] + (nc, fc)),
                                      a[..., D_FF:].reshape(a.shape[:-1] + (nc, fc))], axis=-1)
    wup = jnp.moveaxis(pair(w_up), 1, 0).astype(BF16)
    cw = jnp.pad(jnp.moveaxis(pair(conv_w), 1, 0), ((0, 0), (0, 8 - CONV_WIDTH), (0, 0)))
    cb = jnp.moveaxis(pair(conv_b.reshape(1, -1)), 1, 0)
    wdn = w_down.reshape(nc, fc, D).astype(BF16)
    return ffn_norm.reshape(1, D), wup, cw, cb, wdn, ple_proj.astype(BF16), ple_gate.astype(BF16)


def _ffn(x, p, weights):
    B, S, D = x.shape
    ts = FFN_TS
    fn, wup, cw, cb, wdn, pproj, pgate = weights
    nc, _, fc2 = wup.shape
    tok = lambda w: pl.BlockSpec((1, ts, w), lambda b, i: (b, i, 0))
    once = lambda a: pl.BlockSpec(a.shape, lambda *_: (0,) * a.ndim, pipeline_mode=pl.Buffered(1))
    return pl.pallas_call(
        _ffn_kernel,
        grid=(B, S // ts),
        in_specs=[tok(D), tok(PLE_DIM)] + [once(a) for a in (fn, wup, cw, cb, wdn, pproj, pgate)],
        out_specs=tok(D),
        out_shape=jax.ShapeDtypeStruct((B, S, D), F32),
        scratch_shapes=[pltpu.VMEM((nc, HALO, fc2), F32), pltpu.VMEM((ts + HALO, fc2), F32),
                        pltpu.VMEM((ts, D), F32)],
        compiler_params=_cparams(("arbitrary", "arbitrary")),
        name="ffn_ple",
    )(x, p, fn, wup, cw, cb, wdn, pproj, pgate)


def kernel(x, p, positions, attn_norm, w_in, mla_q_norm, mla_w_uq, mla_kv_norm, mla_w_ukv, mla_q_gain, mla_k_gain, swa_q_gain, swa_k_gain, swa_sinks, moba_q_gain, moba_k_gain, group_norm, w_o, ffn_norm, w_up, conv_w, conv_b, w_down, ple_proj, ple_gate):
    B, S, D = x.shape
    depth = w_in.shape[0]
    assert D == D_MODEL and S % 512 == 0
    tables = _rope_tables(positions)
    for i in range(depth):
        pw = _prep_weights(w_in[i], mla_q_norm[i], mla_w_uq[i], mla_kv_norm[i], mla_w_ukv[i], mla_q_gain[i],
                           mla_k_gain[i], swa_q_gain[i], swa_k_gain[i], moba_q_gain[i], moba_k_gain[i])
        (qa, ka, va, qm, km, vm, qc, kc, vc, qd, qd32, kd, vd, kmean) = _prep(x, attn_norm[i], tables, pw)
        o_a = _sb_attn(qa, ka, va)
        o_b = _mla_attn(qm, km, vm)
        o_c = _swa_attn(qc, kc, vc, swa_sinks[i])
        o_d = _moba_attn(qd, qd32, kd, vd, kmean)
        x = _oproj((o_a, o_b, o_c, o_d), group_norm[i], w_o[i].astype(BF16), x)
        fw = _ffn_weights(ffn_norm[i], w_up[i], conv_w[i], conv_b[i], w_down[i], ple_proj[i], ple_gate[i])
        x = _ffn(x, p[i], fw)
    return x
```

```python
import functools

import jax
import jax.numpy as jnp
from jax import lax
from jax.experimental import pallas as pl
from jax.experimental.pallas import tpu as pltpu

F32 = jnp.float32
BF16 = jnp.bfloat16

D_MODEL = 1024
HEAD_DIM = 64
GROUP_HEADS = 4
GROUP_WIDTH = GROUP_HEADS * HEAD_DIM
N_GROUPS = 4
ROPE_THETA = 10000.0
EPS = 1e-6
NEG = -1e30
LOG2E = 1.4426950408889634

MLA_Q_RANK = 192
MLA_KV_RANK = 128
MLA_NOPE = 64
MLA_ROPE = 32
MLA_V = 64
MLA_QK = MLA_NOPE + MLA_ROPE
SWA_KV_HEADS = 2
WINDOW = 128
MOBA_BLOCK = 256
MOBA_TOPK = 3
D_FF = 2816
CONV_WIDTH = 3
PLE_DIM = 256

SB_COLS = 3 * GROUP_WIDTH
MLA_COLS = MLA_Q_RANK + MLA_KV_RANK + MLA_ROPE
SWA_COLS = GROUP_WIDTH + 2 * SWA_KV_HEADS * HEAD_DIM
MOBA_COLS = 3 * GROUP_WIDTH
OFF_SB = 0
OFF_MLA = OFF_SB + SB_COLS
OFF_SWA = OFF_MLA + MLA_COLS
OFF_MOBA = OFF_SWA + SWA_COLS

LANES = 128
SEG = LANES
Q_PAD = GROUP_HEADS * SEG
N_IN_PAD = 2560
VMEM_LIMIT = 56 * 1024 * 1024

PREP_TS = 256
OUT_TS = 512
FFN_TS = 512
FFN_FC = 256
SB_T = 128
SB_EXIT = -45.0
MLA_T = 256
MLA_WIDE = 2
HALO = 8


def _dot(a, b):
    return jnp.dot(a, b, preferred_element_type=F32)


def _dot_nt(a, b):
    return lax.dot_general(a, b, (((1,), (1,)), ((), ())), preferred_element_type=F32)


def _split_bf16(x):
    hi = x.astype(BF16)
    lo = (x - hi.astype(F32)).astype(BF16)
    return hi, lo


def _cparams(sem):
    return pltpu.CompilerParams(dimension_semantics=sem, vmem_limit_bytes=VMEM_LIMIT)


def _const_spec(shape):
    nd = len(shape)
    return pl.BlockSpec(shape, lambda *_: (0,) * nd)


def _rope_tables_kernel(pos_ref, inv_ref, c64_ref, s64_ref, cm_ref, sm_ref):
    pos = pos_ref[0].astype(F32)
    lane = lax.broadcasted_iota(jnp.int32, (1, LANES), 1)
    ang = pos * inv_ref[0:1, :]
    first = (lane & (HEAD_DIM - 1)) < (HEAD_DIM // 2)
    c64_ref[0] = jnp.cos(ang)
    s64_ref[0] = jnp.where(first, -jnp.sin(ang), jnp.sin(ang))
    angm = pos * inv_ref[1:2, :]
    in_rope = (lane >= MLA_NOPE) & (lane < MLA_QK)
    rope_first = lane < MLA_NOPE + MLA_ROPE // 2
    cm_ref[0] = jnp.where(in_rope, jnp.cos(angm), 1.0)
    sm_ref[0] = jnp.where(in_rope, jnp.where(rope_first, -jnp.sin(angm), jnp.sin(angm)), 0.0)


def _rope_tables(positions):
    B, S = positions.shape
    ts = 512
    half64 = HEAD_DIM // 2
    halfm = MLA_ROPE // 2
    inv64 = ROPE_THETA ** (-jnp.arange(half64, dtype=F32) / half64)
    invm = ROPE_THETA ** (-jnp.arange(halfm, dtype=F32) / halfm)
    lane = jnp.arange(LANES)
    row0 = inv64[lane % half64]
    row1 = jnp.where((lane >= MLA_NOPE) & (lane < MLA_QK), invm[(lane - MLA_NOPE) % halfm], 0.0)
    inv = jnp.zeros((8, LANES), F32).at[0].set(row0).at[1].set(row1)
    tab = jax.ShapeDtypeStruct((B, S, LANES), F32)
    spec = pl.BlockSpec((1, ts, LANES), lambda b, i: (b, i, 0))
    return pl.pallas_call(
        _rope_tables_kernel,
        grid=(B, S // ts),
        in_specs=[pl.BlockSpec((1, ts, 1), lambda b, i: (b, i, 0)), _const_spec((8, LANES))],
        out_specs=[spec] * 4,
        out_shape=[tab] * 4,
        compiler_params=_cparams(("parallel", "parallel")),
        name="rope_tables",
    )(positions.reshape(B, S, 1), inv)


def _rope64(x, cos, sin_signed, lane):
    first = (lane & (HEAD_DIM - 1)) < (HEAD_DIM // 2)
    partner = jnp.where(first, pltpu.roll(x, LANES - HEAD_DIM // 2, 1), pltpu.roll(x, HEAD_DIM // 2, 1))
    return x * cos + partner * sin_signed


def _rope_mla(x, cos, sin_signed, lane):
    first = lane < MLA_NOPE + MLA_ROPE // 2
    partner = jnp.where(first, pltpu.roll(x, LANES - MLA_ROPE // 2, 1), pltpu.roll(x, MLA_ROPE // 2, 1))
    return x * cos + partner * sin_signed


def _ms64(x, bsel):
    hi, lo = _split_bf16(x * x)
    return (_dot(hi, bsel) + _dot(lo, bsel)) * (1.0 / HEAD_DIM)


def _prep_kernel(x_ref, an_ref, win_ref, c64_ref, s64_ref, cm_ref, sm_ref,
                 qn_ref, wuq_ref, kvn_ref, wukv_ref, qg_ref, kg_ref, g64_ref,
                 qa_ref, ka_ref, va_ref, qm_ref, km_ref, vm_ref, qc_ref, kc_ref, vc_ref,
                 qd_ref, qd32_ref, kd_ref, vd_ref, kmean_ref):
    ts = x_ref.shape[1]
    x = x_ref[0]
    ms = jnp.mean(x * x, axis=-1, keepdims=True)
    h = (x * lax.rsqrt(ms + EPS) * an_ref[...]).astype(BF16)
    proj = _dot(h, win_ref[...])

    lane = lax.broadcasted_iota(jnp.int32, (1, LANES), 1)
    lo_half = lane < HEAD_DIM
    r = lax.broadcasted_iota(jnp.int32, (LANES, LANES), 0) // HEAD_DIM
    c = lax.broadcasted_iota(jnp.int32, (LANES, LANES), 1) // HEAD_DIM
    bsel = jnp.where(r == c, 1.0, 0.0).astype(BF16)
    c64, s64, cm, sm = c64_ref[0], s64_ref[0], cm_ref[0], sm_ref[0]
    scale64 = HEAD_DIM ** -0.5

    def seg(off, j):
        return proj[:, off + j * SEG: off + (j + 1) * SEG]

    def store_pair(ref, val, seg_lo, seg_hi):
        ref[0, :, seg_lo * SEG:(seg_lo + 1) * SEG] = jnp.where(lo_half, val, 0.0).astype(BF16)
        ref[0, :, seg_hi * SEG:(seg_hi + 1) * SEG] = jnp.where(lo_half, 0.0, val).astype(BF16)

    for p in range(2):
        store_pair(qa_ref, seg(0, p) * scale64, 2 * p, 2 * p + 1)
    ka_ref[0] = proj[:, 256:512].astype(BF16)
    va_ref[0] = proj[:, 512:768].astype(BF16)

    cq = proj[:, 768:1024]
    cqn = (cq * lax.rsqrt(jnp.sum(cq * cq, axis=-1, keepdims=True) * (1.0 / MLA_Q_RANK) + EPS)
           * qn_ref[...]).astype(BF16)
    qm_raw = _dot(cqn, wuq_ref[...])
    ckv = proj[:, 1024:1152]
    ckvn = (ckv * lax.rsqrt(jnp.mean(ckv * ckv, axis=-1, keepdims=True) + EPS) * kvn_ref[...]).astype(BF16)
    kv = _dot(ckvn, wukv_ref[...])
    kpe = proj[:, 1152:1280]
    vm_ref[0] = kv[:, 512:768].astype(BF16)
    scale_m = MLA_QK ** -0.5 * LOG2E
    for hh in range(GROUP_HEADS):
        sl = slice(hh * SEG, (hh + 1) * SEG)
        qs = qm_raw[:, sl]
        qs = qs * lax.rsqrt(jnp.sum(qs * qs, axis=-1, keepdims=True) * (1.0 / MLA_QK) + EPS) * qg_ref[...]
        qm_ref[0, :, sl] = (_rope_mla(qs, cm, sm, lane) * scale_m).astype(BF16)
        ks = kv[:, sl] + kpe
        ks = ks * lax.rsqrt(jnp.sum(ks * ks, axis=-1, keepdims=True) * (1.0 / MLA_QK) + EPS) * kg_ref[...]
        km_ref[0, :, sl] = _rope_mla(ks, cm, sm, lane).astype(BF16)

    def norm_rope64(v, gain_row):
        v = v * lax.rsqrt(_ms64(v, bsel) + EPS) * g64_ref[gain_row:gain_row + 1, :]
        return _rope64(v, c64, s64, lane)

    qa_pair = norm_rope64(seg(1280, 0), 0) * scale64
    qb_pair = norm_rope64(seg(1280, 1), 0) * scale64
    store_pair(qc_ref, qa_pair, 0, 2)
    store_pair(qc_ref, qb_pair, 1, 3)
    kc_ref[0] = norm_rope64(seg(1536, 0), 1).astype(BF16)
    vc_ref[0] = seg(1664, 0).astype(BF16)

    for p in range(2):
        qd = norm_rope64(seg(1792, p), 2)
        qd32_ref[0, :, p * SEG:(p + 1) * SEG] = qd
        store_pair(qd_ref, qd * (scale64 * LOG2E), 2 * p, 2 * p + 1)
        kd = norm_rope64(seg(2048, p), 3)
        kd_ref[0, :, p * SEG:(p + 1) * SEG] = kd.astype(BF16)
        for blk in range(ts // MOBA_BLOCK):
            kmean_ref[0, blk, :, p * SEG:(p + 1) * SEG] = jnp.mean(
                kd[blk * MOBA_BLOCK:(blk + 1) * MOBA_BLOCK], axis=0, keepdims=True)
    vd_ref[0] = proj[:, 2304:2560].astype(BF16)


def _prep_weights(w_in, mla_q_norm, mla_w_uq, mla_kv_norm, mla_w_ukv, mla_q_gain, mla_k_gain,
                  swa_q_gain, swa_k_gain, moba_q_gain, moba_k_gain):
    D = w_in.shape[0]
    z = lambda n: jnp.zeros((D, n), F32)
    mla = w_in[:, OFF_MLA:OFF_MLA + MLA_COLS]
    swa = w_in[:, OFF_SWA:OFF_SWA + SWA_COLS]
    swa_q = swa[:, :GROUP_WIDTH].reshape(D, GROUP_HEADS, HEAD_DIM)[:, jnp.array([0, 2, 1, 3])].reshape(D, GROUP_WIDTH)
    win = jnp.concatenate([
        w_in[:, OFF_SB:OFF_SB + SB_COLS],
        mla[:, :MLA_Q_RANK], z(256 - MLA_Q_RANK),
        mla[:, MLA_Q_RANK:MLA_Q_RANK + MLA_KV_RANK],
        z(MLA_NOPE), mla[:, MLA_Q_RANK + MLA_KV_RANK:], z(SEG - MLA_QK),
        swa_q, swa[:, GROUP_WIDTH:],
        w_in[:, OFF_MOBA:OFF_MOBA + MOBA_COLS],
    ], axis=1).astype(BF16)
    assert win.shape[1] == N_IN_PAD
    qn = jnp.pad(mla_q_norm, (0, 256 - MLA_Q_RANK)).reshape(1, 256)
    wuq = mla_w_uq.reshape(MLA_Q_RANK, GROUP_HEADS, MLA_QK)
    wuq = jnp.pad(wuq, ((0, 256 - MLA_Q_RANK), (0, 0), (0, SEG - MLA_QK))).reshape(256, Q_PAD).astype(BF16)
    wukv = mla_w_ukv.reshape(MLA_KV_RANK, GROUP_HEADS, MLA_NOPE + MLA_V)
    wkn = jnp.pad(wukv[:, :, :MLA_NOPE], ((0, 0), (0, 0), (0, SEG - MLA_NOPE))).reshape(MLA_KV_RANK, Q_PAD)
    wv = wukv[:, :, MLA_NOPE:].reshape(MLA_KV_RANK, GROUP_WIDTH)
    wukv_r = jnp.concatenate([wkn, wv], axis=1).astype(BF16)
    qg = jnp.pad(mla_q_gain, (0, SEG - MLA_QK)).reshape(1, SEG)
    kg = jnp.pad(mla_k_gain, (0, SEG - MLA_QK)).reshape(1, SEG)
    g64 = jnp.zeros((8, LANES), F32)
    for row, g in enumerate((swa_q_gain, swa_k_gain, moba_q_gain, moba_k_gain)):
        g64 = g64.at[row].set(jnp.tile(g, LANES // HEAD_DIM))
    return win, qn, wuq, mla_kv_norm.reshape(1, MLA_KV_RANK), wukv_r, qg, kg, g64


def _prep(x, attn_norm, tables, weights):
    B, S, D = x.shape
    ts = PREP_TS
    win, qn, wuq, kvn, wukv, qg, kg, g64 = weights
    tok = lambda w: pl.BlockSpec((1, ts, w), lambda b, i: (b, i, 0))
    nb_t = ts // MOBA_BLOCK
    out_widths = [(Q_PAD, BF16), (256, BF16), (256, BF16),
                  (Q_PAD, BF16), (Q_PAD, BF16), (256, BF16),
                  (Q_PAD, BF16), (128, BF16), (128, BF16),
                  (Q_PAD, BF16), (256, F32), (256, BF16), (256, BF16)]
    out_shape = [jax.ShapeDtypeStruct((B, S, w), dt) for w, dt in out_widths]
    out_specs = [tok(w) for w, _ in out_widths]
    out_shape.append(jax.ShapeDtypeStruct((B, S // MOBA_BLOCK, 1, 256), F32))
    out_specs.append(pl.BlockSpec((1, nb_t, 1, 256), lambda b, i: (b, i, 0, 0)))
    consts = [attn_norm.reshape(1, D), win]
    tail = [qn, wuq, kvn, wukv, qg, kg, g64]
    in_specs = ([tok(D)] + [_const_spec(a.shape) for a in consts] + [tok(LANES)] * 4
                + [_const_spec(a.shape) for a in tail])
    return pl.pallas_call(
        _prep_kernel,
        grid=(B, S // ts),
        in_specs=in_specs,
        out_specs=out_specs,
        out_shape=out_shape,
        compiler_params=_cparams(("parallel", "parallel")),
        name="prep",
    )(x, *consts, *tables, *tail)


def _sb_kernel(q_ref, k_ref, v_ref, o_ref, acc_ref, carry_ref):
    T = SB_T
    i = pl.program_id(1)
    row = lax.broadcasted_iota(jnp.int32, (T, T), 0)
    col = lax.broadcasted_iota(jnp.int32, (T, T), 1)
    upper = jnp.concatenate([jnp.where(row > col, 1.0, 0.0), jnp.ones((T, T), F32)], axis=1).astype(BF16)
    causal = col < row
    lane = lax.broadcasted_iota(jnp.int32, (1, LANES), 1)

    heads = range(GROUP_HEADS)

    def block(j, first):
        ks = pl.ds(pl.multiple_of(j * T, T), T)
        kv = lambda ref, hh: ref[0, ks, (hh // 2) * SEG:(hh // 2 + 1) * SEG]
        z = [_dot_nt(q_ref[0, :, hh * SEG:(hh + 1) * SEG], kv(k_ref, hh)) for hh in heads]
        log_beta = [jnp.minimum(zz, 0.0) - jnp.log(1.0 + jnp.exp(-jnp.abs(zz))) for zz in z]
        log_keep = [lb - zz for lb, zz in zip(log_beta, z)]
        if first:
            log_keep = [jnp.where(causal, lk, 0.0) for lk in log_keep]
        parts = [_split_bf16(lk) for lk in log_keep]
        sums = [_dot(hi, upper) + _dot(lo, upper) for hi, lo in parts]
        if first:
            w = [jnp.where(causal, jnp.exp(lb + sm[:, :T]), 0.0) for lb, sm in zip(log_beta, sums)]
            carry = [sm[:, T:] for sm in sums]
        else:
            w = [jnp.exp(lb + (sm[:, :T] + carry_ref[hh])) for hh, lb, sm in zip(heads, log_beta, sums)]
            carry = [carry_ref[hh] + sm[:, T:] for hh, sm in zip(heads, sums)]
        pv = [_dot(ww.astype(BF16), kv(v_ref, hh)) for hh, ww in zip(heads, w)]
        for hh in heads:
            acc_ref[hh] = pv[hh] if first else acc_ref[hh] + pv[hh]
            carry_ref[hh] = carry[hh]
        return jnp.max(jnp.maximum(jnp.maximum(carry[0], carry[1]), jnp.maximum(carry[2], carry[3])))

    def cond(st):
        j, mx = st
        return jnp.logical_and(j >= 0, mx > SB_EXIT)

    def body(st):
        return st[0] - 1, block(st[0], False)

    lax.while_loop(cond, body, (i - 1, block(i, True)))

    lo_half = lane < HEAD_DIM
    for p in range(2):
        o_ref[0, :, p * SEG:(p + 1) * SEG] = jnp.where(lo_half, acc_ref[2 * p], acc_ref[2 * p + 1]).astype(o_ref.dtype)


def _sb_attn(q, k, v):
    B, S, _ = k.shape
    T = SB_T
    full = pl.BlockSpec((1, S, GROUP_WIDTH), lambda b, i: (b, 0, 0))
    return pl.pallas_call(
        _sb_kernel,
        grid=(B, S // T),
        in_specs=[pl.BlockSpec((1, T, Q_PAD), lambda b, i: (b, i, 0)), full, full],
        out_specs=pl.BlockSpec((1, T, GROUP_WIDTH), lambda b, i: (b, i, 0)),
        out_shape=jax.ShapeDtypeStruct((B, S, GROUP_WIDTH), BF16),
        scratch_shapes=[pltpu.VMEM((GROUP_HEADS, T, SEG), F32), pltpu.VMEM((GROUP_HEADS, T, T), F32)],
        compiler_params=_cparams(("parallel", "arbitrary")),
        name="sb_attn",
    )(q, k, v)


def _softmax_update(acc_ref, m_ref, hh, s, v_aug):
    _softmax_update_all(acc_ref, m_ref, [hh], [s], [v_aug])


def _softmax_update_all(acc_ref, m_ref, heads, scores, v_augs):
    tk = scores[0].shape[1]
    m_old = [m_ref[hh] for hh in heads]
    m_new = [jnp.maximum(mo, jnp.max(s, axis=1, keepdims=True)) for mo, s in zip(m_old, scores)]
    p = [jnp.concatenate([jnp.exp2(s[:, c * LANES:(c + 1) * LANES] - mn) for c in range(tk // LANES)],
                         axis=1).astype(BF16) for s, mn in zip(scores, m_new)]
    pv = [_dot(pp, va) for pp, va in zip(p, v_augs)]
    for hh, mo, mn, r in zip(heads, m_old, m_new, pv):
        acc_ref[hh] = jnp.exp2(mo - mn) * acc_ref[hh] + r
        m_ref[hh] = mn


def _augment_v(v_pair, hh, lo_half):
    one = jnp.ones((), v_pair.dtype)
    return jnp.where(lo_half, v_pair, one) if hh % 2 == 0 else jnp.where(lo_half, one, v_pair)


def _softmax_finish(acc_ref, o_ref, lo_half):
    outs = []
    for hh in range(GROUP_HEADS):
        r = acc_ref[hh]
        outs.append(r / pltpu.roll(r, HEAD_DIM, 1))
    for p in range(2):
        o_ref[0, :, p * SEG:(p + 1) * SEG] = jnp.where(lo_half, outs[2 * p], outs[2 * p + 1]).astype(o_ref.dtype)


def _mla_kernel(q_ref, k_ref, v_ref, o_ref, acc_ref, m_ref):
    T = MLA_T
    i = pl.program_id(1)
    row = lax.broadcasted_iota(jnp.int32, (T, T), 0)
    col = lax.broadcasted_iota(jnp.int32, (T, T), 1)
    causal = col <= row
    lo_half = lax.broadcasted_iota(jnp.int32, (1, LANES), 1) < HEAD_DIM

    def step(first_key, tk, masked):
        ks = pl.ds(pl.multiple_of(first_key, T), tk)
        heads = range(GROUP_HEADS)
        s = [_dot_nt(q_ref[0, :, hh * SEG:(hh + 1) * SEG], k_ref[0, ks, hh * SEG:(hh + 1) * SEG]) for hh in heads]
        if masked:
            s = [jnp.where(causal, ss, NEG) for ss in s]
        v = [_augment_v(v_ref[0, ks, (hh // 2) * SEG:(hh // 2 + 1) * SEG], hh, lo_half) for hh in heads]
        _softmax_update_all(acc_ref, m_ref, heads, s, v)

    m_ref[...] = jnp.full(m_ref.shape, NEG, F32)
    acc_ref[...] = jnp.zeros(acc_ref.shape, F32)

    def body(j, carry):
        step(j * (MLA_WIDE * T), MLA_WIDE * T, False)
        return carry

    lax.fori_loop(0, i // MLA_WIDE, body, 0)
    for r in range(MLA_WIDE - 1):
        @pl.when(i % MLA_WIDE > r)
        def _(r=r):
            step((i // MLA_WIDE * MLA_WIDE + r) * T, T, False)
    step(i * T, T, True)
    _softmax_finish(acc_ref, o_ref, lo_half)


def _mla_attn(q, k, v):
    B, S, _ = k.shape
    T = MLA_T
    return pl.pallas_call(
        _mla_kernel,
        grid=(B, S // T),
        in_specs=[pl.BlockSpec((1, T, Q_PAD), lambda b, i: (b, i, 0)),
                  pl.BlockSpec((1, S, Q_PAD), lambda b, i: (b, 0, 0)),
                  pl.BlockSpec((1, S, GROUP_WIDTH), lambda b, i: (b, 0, 0))],
        out_specs=pl.BlockSpec((1, T, GROUP_WIDTH), lambda b, i: (b, i, 0)),
        out_shape=jax.ShapeDtypeStruct((B, S, GROUP_WIDTH), BF16),
        scratch_shapes=[pltpu.VMEM((GROUP_HEADS, T, SEG), F32), pltpu.VMEM((GROUP_HEADS, T, LANES), F32)],
        compiler_params=_cparams(("parallel", "arbitrary")),
        name="mla_attn",
    )(q, k, v)


def _swa_kernel(sink_ref, q_ref, k_ref, v_ref, o_ref):
    W = WINDOW
    i = pl.program_id(1)
    start = pl.multiple_of(jnp.maximum(i - 1, 0) * W, W)
    k = k_ref[0, pl.ds(start, 2 * W), :]
    v = v_ref[0, pl.ds(start, 2 * W), :]
    qpos = i * W + lax.broadcasted_iota(jnp.int32, (W, 2 * W), 0)
    kpos = start + lax.broadcasted_iota(jnp.int32, (W, 2 * W), 1)
    band = jnp.logical_and(kpos <= qpos, qpos - kpos < W)
    lane = lax.broadcasted_iota(jnp.int32, (1, LANES), 1)
    lo_half = lane < HEAD_DIM
    outs = []
    for hh in range(GROUP_HEADS):
        s = _dot_nt(q_ref[0, :, hh * SEG:(hh + 1) * SEG], k)
        s = jnp.where(band, s, NEG)
        sink = sink_ref[hh]
        m = jnp.maximum(jnp.max(s, axis=1, keepdims=True), sink)
        p = jnp.exp(s - m)
        denom = jnp.sum(p, axis=1, keepdims=True) + jnp.exp(sink - m)
        outs.append(_dot(p.astype(BF16), v) / denom)
    o_ref[0, :, 0:SEG] = jnp.where(lo_half, outs[0], pltpu.roll(outs[1], HEAD_DIM, 1)).astype(o_ref.dtype)
    o_ref[0, :, SEG:2 * SEG] = jnp.where(lo_half, pltpu.roll(outs[2], HEAD_DIM, 1), outs[3]).astype(o_ref.dtype)


def _swa_attn(q, k, v, sinks):
    B, S, _ = k.shape
    W = WINDOW
    full = pl.BlockSpec((1, S, 2 * HEAD_DIM), lambda b, i: (b, 0, 0))
    return pl.pallas_call(
        _swa_kernel,
        grid=(B, S // W),
        in_specs=[pl.BlockSpec(memory_space=pltpu.SMEM),
                  pl.BlockSpec((1, W, Q_PAD), lambda b, i: (b, i, 0)), full, full],
        out_specs=pl.BlockSpec((1, W, GROUP_WIDTH), lambda b, i: (b, i, 0)),
        out_shape=jax.ShapeDtypeStruct((B, S, GROUP_WIDTH), BF16),
        compiler_params=_cparams(("parallel", "parallel")),
        name="swa_attn",
    )(sinks, q, k, v)


def _moba_kernel(q_ref, q32_ref, k_ref, v_ref, kmean_ref, o_ref, acc_ref, m_ref, qaug_ref):
    T = MOBA_BLOCK
    own = pl.program_id(1)
    nb = kmean_ref.shape[1]
    row = lax.broadcasted_iota(jnp.int32, (T, T), 0)
    col = lax.broadcasted_iota(jnp.int32, (T, T), 1)
    causal = col <= row
    lane = lax.broadcasted_iota(jnp.int32, (1, LANES), 1)
    lo_half = lane < HEAD_DIM
    blk = lax.broadcasted_iota(jnp.int32, (T, nb), 1).astype(F32)
    kblk = lax.broadcasted_iota(jnp.int32, (T, nb), 1)
    own_f = own.astype(F32)

    m_ref[...] = jnp.full(m_ref.shape, NEG, F32)
    acc_ref[...] = jnp.zeros(acc_ref.shape, F32)
    own_ks = pl.ds(pl.multiple_of(own * T, T), T)

    heads = range(GROUP_HEADS)
    pair_sl = [slice((hh // 2) * SEG, (hh // 2 + 1) * SEG) for hh in heads]

    s = [jnp.where(causal, _dot_nt(q_ref[0, :, hh * SEG:(hh + 1) * SEG], k_ref[0, own_ks, pair_sl[hh]]), NEG)
         for hh in heads]
    v = [_augment_v(v_ref[0, own_ks, pair_sl[hh]], hh, lo_half) for hh in heads]
    _softmax_update_all(acc_ref, m_ref, heads, s, v)

    g = []
    for hh in heads:
        head_lanes = lo_half if hh % 2 == 0 else jnp.logical_not(lo_half)
        qh_hi, qh_lo = _split_bf16(jnp.where(head_lanes, q32_ref[0, :, pair_sl[hh]], 0.0))
        km_hi, km_lo = _split_bf16(kmean_ref[0, :, pair_sl[hh]])
        gate = _dot_nt(qh_hi, km_hi) + (_dot_nt(qh_hi, km_lo) + _dot_nt(qh_lo, km_hi))
        g.append(jnp.where(blk < own_f, gate, NEG))
    sel = [jnp.zeros((T, nb), F32) for _ in heads]
    for t in range(MOBA_TOPK):
        mx = [jnp.max(gg, axis=1, keepdims=True) for gg in g]
        first = [jnp.min(jnp.where(gg == m, blk, float(nb)), axis=1, keepdims=True)
                 for gg, m in zip(g, mx)]
        hit = [blk == f for f in first]
        sel = [jnp.where(jnp.logical_and(h, t < own), 1.0, sl) for h, sl in zip(hit, sel)]
        g = [jnp.where(h, -jnp.inf, gg) for h, gg in zip(hit, g)]
    for hh in heads:
        qaug_ref[hh, :, 0:SEG] = q_ref[0, :, hh * SEG:(hh + 1) * SEG]
        qaug_ref[hh, :, SEG:2 * SEG] = ((1.0 - sel[hh]) * NEG).astype(BF16)

    def step(n0, nblk):
        ks = pl.ds(pl.multiple_of(n0 * T, T), nblk * T)
        onehot = jnp.concatenate([jnp.where(kblk == n0 + r, 1.0, 0.0) for r in range(nblk)], axis=0).astype(BF16)
        heads = range(GROUP_HEADS)
        k_aug = [jnp.concatenate([k_ref[0, ks, pair * SEG:(pair + 1) * SEG], onehot], axis=1) for pair in range(2)]
        s = [_dot_nt(qaug_ref[hh], k_aug[hh // 2]) for hh in heads]
        v = [_augment_v(v_ref[0, ks, (hh // 2) * SEG:(hh // 2 + 1) * SEG], hh, lo_half) for hh in heads]
        _softmax_update_all(acc_ref, m_ref, heads, s, v)

    def body(j, carry):
        step(j * MLA_WIDE, MLA_WIDE)
        return carry

    lax.fori_loop(0, own // MLA_WIDE, body, 0)
    for r in range(MLA_WIDE - 1):
        @pl.when(own % MLA_WIDE > r)
        def _(r=r):
            step(own // MLA_WIDE * MLA_WIDE + r, 1)
    _softmax_finish(acc_ref, o_ref, lo_half)


def _moba_attn(q, q32, k, v, kmean):
    B, S, _ = k.shape
    T = MOBA_BLOCK
    assert S // T <= LANES
    nb = LANES
    kmean = jnp.pad(kmean.reshape(B, S // T, GROUP_WIDTH), ((0, 0), (0, nb - S // T), (0, 0)))
    full = pl.BlockSpec((1, S, GROUP_WIDTH), lambda b, i: (b, 0, 0))
    return pl.pallas_call(
        _moba_kernel,
        grid=(B, S // T),
        in_specs=[pl.BlockSpec((1, T, Q_PAD), lambda b, i: (b, i, 0)),
                  pl.BlockSpec((1, T, GROUP_WIDTH), lambda b, i: (b, i, 0)),
                  full, full,
                  pl.BlockSpec((1, nb, GROUP_WIDTH), lambda b, i: (b, 0, 0))],
        out_specs=pl.BlockSpec((1, T, GROUP_WIDTH), lambda b, i: (b, i, 0)),
        out_shape=jax.ShapeDtypeStruct((B, S, GROUP_WIDTH), BF16),
        scratch_shapes=[pltpu.VMEM((GROUP_HEADS, T, SEG), F32), pltpu.VMEM((GROUP_HEADS, T, LANES), F32),
                        pltpu.VMEM((GROUP_HEADS, T, SEG + nb), BF16)],
        compiler_params=_cparams(("parallel", "arbitrary")),
        name="moba_attn",
    )(q, q32, k, v, kmean)


def _oproj_kernel(oa_ref, ob_ref, oc_ref, od_ref, gn_ref, wo_ref, x_ref, y_ref):
    acc = x_ref[0]
    for g, ref in enumerate((oa_ref, ob_ref, oc_ref, od_ref)):
        o = ref[0].astype(F32)
        ms = jnp.mean(o * o, axis=-1, keepdims=True)
        n = (o * lax.rsqrt(ms + EPS) * gn_ref[:, g * GROUP_WIDTH:(g + 1) * GROUP_WIDTH]).astype(BF16)
        acc = acc + _dot(n, wo_ref[g * GROUP_WIDTH:(g + 1) * GROUP_WIDTH, :])
    y_ref[0] = acc


def _oproj(groups, group_norm, w_o, x):
    B, S, D = x.shape
    ts = OUT_TS
    tok = lambda w: pl.BlockSpec((1, ts, w), lambda b, i: (b, i, 0))
    return pl.pallas_call(
        _oproj_kernel,
        grid=(B, S // ts),
        in_specs=[tok(GROUP_WIDTH)] * 4 + [_const_spec((1, D)), _const_spec(w_o.shape), tok(D)],
        out_specs=tok(D),
        out_shape=jax.ShapeDtypeStruct((B, S, D), F32),
        compiler_params=_cparams(("parallel", "parallel")),
        name="oproj",
    )(*groups, group_norm.reshape(1, D), w_o, x)


def _ffn_kernel(x_ref, p_ref, fn_ref, wup_ref, cw_ref, cb_ref, wdn_ref, pproj_ref, pgate_ref, y_ref,
                halo_ref, ubuf0_ref, ubuf1_ref, acc_ref, h_ref):
    ts = x_ref.shape[1]
    fc2 = wup_ref.shape[2]
    fc = fc2 // 2
    n_chunks = wup_ref.shape[0]
    i = pl.program_id(1)

    @pl.when(i == 0)
    def _():
        halo_ref[...] = jnp.zeros(halo_ref.shape, F32)

    x = x_ref[0]
    ms = jnp.mean(x * x, axis=-1, keepdims=True)
    h_ref[...] = (x * lax.rsqrt(ms + EPS) * fn_ref[...]).astype(BF16)
    acc_ref[...] = x

    def up(c, ubuf_ref):
        u = _dot(h_ref[...], wup_ref[c])
        ubuf_ref[0:HALO, :] = halo_ref[c]
        ubuf_ref[HALO:HALO + ts, :] = u
        halo_ref[c] = u[ts - HALO:ts, :]

    def down(c, ubuf_ref):
        cw = cw_ref[c]
        y = (cw[0:1, :] * ubuf_ref[HALO - 2:HALO - 2 + ts, :] + cw[1:2, :] * ubuf_ref[HALO - 1:HALO - 1 + ts, :]
             + cw[2:3, :] * ubuf_ref[HALO:HALO + ts, :] + cb_ref[c])
        ya = y[:, :fc]
        g = ya * (1.0 / (1.0 + jnp.exp(-ya))) * y[:, fc:]
        acc_ref[...] += _dot(g.astype(BF16), wdn_ref[c])

    up(0, ubuf0_ref)

    def pair(cc, carry):
        up(2 * cc + 1, ubuf1_ref)
        down(2 * cc, ubuf0_ref)
        up(2 * cc + 2, ubuf0_ref)
        down(2 * cc + 1, ubuf1_ref)
        return carry

    assert n_chunks % 2 == 1
    lax.fori_loop(0, n_chunks // 2, pair, 0)
    down(n_chunks - 1, ubuf0_ref)
    x2 = acc_ref[...]
    gate = _dot(x2.astype(BF16), pgate_ref[...])
    emb = _dot(p_ref[0].astype(BF16), pproj_ref[...])
    y_ref[0] = x2 + emb * (1.0 / (1.0 + jnp.exp(-gate)))


def _ffn_weights(ffn_norm, w_up, conv_w, conv_b, w_down, ple_proj, ple_gate):
    fc = FFN_FC
    nc = D_FF // fc
    D = w_up.shape[0]
    pair = lambda a: jnp.concatenate([a[..., :D_FF].reshape(a.shape[:-1] + (nc, fc)),
                                      a[..., D_FF:].reshape(a.shape[:-1] + (nc, fc))], axis=-1)
    wup = jnp.moveaxis(pair(w_up), 1, 0).astype(BF16)
    cw = jnp.pad(jnp.moveaxis(pair(conv_w), 1, 0), ((0, 0), (0, 8 - CONV_WIDTH), (0, 0)))
    cb = jnp.moveaxis(pair(conv_b.reshape(1, -1)), 1, 0)
    wdn = w_down.reshape(nc, fc, D).astype(BF16)
    return ffn_norm.reshape(1, D), wup, cw, cb, wdn, ple_proj.astype(BF16), ple_gate.astype(BF16)


def _ffn(x, p, weights):
    B, S, D = x.shape
    ts = FFN_TS
    fn, wup, cw, cb, wdn, pproj, pgate = weights
    nc, _, fc2 = wup.shape
    tok = lambda w: pl.BlockSpec((1, ts, w), lambda b, i: (b, i, 0))
    once = lambda a: pl.BlockSpec(a.shape, lambda *_: (0,) * a.ndim, pipeline_mode=pl.Buffered(1))
    return pl.pallas_call(
        _ffn_kernel,
        grid=(B, S // ts),
        in_specs=[tok(D), tok(PLE_DIM)] + [once(a) for a in (fn, wup, cw, cb, wdn, pproj, pgate)],
        out_specs=tok(D),
        out_shape=jax.ShapeDtypeStruct((B, S, D), F32),
        scratch_shapes=[pltpu.VMEM((nc, HALO, fc2), F32), pltpu.VMEM((ts + HALO, fc2), F32),
                        pltpu.VMEM((ts + HALO, fc2), F32), pltpu.VMEM((ts, D), F32), pltpu.VMEM((ts, D), BF16)],
        compiler_params=_cparams(("arbitrary", "arbitrary")),
        name="ffn_ple",
    )(x, p, fn, wup, cw, cb, wdn, pproj, pgate)


def kernel(x, p, positions, attn_norm, w_in, mla_q_norm, mla_w_uq, mla_kv_norm, mla_w_ukv, mla_q_gain, mla_k_gain, swa_q_gain, swa_k_gain, swa_sinks, moba_q_gain, moba_k_gain, group_norm, w_o, ffn_norm, w_up, conv_w, conv_b, w_down, ple_proj, ple_gate):
    B, S, D = x.shape
    depth = w_in.shape[0]
    assert D == D_MODEL and S % 512 == 0
    tables = _rope_tables(positions)
    for i in range(depth):
        pw = _prep_weights(w_in[i], mla_q_norm[i], mla_w_uq[i], mla_kv_norm[i], mla_w_ukv[i], mla_q_gain[i],
                           mla_k_gain[i], swa_q_gain[i], swa_k_gain[i], moba_q_gain[i], moba_k_gain[i])
        (qa, ka, va, qm, km, vm, qc, kc, vc, qd, qd32, kd, vd, kmean) = _prep(x, attn_norm[i], tables, pw)
        o_a = _sb_attn(qa, ka, va)
        o_b = _mla_attn(qm, km, vm)
        o_c = _swa_attn(qc, kc, vc, swa_sinks[i])
        o_d = _moba_attn(qd, qd32, kd, vd, kmean)
        x = _oproj((o_a, o_b, o_c, o_d), group_norm[i], w_o[i].astype(BF16), x)
        fw = _ffn_weights(ffn_norm[i], w_up[i], conv_w[i], conv_b[i], w_down[i], ple_proj[i], ple_gate[i])
        x = _ffn(x, p[i], fw)
    return x
```

```python
import functools

import jax
import jax.numpy as jnp
from jax import lax
from jax.experimental import pallas as pl
from jax.experimental.pallas import tpu as pltpu

F32 = jnp.float32
BF16 = jnp.bfloat16

D_MODEL = 1024
HEAD_DIM = 64
GROUP_HEADS = 4
GROUP_WIDTH = GROUP_HEADS * HEAD_DIM
N_GROUPS = 4
ROPE_THETA = 10000.0
EPS = 1e-6
NEG = -1e30
LOG2E = 1.4426950408889634

MLA_Q_RANK = 192
MLA_KV_RANK = 128
MLA_NOPE = 64
MLA_ROPE = 32
MLA_V = 64
MLA_QK = MLA_NOPE + MLA_ROPE
SWA_KV_HEADS = 2
WINDOW = 128
MOBA_BLOCK = 256
MOBA_TOPK = 3
D_FF = 2816
CONV_WIDTH = 3
PLE_DIM = 256

SB_COLS = 3 * GROUP_WIDTH
MLA_COLS = MLA_Q_RANK + MLA_KV_RANK + MLA_ROPE
SWA_COLS = GROUP_WIDTH + 2 * SWA_KV_HEADS * HEAD_DIM
MOBA_COLS = 3 * GROUP_WIDTH
OFF_SB = 0
OFF_MLA = OFF_SB + SB_COLS
OFF_SWA = OFF_MLA + MLA_COLS
OFF_MOBA = OFF_SWA + SWA_COLS

LANES = 128
SEG = LANES
Q_PAD = GROUP_HEADS * SEG
N_IN_PAD = 2560
VMEM_LIMIT = 56 * 1024 * 1024

PREP_TS = 256
OUT_TS = 512
FFN_TS = 512
FFN_FC = 256
SB_T = 128
SB_EXIT = -45.0
MLA_T = 256
MLA_WIDE = 2
HALO = 8
ONES_ROWS = 16


def _dot(a, b):
    return jnp.dot(a, b, preferred_element_type=F32)


def _dot_nt(a, b):
    return lax.dot_general(a, b, (((1,), (1,)), ((), ())), preferred_element_type=F32)


def _split_bf16(x):
    hi = x.astype(BF16)
    lo = (x - hi.astype(F32)).astype(BF16)
    return hi, lo


def _cparams(sem):
    return pltpu.CompilerParams(dimension_semantics=sem, vmem_limit_bytes=VMEM_LIMIT)


def _const_spec(shape):
    nd = len(shape)
    return pl.BlockSpec(shape, lambda *_: (0,) * nd)


def _rope_tables_kernel(pos_ref, inv_ref, c64_ref, s64_ref, cm_ref, sm_ref):
    pos = pos_ref[0].astype(F32)
    lane = lax.broadcasted_iota(jnp.int32, (1, LANES), 1)
    ang = pos * inv_ref[0:1, :]
    first = (lane & (HEAD_DIM - 1)) < (HEAD_DIM // 2)
    c64_ref[0] = jnp.cos(ang)
    s64_ref[0] = jnp.where(first, -jnp.sin(ang), jnp.sin(ang))
    angm = pos * inv_ref[1:2, :]
    in_rope = (lane >= MLA_NOPE) & (lane < MLA_QK)
    rope_first = lane < MLA_NOPE + MLA_ROPE // 2
    cm_ref[0] = jnp.where(in_rope, jnp.cos(angm), 1.0)
    sm_ref[0] = jnp.where(in_rope, jnp.where(rope_first, -jnp.sin(angm), jnp.sin(angm)), 0.0)


def _rope_tables(positions):
    B, S = positions.shape
    ts = 512
    half64 = HEAD_DIM // 2
    halfm = MLA_ROPE // 2
    inv64 = ROPE_THETA ** (-jnp.arange(half64, dtype=F32) / half64)
    invm = ROPE_THETA ** (-jnp.arange(halfm, dtype=F32) / halfm)
    lane = jnp.arange(LANES)
    row0 = inv64[lane % half64]
    row1 = jnp.where((lane >= MLA_NOPE) & (lane < MLA_QK), invm[(lane - MLA_NOPE) % halfm], 0.0)
    inv = jnp.zeros((8, LANES), F32).at[0].set(row0).at[1].set(row1)
    tab = jax.ShapeDtypeStruct((B, S, LANES), F32)
    spec = pl.BlockSpec((1, ts, LANES), lambda b, i: (b, i, 0))
    return pl.pallas_call(
        _rope_tables_kernel,
        grid=(B, S // ts),
        in_specs=[pl.BlockSpec((1, ts, 1), lambda b, i: (b, i, 0)), _const_spec((8, LANES))],
        out_specs=[spec] * 4,
        out_shape=[tab] * 4,
        compiler_params=_cparams(("parallel", "parallel")),
        name="rope_tables",
    )(positions.reshape(B, S, 1), inv)


def _rope64(x, cos, sin_signed, lane):
    first = (lane & (HEAD_DIM - 1)) < (HEAD_DIM // 2)
    partner = jnp.where(first, pltpu.roll(x, LANES - HEAD_DIM // 2, 1), pltpu.roll(x, HEAD_DIM // 2, 1))
    return x * cos + partner * sin_signed


def _rope_mla(x, cos, sin_signed, lane):
    first = lane < MLA_NOPE + MLA_ROPE // 2
    partner = jnp.where(first, pltpu.roll(x, LANES - MLA_ROPE // 2, 1), pltpu.roll(x, MLA_ROPE // 2, 1))
    return x * cos + partner * sin_signed


def _ms64(x, bsel):
    hi, lo = _split_bf16(x * x)
    return (_dot(hi, bsel) + _dot(lo, bsel)) * (1.0 / HEAD_DIM)


def _prep_kernel(x_ref, an_ref, win_ref, c64_ref, s64_ref, cm_ref, sm_ref,
                 qn_ref, wuq_ref, kvn_ref, wukv_ref, qg_ref, kg_ref, g64_ref,
                 qa_ref, ka_ref, va_ref, qm_ref, km_ref, vm_ref, qc_ref, kc_ref, vc_ref,
                 qd_ref, qd32_ref, kd_ref, vd_ref, kmean_ref):
    ts = x_ref.shape[1]
    x = x_ref[0]
    ms = jnp.mean(x * x, axis=-1, keepdims=True)
    h = (x * lax.rsqrt(ms + EPS) * an_ref[...]).astype(BF16)
    proj = _dot(h, win_ref[...])

    lane = lax.broadcasted_iota(jnp.int32, (1, LANES), 1)
    lo_half = lane < HEAD_DIM
    r = lax.broadcasted_iota(jnp.int32, (LANES, LANES), 0) // HEAD_DIM
    c = lax.broadcasted_iota(jnp.int32, (LANES, LANES), 1) // HEAD_DIM
    bsel = jnp.where(r == c, 1.0, 0.0).astype(BF16)
    c64, s64, cm, sm = c64_ref[0], s64_ref[0], cm_ref[0], sm_ref[0]
    scale64 = HEAD_DIM ** -0.5

    def seg(off, j):
        return proj[:, off + j * SEG: off + (j + 1) * SEG]

    def store_pair(ref, val, seg_lo, seg_hi):
        ref[0, :, seg_lo * SEG:(seg_lo + 1) * SEG] = jnp.where(lo_half, val, 0.0).astype(BF16)
        ref[0, :, seg_hi * SEG:(seg_hi + 1) * SEG] = jnp.where(lo_half, 0.0, val).astype(BF16)

    for p in range(2):
        store_pair(qa_ref, seg(0, p) * scale64, 2 * p, 2 * p + 1)
    ka_ref[0] = proj[:, 256:512].astype(BF16)
    va_ref[0] = proj[:, 512:768].astype(BF16)

    cq = proj[:, 768:1024]
    cqn = (cq * lax.rsqrt(jnp.sum(cq * cq, axis=-1, keepdims=True) * (1.0 / MLA_Q_RANK) + EPS)
           * qn_ref[...]).astype(BF16)
    qm_raw = _dot(cqn, wuq_ref[...])
    ckv = proj[:, 1024:1152]
    ckvn = (ckv * lax.rsqrt(jnp.mean(ckv * ckv, axis=-1, keepdims=True) + EPS) * kvn_ref[...]).astype(BF16)
    kv = _dot(ckvn, wukv_ref[...])
    kpe = proj[:, 1152:1280]

    def store_transposed(ref, v):
        for blk in range(ts // MOBA_BLOCK):
            ref[0, blk] = v[blk * MOBA_BLOCK:(blk + 1) * MOBA_BLOCK, :].T.astype(BF16)

    store_transposed(vm_ref, kv[:, 512:768])
    scale_m = MLA_QK ** -0.5 * LOG2E
    for hh in range(GROUP_HEADS):
        sl = slice(hh * SEG, (hh + 1) * SEG)
        qs = qm_raw[:, sl]
        qs = qs * lax.rsqrt(jnp.sum(qs * qs, axis=-1, keepdims=True) * (1.0 / MLA_QK) + EPS) * qg_ref[...]
        qm_ref[0, :, sl] = (_rope_mla(qs, cm, sm, lane) * scale_m).astype(BF16)
        ks = kv[:, sl] + kpe
        ks = ks * lax.rsqrt(jnp.sum(ks * ks, axis=-1, keepdims=True) * (1.0 / MLA_QK) + EPS) * kg_ref[...]
        km_ref[0, :, sl] = _rope_mla(ks, cm, sm, lane).astype(BF16)

    def norm_rope64(v, gain_row):
        v = v * lax.rsqrt(_ms64(v, bsel) + EPS) * g64_ref[gain_row:gain_row + 1, :]
        return _rope64(v, c64, s64, lane)

    qa_pair = norm_rope64(seg(1280, 0), 0) * scale64
    qb_pair = norm_rope64(seg(1280, 1), 0) * scale64
    store_pair(qc_ref, qa_pair, 0, 2)
    store_pair(qc_ref, qb_pair, 1, 3)
    kc_ref[0] = norm_rope64(seg(1536, 0), 1).astype(BF16)
    vc_ref[0] = seg(1664, 0).astype(BF16)

    for p in range(2):
        qd = norm_rope64(seg(1792, p), 2)
        qd32_ref[0, :, p * SEG:(p + 1) * SEG] = qd
        store_pair(qd_ref, qd * (scale64 * LOG2E), 2 * p, 2 * p + 1)
        kd = norm_rope64(seg(2048, p), 3)
        kd_ref[0, :, p * SEG:(p + 1) * SEG] = kd.astype(BF16)
        for blk in range(ts // MOBA_BLOCK):
            kmean_ref[0, blk, :, p * SEG:(p + 1) * SEG] = jnp.mean(
                kd[blk * MOBA_BLOCK:(blk + 1) * MOBA_BLOCK], axis=0, keepdims=True)
    store_transposed(vd_ref, proj[:, 2304:2560])


def _prep_weights(w_in, mla_q_norm, mla_w_uq, mla_kv_norm, mla_w_ukv, mla_q_gain, mla_k_gain,
                  swa_q_gain, swa_k_gain, moba_q_gain, moba_k_gain):
    D = w_in.shape[0]
    z = lambda n: jnp.zeros((D, n), F32)
    mla = w_in[:, OFF_MLA:OFF_MLA + MLA_COLS]
    swa = w_in[:, OFF_SWA:OFF_SWA + SWA_COLS]
    swa_q = swa[:, :GROUP_WIDTH].reshape(D, GROUP_HEADS, HEAD_DIM)[:, jnp.array([0, 2, 1, 3])].reshape(D, GROUP_WIDTH)
    win = jnp.concatenate([
        w_in[:, OFF_SB:OFF_SB + SB_COLS],
        mla[:, :MLA_Q_RANK], z(256 - MLA_Q_RANK),
        mla[:, MLA_Q_RANK:MLA_Q_RANK + MLA_KV_RANK],
        z(MLA_NOPE), mla[:, MLA_Q_RANK + MLA_KV_RANK:], z(SEG - MLA_QK),
        swa_q, swa[:, GROUP_WIDTH:],
        w_in[:, OFF_MOBA:OFF_MOBA + MOBA_COLS],
    ], axis=1).astype(BF16)
    assert win.shape[1] == N_IN_PAD
    qn = jnp.pad(mla_q_norm, (0, 256 - MLA_Q_RANK)).reshape(1, 256)
    wuq = mla_w_uq.reshape(MLA_Q_RANK, GROUP_HEADS, MLA_QK)
    wuq = jnp.pad(wuq, ((0, 256 - MLA_Q_RANK), (0, 0), (0, SEG - MLA_QK))).reshape(256, Q_PAD).astype(BF16)
    wukv = mla_w_ukv.reshape(MLA_KV_RANK, GROUP_HEADS, MLA_NOPE + MLA_V)
    wkn = jnp.pad(wukv[:, :, :MLA_NOPE], ((0, 0), (0, 0), (0, SEG - MLA_NOPE))).reshape(MLA_KV_RANK, Q_PAD)
    wv = wukv[:, :, MLA_NOPE:].reshape(MLA_KV_RANK, GROUP_WIDTH)
    wukv_r = jnp.concatenate([wkn, wv], axis=1).astype(BF16)
    qg = jnp.pad(mla_q_gain, (0, SEG - MLA_QK)).reshape(1, SEG)
    kg = jnp.pad(mla_k_gain, (0, SEG - MLA_QK)).reshape(1, SEG)
    g64 = jnp.zeros((8, LANES), F32)
    for row, g in enumerate((swa_q_gain, swa_k_gain, moba_q_gain, moba_k_gain)):
        g64 = g64.at[row].set(jnp.tile(g, LANES // HEAD_DIM))
    return win, qn, wuq, mla_kv_norm.reshape(1, MLA_KV_RANK), wukv_r, qg, kg, g64


def _prep(x, attn_norm, tables, weights):
    B, S, D = x.shape
    ts = PREP_TS
    win, qn, wuq, kvn, wukv, qg, kg, g64 = weights
    tok = lambda w: pl.BlockSpec((1, ts, w), lambda b, i: (b, i, 0))
    nb_t = ts // MOBA_BLOCK
    VT = "channel-major value blocks"
    out_widths = [(Q_PAD, BF16), (256, BF16), (256, BF16),
                  (Q_PAD, BF16), (Q_PAD, BF16), (VT, BF16),
                  (Q_PAD, BF16), (128, BF16), (128, BF16),
                  (Q_PAD, BF16), (256, F32), (256, BF16), (VT, BF16)]
    vt_shape = jax.ShapeDtypeStruct((B, S // MOBA_BLOCK, GROUP_WIDTH, MOBA_BLOCK), BF16)
    vt_spec = pl.BlockSpec((1, nb_t, GROUP_WIDTH, MOBA_BLOCK), lambda b, i: (b, i, 0, 0))
    out_shape = [vt_shape if w is VT else jax.ShapeDtypeStruct((B, S, w), dt) for w, dt in out_widths]
    out_specs = [vt_spec if w is VT else tok(w) for w, _ in out_widths]
    out_shape.append(jax.ShapeDtypeStruct((B, S // MOBA_BLOCK, 1, 256), F32))
    out_specs.append(pl.BlockSpec((1, nb_t, 1, 256), lambda b, i: (b, i, 0, 0)))
    consts = [attn_norm.reshape(1, D), win]
    tail = [qn, wuq, kvn, wukv, qg, kg, g64]
    in_specs = ([tok(D)] + [_const_spec(a.shape) for a in consts] + [tok(LANES)] * 4
                + [_const_spec(a.shape) for a in tail])
    return pl.pallas_call(
        _prep_kernel,
        grid=(B, S // ts),
        in_specs=in_specs,
        out_specs=out_specs,
        out_shape=out_shape,
        compiler_params=_cparams(("parallel", "parallel")),
        name="prep",
    )(x, *consts, *tables, *tail)


def _sb_kernel(q_ref, k_ref, v_ref, o_ref, acc_ref, carry_ref):
    T = SB_T
    i = pl.program_id(1)
    row = lax.broadcasted_iota(jnp.int32, (T, T), 0)
    col = lax.broadcasted_iota(jnp.int32, (T, T), 1)
    upper = jnp.concatenate([jnp.where(row > col, 1.0, 0.0), jnp.ones((T, T), F32)], axis=1).astype(BF16)
    causal = col < row
    lane = lax.broadcasted_iota(jnp.int32, (1, LANES), 1)

    heads = range(GROUP_HEADS)

    def block(j, first):
        ks = pl.ds(pl.multiple_of(j * T, T), T)
        kv = lambda ref, hh: ref[0, ks, (hh // 2) * SEG:(hh // 2 + 1) * SEG]
        z = [_dot_nt(q_ref[0, :, hh * SEG:(hh + 1) * SEG], kv(k_ref, hh)) for hh in heads]
        log_beta = [jnp.minimum(zz, 0.0) - jnp.log(1.0 + jnp.exp(-jnp.abs(zz))) for zz in z]
        log_keep = [lb - zz for lb, zz in zip(log_beta, z)]
        if first:
            log_keep = [jnp.where(causal, lk, 0.0) for lk in log_keep]
        parts = [_split_bf16(lk) for lk in log_keep]
        sums = [_dot(hi, upper) + _dot(lo, upper) for hi, lo in parts]
        if first:
            w = [jnp.where(causal, jnp.exp(lb + sm[:, :T]), 0.0) for lb, sm in zip(log_beta, sums)]
            carry = [sm[:, T:] for sm in sums]
        else:
            w = [jnp.exp(lb + (sm[:, :T] + carry_ref[hh])) for hh, lb, sm in zip(heads, log_beta, sums)]
            carry = [carry_ref[hh] + sm[:, T:] for hh, sm in zip(heads, sums)]
        pv = [_dot(ww.astype(BF16), kv(v_ref, hh)) for hh, ww in zip(heads, w)]
        for hh in heads:
            acc_ref[hh] = pv[hh] if first else acc_ref[hh] + pv[hh]
            carry_ref[hh] = carry[hh]
        return jnp.max(jnp.maximum(jnp.maximum(carry[0], carry[1]), jnp.maximum(carry[2], carry[3])))

    def cond(st):
        j, mx = st
        return jnp.logical_and(j >= 0, mx > SB_EXIT)

    def body(st):
        return st[0] - 1, block(st[0], False)

    lax.while_loop(cond, body, (i - 1, block(i, True)))

    lo_half = lane < HEAD_DIM
    for p in range(2):
        o_ref[0, :, p * SEG:(p + 1) * SEG] = jnp.where(lo_half, acc_ref[2 * p], acc_ref[2 * p + 1]).astype(o_ref.dtype)


def _sb_attn(q, k, v):
    B, S, _ = k.shape
    T = SB_T
    full = pl.BlockSpec((1, S, GROUP_WIDTH), lambda b, i: (b, 0, 0))
    return pl.pallas_call(
        _sb_kernel,
        grid=(B, S // T),
        in_specs=[pl.BlockSpec((1, T, Q_PAD), lambda b, i: (b, i, 0)), full, full],
        out_specs=pl.BlockSpec((1, T, GROUP_WIDTH), lambda b, i: (b, i, 0)),
        out_shape=jax.ShapeDtypeStruct((B, S, GROUP_WIDTH), BF16),
        scratch_shapes=[pltpu.VMEM((GROUP_HEADS, T, SEG), F32), pltpu.VMEM((GROUP_HEADS, T, T), F32)],
        compiler_params=_cparams(("parallel", "arbitrary")),
        name="sb_attn",
    )(q, k, v)


def _softmax_update_all(acc_ref, m_ref, heads, scores, vt_augs):
    m_old = [m_ref[hh][0:1, :] for hh in heads]
    acc_old = [acc_ref[hh] for hh in heads]
    m_new = [jnp.maximum(mo, jnp.max(s, axis=0, keepdims=True)) for mo, s in zip(m_old, scores)]
    p = [jnp.exp2(s - mn).astype(BF16) for s, mn in zip(scores, m_new)]
    pv = [_dot(va, pp) for va, pp in zip(vt_augs, p)]
    for hh, mo, mn, r, a in zip(heads, m_old, m_new, pv, acc_old):
        acc_ref[hh] = jnp.exp2(mo - mn) * a + r
        m_ref[hh] = jnp.broadcast_to(mn, m_ref.shape[1:])


def _values_t(vt_ref, first_blk, nblk, hh):
    rows = slice(hh * HEAD_DIM, (hh + 1) * HEAD_DIM)
    v = jnp.concatenate([vt_ref[0, first_blk + r, rows, :] for r in range(nblk)], axis=1)
    return jnp.concatenate([v, jnp.ones((ONES_ROWS, v.shape[1]), v.dtype)], axis=0)


def _softmax_pipelined(n_steps, qk_pair, vt_ref, acc_ref, m_ref, s_ref, p_ref, a_ref):
    pairs = ((0, 1), (2, 3))
    s_ref[0] = jnp.full(s_ref.shape[1:], -jnp.inf, F32)
    p_ref[0] = jnp.zeros(p_ref.shape[1:], BF16)
    a_ref[0] = jnp.ones(a_ref.shape[1:], F32)

    def softmax(m_old, s):
        m_new = jnp.maximum(m_old, jnp.max(s, axis=0, keepdims=True))
        return jnp.exp2(s - m_new).astype(BF16), jnp.exp2(m_old - m_new), m_new

    def half_step(j, rd, wr, with_scores):
        prev = jnp.maximum(j - 1, 0) * MLA_WIDE
        m_old = [m_ref[hh][0:1, :] for hh in range(GROUP_HEADS)]
        acc_old = [acc_ref[hh] for hh in range(GROUP_HEADS)]
        s1_prev = [s_ref[rd, n] for n in range(2)]
        p0_prev = [p_ref[rd, n] for n in range(2)]
        a0_prev = [a_ref[rd, n][0:1, :] for n in range(2)]
        vt = [_values_t(vt_ref, prev, MLA_WIDE, hh) for hh in range(GROUP_HEADS)]

        acc = [a0_prev[n] * acc_old[hh] + _dot(vt[hh], p0_prev[n]) for n, hh in enumerate(pairs[0])]
        s0 = qk_pair(0, j * MLA_WIDE) if with_scores else None
        r1 = [softmax(m_old[hh], s1_prev[n]) for n, hh in enumerate(pairs[1])]
        s1 = qk_pair(1, j * MLA_WIDE) if with_scores else None
        acc += [r1[n][1] * acc_old[hh] + _dot(vt[hh], r1[n][0]) for n, hh in enumerate(pairs[1])]
        m_new = [None, None, r1[0][2], r1[1][2]]
        if with_scores:
            r0 = [softmax(m_old[hh], s0[n]) for n, hh in enumerate(pairs[0])]
            m_new[0:2] = [r0[0][2], r0[1][2]]

        for hh in range(GROUP_HEADS):
            acc_ref[hh] = acc[hh]
            if m_new[hh] is not None:
                m_ref[hh] = jnp.broadcast_to(m_new[hh], m_ref.shape[1:])
        if with_scores:
            for n in range(2):
                p_ref[wr, n] = r0[n][0]
                a_ref[wr, n] = jnp.broadcast_to(r0[n][1], a_ref.shape[2:])
                s_ref[wr, n] = s1[n]

    def body(u, carry):
        half_step(2 * u, 0, 1, True)
        half_step(2 * u + 1, 1, 0, True)
        return carry

    lax.fori_loop(0, n_steps // 2, body, 0)

    @pl.when(n_steps % 2 == 1)
    def _():
        half_step(n_steps - 1, 0, 1, True)
        half_step(n_steps, 1, None, False)

    @pl.when(n_steps % 2 == 0)
    def _():
        half_step(n_steps, 0, None, False)


def _softmax_finish(acc_ref, o_ref):
    for p in range(2):
        halves = [acc_ref[hh][0:HEAD_DIM, :] / acc_ref[hh][HEAD_DIM:HEAD_DIM + 1, :] for hh in (2 * p, 2 * p + 1)]
        o_ref[0, :, p * SEG:(p + 1) * SEG] = jnp.concatenate(halves, axis=0).T.astype(o_ref.dtype)


def _mla_kernel(q_ref, k_ref, vt_ref, o_ref, acc_ref, m_ref, s_ref, p_ref, a_ref):
    T = MLA_T
    i = pl.program_id(1)
    key = lax.broadcasted_iota(jnp.int32, (T, T), 0)
    qry = lax.broadcasted_iota(jnp.int32, (T, T), 1)
    causal = key <= qry

    def scores(hh, first_blk, nblk):
        ks = pl.ds(pl.multiple_of(first_blk * T, T), nblk * T)
        return _dot_nt(k_ref[0, ks, hh * SEG:(hh + 1) * SEG], q_ref[0, :, hh * SEG:(hh + 1) * SEG])

    def step(first_blk, nblk, masked):
        heads = range(GROUP_HEADS)
        s = [scores(hh, first_blk, nblk) for hh in heads]
        if masked:
            s = [jnp.where(causal, ss, NEG) for ss in s]
        vt = [_values_t(vt_ref, first_blk, nblk, hh) for hh in heads]
        _softmax_update_all(acc_ref, m_ref, heads, s, vt)

    m_ref[...] = jnp.full(m_ref.shape, NEG, F32)
    acc_ref[...] = jnp.zeros(acc_ref.shape, F32)
    _softmax_pipelined(i // MLA_WIDE, lambda pair, blk: [scores(2 * pair + n, blk, MLA_WIDE) for n in range(2)],
                       vt_ref, acc_ref, m_ref, s_ref, p_ref, a_ref)
    for r in range(MLA_WIDE - 1):
        @pl.when(i % MLA_WIDE > r)
        def _(r=r):
            step(i // MLA_WIDE * MLA_WIDE + r, 1, False)
    step(i, 1, True)
    _softmax_finish(acc_ref, o_ref)


def _softmax_scratch(T):
    tk = MLA_WIDE * T
    return [pltpu.VMEM((GROUP_HEADS, HEAD_DIM + ONES_ROWS, T), F32), pltpu.VMEM((GROUP_HEADS, 8, T), F32),
            pltpu.VMEM((2, 2, tk, T), F32), pltpu.VMEM((2, 2, tk, T), BF16), pltpu.VMEM((2, 2, 8, T), F32)]


def _mla_attn(q, k, vt):
    B, S, _ = k.shape
    T = MLA_T
    assert T == MOBA_BLOCK
    return pl.pallas_call(
        _mla_kernel,
        grid=(B, S // T),
        in_specs=[pl.BlockSpec((1, T, Q_PAD), lambda b, i: (b, i, 0)),
                  pl.BlockSpec((1, S, Q_PAD), lambda b, i: (b, 0, 0)),
                  pl.BlockSpec((1,) + vt.shape[1:], lambda b, i: (b, 0, 0, 0))],
        out_specs=pl.BlockSpec((1, T, GROUP_WIDTH), lambda b, i: (b, i, 0)),
        out_shape=jax.ShapeDtypeStruct((B, S, GROUP_WIDTH), BF16),
        scratch_shapes=_softmax_scratch(T),
        compiler_params=_cparams(("parallel", "arbitrary")),
        name="mla_attn",
    )(q, k, vt)


def _swa_kernel(sink_ref, q_ref, k_ref, v_ref, o_ref):
    W = WINDOW
    i = pl.program_id(1)
    start = pl.multiple_of(jnp.maximum(i - 1, 0) * W, W)
    k = k_ref[0, pl.ds(start, 2 * W), :]
    v = v_ref[0, pl.ds(start, 2 * W), :]
    qpos = i * W + lax.broadcasted_iota(jnp.int32, (W, 2 * W), 0)
    kpos = start + lax.broadcasted_iota(jnp.int32, (W, 2 * W), 1)
    band = jnp.logical_and(kpos <= qpos, qpos - kpos < W)
    lane = lax.broadcasted_iota(jnp.int32, (1, LANES), 1)
    lo_half = lane < HEAD_DIM
    outs = []
    for hh in range(GROUP_HEADS):
        s = _dot_nt(q_ref[0, :, hh * SEG:(hh + 1) * SEG], k)
        s = jnp.where(band, s, NEG)
        sink = sink_ref[hh]
        m = jnp.maximum(jnp.max(s, axis=1, keepdims=True), sink)
        p = jnp.exp(s - m)
        denom = jnp.sum(p, axis=1, keepdims=True) + jnp.exp(sink - m)
        outs.append(_dot(p.astype(BF16), v) / denom)
    o_ref[0, :, 0:SEG] = jnp.where(lo_half, outs[0], pltpu.roll(outs[1], HEAD_DIM, 1)).astype(o_ref.dtype)
    o_ref[0, :, SEG:2 * SEG] = jnp.where(lo_half, pltpu.roll(outs[2], HEAD_DIM, 1), outs[3]).astype(o_ref.dtype)


def _swa_attn(q, k, v, sinks):
    B, S, _ = k.shape
    W = WINDOW
    full = pl.BlockSpec((1, S, 2 * HEAD_DIM), lambda b, i: (b, 0, 0))
    return pl.pallas_call(
        _swa_kernel,
        grid=(B, S // W),
        in_specs=[pl.BlockSpec(memory_space=pltpu.SMEM),
                  pl.BlockSpec((1, W, Q_PAD), lambda b, i: (b, i, 0)), full, full],
        out_specs=pl.BlockSpec((1, W, GROUP_WIDTH), lambda b, i: (b, i, 0)),
        out_shape=jax.ShapeDtypeStruct((B, S, GROUP_WIDTH), BF16),
        compiler_params=_cparams(("parallel", "parallel")),
        name="swa_attn",
    )(sinks, q, k, v)


def _moba_kernel(q_ref, q32_ref, k_ref, vt_ref, kmean_ref, o_ref, acc_ref, m_ref, s_ref, p_ref, a_ref, qaug_ref):
    T = MOBA_BLOCK
    own = pl.program_id(1)
    nbr = kmean_ref.shape[1]
    key = lax.broadcasted_iota(jnp.int32, (T, T), 0)
    qry = lax.broadcasted_iota(jnp.int32, (T, T), 1)
    causal = key <= qry
    lo_half = lax.broadcasted_iota(jnp.int32, (1, LANES), 1) < HEAD_DIM
    blk = lax.broadcasted_iota(jnp.int32, (nbr, T), 0).astype(F32)
    kblk = lax.broadcasted_iota(jnp.int32, (T, LANES), 1)
    own_f = own.astype(F32)

    m_ref[...] = jnp.full(m_ref.shape, NEG, F32)
    acc_ref[...] = jnp.zeros(acc_ref.shape, F32)
    own_ks = pl.ds(pl.multiple_of(own * T, T), T)

    heads = range(GROUP_HEADS)
    pair_sl = [slice((hh // 2) * SEG, (hh // 2 + 1) * SEG) for hh in heads]

    s = [jnp.where(causal, _dot_nt(k_ref[0, own_ks, pair_sl[hh]], q_ref[0, :, hh * SEG:(hh + 1) * SEG]), NEG)
         for hh in heads]
    _softmax_update_all(acc_ref, m_ref, heads, s, [_values_t(vt_ref, own, 1, hh) for hh in heads])

    g = []
    for hh in heads:
        head_lanes = lo_half if hh % 2 == 0 else jnp.logical_not(lo_half)
        qh_hi, qh_lo = _split_bf16(jnp.where(head_lanes, q32_ref[0, :, pair_sl[hh]], 0.0))
        km_hi, km_lo = _split_bf16(kmean_ref[0, :, pair_sl[hh]])
        gate = _dot_nt(km_hi, qh_hi) + (_dot_nt(km_lo, qh_hi) + _dot_nt(km_hi, qh_lo))
        g.append(jnp.where(blk < own_f, gate, NEG))
    sel = [jnp.zeros((nbr, T), F32) for _ in heads]
    for t in range(MOBA_TOPK):
        mx = [jnp.max(gg, axis=0, keepdims=True) for gg in g]
        first = [jnp.min(jnp.where(gg == m, blk, float(nbr)), axis=0, keepdims=True)
                 for gg, m in zip(g, mx)]
        hit = [blk == f for f in first]
        sel = [jnp.where(jnp.logical_and(h, t < own), 1.0, sl) for h, sl in zip(hit, sel)]
        g = [jnp.where(h, -jnp.inf, gg) for h, gg in zip(hit, g)]
    for hh in heads:
        neg_t = jnp.concatenate([(1.0 - sel[hh]) * NEG, jnp.zeros((LANES - nbr, T), F32)], axis=0)
        qaug_ref[hh, :, 0:SEG] = q_ref[0, :, hh * SEG:(hh + 1) * SEG]
        qaug_ref[hh, :, SEG:2 * SEG] = neg_t.T.astype(BF16)

    def scores_pair(pair, n0, nblk):
        ks = pl.ds(pl.multiple_of(n0 * T, T), nblk * T)
        onehot = jnp.concatenate([jnp.where(kblk == n0 + r, 1.0, 0.0) for r in range(nblk)], axis=0).astype(BF16)
        k_aug = jnp.concatenate([k_ref[0, ks, pair * SEG:(pair + 1) * SEG], onehot], axis=1)
        return [_dot_nt(k_aug, qaug_ref[2 * pair + n]) for n in range(2)]

    def step(n0, nblk):
        s = scores_pair(0, n0, nblk) + scores_pair(1, n0, nblk)
        _softmax_update_all(acc_ref, m_ref, heads, s, [_values_t(vt_ref, n0, nblk, hh) for hh in heads])

    _softmax_pipelined(own // MLA_WIDE, lambda pair, blk: scores_pair(pair, blk, MLA_WIDE),
                       vt_ref, acc_ref, m_ref, s_ref, p_ref, a_ref)
    for r in range(MLA_WIDE - 1):
        @pl.when(own % MLA_WIDE > r)
        def _(r=r):
            step(own // MLA_WIDE * MLA_WIDE + r, 1)
    _softmax_finish(acc_ref, o_ref)


def _moba_attn(q, q32, k, vt, kmean):
    B, S, _ = k.shape
    T = MOBA_BLOCK
    assert S // T <= LANES
    nbr = -(-(S // T) // 16) * 16
    kmean = jnp.pad(kmean.reshape(B, S // T, GROUP_WIDTH), ((0, 0), (0, nbr - S // T), (0, 0)))
    return pl.pallas_call(
        _moba_kernel,
        grid=(B, S // T),
        in_specs=[pl.BlockSpec((1, T, Q_PAD), lambda b, i: (b, i, 0)),
                  pl.BlockSpec((1, T, GROUP_WIDTH), lambda b, i: (b, i, 0)),
                  pl.BlockSpec((1, S, GROUP_WIDTH), lambda b, i: (b, 0, 0)),
                  pl.BlockSpec((1,) + vt.shape[1:], lambda b, i: (b, 0, 0, 0)),
                  pl.BlockSpec((1, nbr, GROUP_WIDTH), lambda b, i: (b, 0, 0))],
        out_specs=pl.BlockSpec((1, T, GROUP_WIDTH), lambda b, i: (b, i, 0)),
        out_shape=jax.ShapeDtypeStruct((B, S, GROUP_WIDTH), BF16),
        scratch_shapes=_softmax_scratch(T) + [pltpu.VMEM((GROUP_HEADS, T, 2 * SEG), BF16)],
        compiler_params=_cparams(("parallel", "arbitrary")),
        name="moba_attn",
    )(q, q32, k, vt, kmean)


def _oproj_kernel(oa_ref, ob_ref, oc_ref, od_ref, gn_ref, wo_ref, x_ref, y_ref):
    acc = x_ref[0]
    for g, ref in enumerate((oa_ref, ob_ref, oc_ref, od_ref)):
        o = ref[0].astype(F32)
        ms = jnp.mean(o * o, axis=-1, keepdims=True)
        n = (o * lax.rsqrt(ms + EPS) * gn_ref[:, g * GROUP_WIDTH:(g + 1) * GROUP_WIDTH]).astype(BF16)
        acc = acc + _dot(n, wo_ref[g * GROUP_WIDTH:(g + 1) * GROUP_WIDTH, :])
    y_ref[0] = acc


def _oproj(groups, group_norm, w_o, x):
    B, S, D = x.shape
    ts = OUT_TS
    tok = lambda w: pl.BlockSpec((1, ts, w), lambda b, i: (b, i, 0))
    return pl.pallas_call(
        _oproj_kernel,
        grid=(B, S // ts),
        in_specs=[tok(GROUP_WIDTH)] * 4 + [_const_spec((1, D)), _const_spec(w_o.shape), tok(D)],
        out_specs=tok(D),
        out_shape=jax.ShapeDtypeStruct((B, S, D), F32),
        compiler_params=_cparams(("parallel", "parallel")),
        name="oproj",
    )(*groups, group_norm.reshape(1, D), w_o, x)


def _ffn_kernel(x_ref, p_ref, fn_ref, wup_ref, cw_ref, cb_ref, wdn_ref, pproj_ref, pgate_ref, y_ref,
                halo_ref, ubuf0_ref, ubuf1_ref, acc_ref, h_ref):
    ts = x_ref.shape[1]
    fc2 = wup_ref.shape[2]
    fc = fc2 // 2
    n_chunks = wup_ref.shape[0]
    i = pl.program_id(1)

    @pl.when(i == 0)
    def _():
        halo_ref[...] = jnp.zeros(halo_ref.shape, F32)

    x = x_ref[0]
    ms = jnp.mean(x * x, axis=-1, keepdims=True)
    h_ref[...] = (x * lax.rsqrt(ms + EPS) * fn_ref[...]).astype(BF16)
    acc_ref[...] = x

    def up(c, ubuf_ref):
        u = _dot(h_ref[...], wup_ref[c])
        ubuf_ref[0:HALO, :] = halo_ref[c]
        ubuf_ref[HALO:HALO + ts, :] = u
        halo_ref[c] = u[ts - HALO:ts, :]

    def down(c, ubuf_ref):
        cw = cw_ref[c]
        y = (cw[0:1, :] * ubuf_ref[HALO - 2:HALO - 2 + ts, :] + cw[1:2, :] * ubuf_ref[HALO - 1:HALO - 1 + ts, :]
             + cw[2:3, :] * ubuf_ref[HALO:HALO + ts, :] + cb_ref[c])
        ya = y[:, :fc]
        g = ya * (1.0 / (1.0 + jnp.exp(-ya))) * y[:, fc:]
        acc_ref[...] += _dot(g.astype(BF16), wdn_ref[c])

    up(0, ubuf0_ref)

    def pair(cc, carry):
        up(2 * cc + 1, ubuf1_ref)
        down(2 * cc, ubuf0_ref)
        up(2 * cc + 2, ubuf0_ref)
        down(2 * cc + 1, ubuf1_ref)
        return carry

    assert n_chunks % 2 == 1
    lax.fori_loop(0, n_chunks // 2, pair, 0)
    down(n_chunks - 1, ubuf0_ref)
    x2 = acc_ref[...]
    gate = _dot(x2.astype(BF16), pgate_ref[...])
    emb = _dot(p_ref[0].astype(BF16), pproj_ref[...])
    y_ref[0] = x2 + emb * (1.0 / (1.0 + jnp.exp(-gate)))


def _ffn_weights(ffn_norm, w_up, conv_w, conv_b, w_down, ple_proj, ple_gate):
    fc = FFN_FC
    nc = D_FF // fc
    D = w_up.shape[0]
    pair = lambda a: jnp.concatenate([a[..., :D_FF].reshape(a.shape[:-1] + (nc, fc)),
                                      a[..., D_FF:].reshape(a.shape[:-1] + (nc, fc))], axis=-1)
    wup = jnp.moveaxis(pair(w_up), 1, 0).astype(BF16)
    cw = jnp.pad(jnp.moveaxis(pair(conv_w), 1, 0), ((0, 0), (0, 8 - CONV_WIDTH), (0, 0)))
    cb = jnp.moveaxis(pair(conv_b.reshape(1, -1)), 1, 0)
    wdn = w_down.reshape(nc, fc, D).astype(BF16)
    return ffn_norm.reshape(1, D), wup, cw, cb, wdn, ple_proj.astype(BF16), ple_gate.astype(BF16)


def _ffn(x, p, weights):
    B, S, D = x.shape
    ts = FFN_TS
    fn, wup, cw, cb, wdn, pproj, pgate = weights
    nc, _, fc2 = wup.shape
    tok = lambda w: pl.BlockSpec((1, ts, w), lambda b, i: (b, i, 0))
    once = lambda a: pl.BlockSpec(a.shape, lambda *_: (0,) * a.ndim, pipeline_mode=pl.Buffered(1))
    return pl.pallas_call(
        _ffn_kernel,
        grid=(B, S // ts),
        in_specs=[tok(D), tok(PLE_DIM)] + [once(a) for a in (fn, wup, cw, cb, wdn, pproj, pgate)],
        out_specs=tok(D),
        out_shape=jax.ShapeDtypeStruct((B, S, D), F32),
        scratch_shapes=[pltpu.VMEM((nc, HALO, fc2), F32), pltpu.VMEM((ts + HALO, fc2), F32),
                        pltpu.VMEM((ts + HALO, fc2), F32), pltpu.VMEM((ts, D), F32), pltpu.VMEM((ts, D), BF16)],
        compiler_params=_cparams(("arbitrary", "arbitrary")),
        name="ffn_ple",
    )(x, p, fn, wup, cw, cb, wdn, pproj, pgate)


def kernel(x, p, positions, attn_norm, w_in, mla_q_norm, mla_w_uq, mla_kv_norm, mla_w_ukv, mla_q_gain, mla_k_gain, swa_q_gain, swa_k_gain, swa_sinks, moba_q_gain, moba_k_gain, group_norm, w_o, ffn_norm, w_up, conv_w, conv_b, w_down, ple_proj, ple_gate):
    B, S, D = x.shape
    depth = w_in.shape[0]
    assert D == D_MODEL and S % 512 == 0
    tables = _rope_tables(positions)
    for i in range(depth):
        pw = _prep_weights(w_in[i], mla_q_norm[i], mla_w_uq[i], mla_kv_norm[i], mla_w_ukv[i], mla_q_gain[i],
                           mla_k_gain[i], swa_q_gain[i], swa_k_gain[i], moba_q_gain[i], moba_k_gain[i])
        (qa, ka, va, qm, km, vm, qc, kc, vc, qd, qd32, kd, vd, kmean) = _prep(x, attn_norm[i], tables, pw)
        o_a = _sb_attn(qa, ka, va)
        o_b = _mla_attn(qm, km, vm)
        o_c = _swa_attn(qc, kc, vc, swa_sinks[i])
        o_d = _moba_attn(qd, qd32, kd, vd, kmean)
        x = _oproj((o_a, o_b, o_c, o_d), group_norm[i], w_o[i].astype(BF16), x)
        fw = _ffn_weights(ffn_norm[i], w_up[i], conv_w[i], conv_b[i], w_down[i], ple_proj[i], ple_gate[i])
        x = _ffn(x, p[i], fw)
    return x
```

```python
import functools

import jax
import jax.numpy as jnp
from jax import lax
from jax.experimental import pallas as pl
from jax.experimental.pallas import tpu as pltpu

F32 = jnp.float32
BF16 = jnp.bfloat16

D_MODEL = 1024
HEAD_DIM = 64
GROUP_HEADS = 4
GROUP_WIDTH = GROUP_HEADS * HEAD_DIM
N_GROUPS = 4
ROPE_THETA = 10000.0
EPS = 1e-6
NEG = -1e30
LOG2E = 1.4426950408889634

MLA_Q_RANK = 192
MLA_KV_RANK = 128
MLA_NOPE = 64
MLA_ROPE = 32
MLA_V = 64
MLA_QK = MLA_NOPE + MLA_ROPE
SWA_KV_HEADS = 2
WINDOW = 128
MOBA_BLOCK = 256
MOBA_TOPK = 3
D_FF = 2816
CONV_WIDTH = 3
PLE_DIM = 256

SB_COLS = 3 * GROUP_WIDTH
MLA_COLS = MLA_Q_RANK + MLA_KV_RANK + MLA_ROPE
SWA_COLS = GROUP_WIDTH + 2 * SWA_KV_HEADS * HEAD_DIM
MOBA_COLS = 3 * GROUP_WIDTH
OFF_SB = 0
OFF_MLA = OFF_SB + SB_COLS
OFF_SWA = OFF_MLA + MLA_COLS
OFF_MOBA = OFF_SWA + SWA_COLS

LANES = 128
SEG = LANES
Q_PAD = GROUP_HEADS * SEG
N_IN_PAD = 2560
VMEM_LIMIT = 56 * 1024 * 1024

PREP_TS = 512
OUT_TS = 512
FFN_TS = 512
FFN_FC = 256
SWA_T = 512
SB_T = 128
SB_EXIT = -45.0
MLA_T = 256
MLA_WIDE = 2
HALO = 8
ONES_ROWS = 16


def _dot(a, b):
    return jnp.dot(a, b, preferred_element_type=F32)


def _dot_nt(a, b):
    return lax.dot_general(a, b, (((1,), (1,)), ((), ())), preferred_element_type=F32)


def _split_bf16(x):
    hi = x.astype(BF16)
    lo = (x - hi.astype(F32)).astype(BF16)
    return hi, lo


def _cparams(sem):
    return pltpu.CompilerParams(dimension_semantics=sem, vmem_limit_bytes=VMEM_LIMIT)


def _const_spec(shape):
    nd = len(shape)
    return pl.BlockSpec(shape, lambda *_: (0,) * nd)


def _rope_tables_kernel(pos_ref, inv_ref, c64_ref, s64_ref, cm_ref, sm_ref):
    pos = pos_ref[0].astype(F32)
    lane = lax.broadcasted_iota(jnp.int32, (1, LANES), 1)
    ang = pos * inv_ref[0:1, :]
    first = (lane & (HEAD_DIM - 1)) < (HEAD_DIM // 2)
    c64_ref[0] = jnp.cos(ang)
    s64_ref[0] = jnp.where(first, -jnp.sin(ang), jnp.sin(ang))
    angm = pos * inv_ref[1:2, :]
    in_rope = (lane >= MLA_NOPE) & (lane < MLA_QK)
    rope_first = lane < MLA_NOPE + MLA_ROPE // 2
    cm_ref[0] = jnp.where(in_rope, jnp.cos(angm), 1.0)
    sm_ref[0] = jnp.where(in_rope, jnp.where(rope_first, -jnp.sin(angm), jnp.sin(angm)), 0.0)


def _rope_tables(positions):
    B, S = positions.shape
    ts = 512
    half64 = HEAD_DIM // 2
    halfm = MLA_ROPE // 2
    inv64 = ROPE_THETA ** (-jnp.arange(half64, dtype=F32) / half64)
    invm = ROPE_THETA ** (-jnp.arange(halfm, dtype=F32) / halfm)
    lane = jnp.arange(LANES)
    row0 = inv64[lane % half64]
    row1 = jnp.where((lane >= MLA_NOPE) & (lane < MLA_QK), invm[(lane - MLA_NOPE) % halfm], 0.0)
    inv = jnp.zeros((8, LANES), F32).at[0].set(row0).at[1].set(row1)
    tab = jax.ShapeDtypeStruct((B, S, LANES), F32)
    spec = pl.BlockSpec((1, ts, LANES), lambda b, i: (b, i, 0))
    return pl.pallas_call(
        _rope_tables_kernel,
        grid=(B, S // ts),
        in_specs=[pl.BlockSpec((1, ts, 1), lambda b, i: (b, i, 0)), _const_spec((8, LANES))],
        out_specs=[spec] * 4,
        out_shape=[tab] * 4,
        compiler_params=_cparams(("parallel", "parallel")),
        name="rope_tables",
    )(positions.reshape(B, S, 1), inv)


def _rope64(x, cos, sin_signed, lane):
    first = (lane & (HEAD_DIM - 1)) < (HEAD_DIM // 2)
    partner = jnp.where(first, pltpu.roll(x, LANES - HEAD_DIM // 2, 1), pltpu.roll(x, HEAD_DIM // 2, 1))
    return x * cos + partner * sin_signed


def _rope_mla(x, cos, sin_signed, lane):
    first = lane < MLA_NOPE + MLA_ROPE // 2
    partner = jnp.where(first, pltpu.roll(x, LANES - MLA_ROPE // 2, 1), pltpu.roll(x, MLA_ROPE // 2, 1))
    return x * cos + partner * sin_signed


def _ms64(x, bsel):
    hi, lo = _split_bf16(x * x)
    return (_dot(hi, bsel) + _dot(lo, bsel)) * (1.0 / HEAD_DIM)


def _prep_kernel(x_ref, an_ref, win_ref, c64_ref, s64_ref, cm_ref, sm_ref,
                 qn_ref, wuq_ref, kvn_ref, wukv_ref, qg_ref, kg_ref, g64_ref,
                 qa_ref, ka_ref, va_ref, qm_ref, km_ref, vm_ref, qc_ref, kc_ref, vc_ref,
                 qd_ref, qd32_ref, kd_ref, vd_ref, kmean_ref):
    ts = x_ref.shape[1]
    subs = [slice(r0, r0 + MOBA_BLOCK) for r0 in range(0, ts, MOBA_BLOCK)]
    projs = []
    for rows in subs:
        x = x_ref[0, rows, :]
        ms = jnp.mean(x * x, axis=-1, keepdims=True)
        h = (x * lax.rsqrt(ms + EPS) * an_ref[...]).astype(BF16)
        projs.append(_dot(h, win_ref[...]))

    lane = lax.broadcasted_iota(jnp.int32, (1, LANES), 1)
    lo_half = lane < HEAD_DIM
    r = lax.broadcasted_iota(jnp.int32, (LANES, LANES), 0) // HEAD_DIM
    c = lax.broadcasted_iota(jnp.int32, (LANES, LANES), 1) // HEAD_DIM
    bsel = jnp.where(r == c, 1.0, 0.0).astype(BF16)
    scale64 = HEAD_DIM ** -0.5
    scale_m = MLA_QK ** -0.5 * LOG2E
    for rows, proj in zip(subs, projs):
        _prep_finish(proj, rows, lane, lo_half, bsel, scale64, scale_m, c64_ref, s64_ref, cm_ref, sm_ref,
                     qn_ref, wuq_ref, kvn_ref, wukv_ref, qg_ref, kg_ref, g64_ref,
                     qa_ref, ka_ref, va_ref, qm_ref, km_ref, vm_ref, qc_ref, kc_ref, vc_ref,
                     qd_ref, qd32_ref, kd_ref, vd_ref, kmean_ref)


def _prep_finish(proj, rows, lane, lo_half, bsel, scale64, scale_m, c64_ref, s64_ref, cm_ref, sm_ref,
                 qn_ref, wuq_ref, kvn_ref, wukv_ref, qg_ref, kg_ref, g64_ref,
                 qa_ref, ka_ref, va_ref, qm_ref, km_ref, vm_ref, qc_ref, kc_ref, vc_ref,
                 qd_ref, qd32_ref, kd_ref, vd_ref, kmean_ref):
    blk = rows.start // MOBA_BLOCK
    c64, s64, cm, sm = c64_ref[0, rows, :], s64_ref[0, rows, :], cm_ref[0, rows, :], sm_ref[0, rows, :]

    def seg(off, j):
        return proj[:, off + j * SEG: off + (j + 1) * SEG]

    def store_pair(ref, val, seg_lo, seg_hi):
        ref[0, rows, seg_lo * SEG:(seg_lo + 1) * SEG] = jnp.where(lo_half, val, 0.0).astype(BF16)
        ref[0, rows, seg_hi * SEG:(seg_hi + 1) * SEG] = jnp.where(lo_half, 0.0, val).astype(BF16)

    for p in range(2):
        store_pair(qa_ref, seg(0, p) * scale64, 2 * p, 2 * p + 1)
    ka_ref[0, rows, :] = proj[:, 256:512].astype(BF16)
    va_ref[0, rows, :] = proj[:, 512:768].astype(BF16)

    cq = proj[:, 768:1024]
    cqn = (cq * lax.rsqrt(jnp.sum(cq * cq, axis=-1, keepdims=True) * (1.0 / MLA_Q_RANK) + EPS)
           * qn_ref[...]).astype(BF16)
    qm_raw = _dot(cqn, wuq_ref[...])
    ckv = proj[:, 1024:1152]
    ckvn = (ckv * lax.rsqrt(jnp.mean(ckv * ckv, axis=-1, keepdims=True) + EPS) * kvn_ref[...]).astype(BF16)
    kv = _dot(ckvn, wukv_ref[...])
    kpe = proj[:, 1152:1280]

    def store_transposed(ref, v):
        ref[0, blk] = v.T.astype(BF16)

    store_transposed(vm_ref, kv[:, 512:768])
    for hh in range(GROUP_HEADS):
        sl = slice(hh * SEG, (hh + 1) * SEG)
        qs = qm_raw[:, sl]
        qs = qs * lax.rsqrt(jnp.sum(qs * qs, axis=-1, keepdims=True) * (1.0 / MLA_QK) + EPS) * qg_ref[...]
        qm_ref[0, rows, sl] = (_rope_mla(qs, cm, sm, lane) * scale_m).astype(BF16)
        ks = kv[:, sl] + kpe
        ks = ks * lax.rsqrt(jnp.sum(ks * ks, axis=-1, keepdims=True) * (1.0 / MLA_QK) + EPS) * kg_ref[...]
        km_ref[0, rows, sl] = _rope_mla(ks, cm, sm, lane).astype(BF16)

    def norm_rope64(v, gain_row):
        v = v * lax.rsqrt(_ms64(v, bsel) + EPS) * g64_ref[gain_row:gain_row + 1, :]
        return _rope64(v, c64, s64, lane)

    qa_pair = norm_rope64(seg(1280, 0), 0) * scale64
    qb_pair = norm_rope64(seg(1280, 1), 0) * scale64
    store_pair(qc_ref, qa_pair, 0, 2)
    store_pair(qc_ref, qb_pair, 1, 3)
    kc_ref[0, rows, :] = norm_rope64(seg(1536, 0), 1).astype(BF16)
    vc_ref[0, rows, :] = seg(1664, 0).astype(BF16)

    for p in range(2):
        qd = norm_rope64(seg(1792, p), 2)
        qd32_ref[0, rows, p * SEG:(p + 1) * SEG] = qd
        store_pair(qd_ref, qd * (scale64 * LOG2E), 2 * p, 2 * p + 1)
        kd = norm_rope64(seg(2048, p), 3)
        kd_ref[0, rows, p * SEG:(p + 1) * SEG] = kd.astype(BF16)
        kmean_ref[0, blk, :, p * SEG:(p + 1) * SEG] = jnp.mean(kd, axis=0, keepdims=True)
    store_transposed(vd_ref, proj[:, 2304:2560])


def _prep_weights(w_in, mla_q_norm, mla_w_uq, mla_kv_norm, mla_w_ukv, mla_q_gain, mla_k_gain,
                  swa_q_gain, swa_k_gain, moba_q_gain, moba_k_gain):
    D = w_in.shape[0]
    z = lambda n: jnp.zeros((D, n), F32)
    mla = w_in[:, OFF_MLA:OFF_MLA + MLA_COLS]
    swa = w_in[:, OFF_SWA:OFF_SWA + SWA_COLS]
    swa_q = swa[:, :GROUP_WIDTH].reshape(D, GROUP_HEADS, HEAD_DIM)[:, jnp.array([0, 2, 1, 3])].reshape(D, GROUP_WIDTH)
    win = jnp.concatenate([
        w_in[:, OFF_SB:OFF_SB + SB_COLS],
        mla[:, :MLA_Q_RANK], z(256 - MLA_Q_RANK),
        mla[:, MLA_Q_RANK:MLA_Q_RANK + MLA_KV_RANK],
        z(MLA_NOPE), mla[:, MLA_Q_RANK + MLA_KV_RANK:], z(SEG - MLA_QK),
        swa_q, swa[:, GROUP_WIDTH:],
        w_in[:, OFF_MOBA:OFF_MOBA + MOBA_COLS],
    ], axis=1).astype(BF16)
    assert win.shape[1] == N_IN_PAD
    qn = jnp.pad(mla_q_norm, (0, 256 - MLA_Q_RANK)).reshape(1, 256)
    wuq = mla_w_uq.reshape(MLA_Q_RANK, GROUP_HEADS, MLA_QK)
    wuq = jnp.pad(wuq, ((0, 256 - MLA_Q_RANK), (0, 0), (0, SEG - MLA_QK))).reshape(256, Q_PAD).astype(BF16)
    wukv = mla_w_ukv.reshape(MLA_KV_RANK, GROUP_HEADS, MLA_NOPE + MLA_V)
    wkn = jnp.pad(wukv[:, :, :MLA_NOPE], ((0, 0), (0, 0), (0, SEG - MLA_NOPE))).reshape(MLA_KV_RANK, Q_PAD)
    wv = wukv[:, :, MLA_NOPE:].reshape(MLA_KV_RANK, GROUP_WIDTH)
    wukv_r = jnp.concatenate([wkn, wv], axis=1).astype(BF16)
    qg = jnp.pad(mla_q_gain, (0, SEG - MLA_QK)).reshape(1, SEG)
    kg = jnp.pad(mla_k_gain, (0, SEG - MLA_QK)).reshape(1, SEG)
    g64 = jnp.zeros((8, LANES), F32)
    for row, g in enumerate((swa_q_gain, swa_k_gain, moba_q_gain, moba_k_gain)):
        g64 = g64.at[row].set(jnp.tile(g, LANES // HEAD_DIM))
    return win, qn, wuq, mla_kv_norm.reshape(1, MLA_KV_RANK), wukv_r, qg, kg, g64


def _prep(x, attn_norm, tables, weights):
    B, S, D = x.shape
    ts = PREP_TS
    win, qn, wuq, kvn, wukv, qg, kg, g64 = weights
    tok = lambda w: pl.BlockSpec((1, ts, w), lambda b, i: (b, i, 0))
    nb_t = ts // MOBA_BLOCK
    VT = "channel-major value blocks"
    out_widths = [(Q_PAD, BF16), (256, BF16), (256, BF16),
                  (Q_PAD, BF16), (Q_PAD, BF16), (VT, BF16),
                  (Q_PAD, BF16), (128, BF16), (128, BF16),
                  (Q_PAD, BF16), (256, F32), (256, BF16), (VT, BF16)]
    vt_shape = jax.ShapeDtypeStruct((B, S // MOBA_BLOCK, GROUP_WIDTH, MOBA_BLOCK), BF16)
    vt_spec = pl.BlockSpec((1, nb_t, GROUP_WIDTH, MOBA_BLOCK), lambda b, i: (b, i, 0, 0))
    out_shape = [vt_shape if w is VT else jax.ShapeDtypeStruct((B, S, w), dt) for w, dt in out_widths]
    out_specs = [vt_spec if w is VT else tok(w) for w, _ in out_widths]
    out_shape.append(jax.ShapeDtypeStruct((B, S // MOBA_BLOCK, 1, 256), F32))
    out_specs.append(pl.BlockSpec((1, nb_t, 1, 256), lambda b, i: (b, i, 0, 0)))
    consts = [attn_norm.reshape(1, D), win]
    tail = [qn, wuq, kvn, wukv, qg, kg, g64]
    in_specs = ([tok(D)] + [_const_spec(a.shape) for a in consts] + [tok(LANES)] * 4
                + [_const_spec(a.shape) for a in tail])
    return pl.pallas_call(
        _prep_kernel,
        grid=(B, S // ts),
        in_specs=in_specs,
        out_specs=out_specs,
        out_shape=out_shape,
        compiler_params=_cparams(("parallel", "parallel")),
        name="prep",
    )(x, *consts, *tables, *tail)


def _sb_kernel(q_ref, k_ref, v_ref, o_ref, acc_ref, carry_ref):
    T = SB_T
    i = pl.program_id(1)
    row = lax.broadcasted_iota(jnp.int32, (T, T), 0)
    col = lax.broadcasted_iota(jnp.int32, (T, T), 1)
    upper = jnp.concatenate([jnp.where(row > col, 1.0, 0.0), jnp.ones((T, T), F32)], axis=1).astype(BF16)
    causal = col < row
    lane = lax.broadcasted_iota(jnp.int32, (1, LANES), 1)

    heads = range(GROUP_HEADS)

    def block(j, first):
        ks = pl.ds(pl.multiple_of(j * T, T), T)
        kv = lambda ref, hh: ref[0, ks, (hh // 2) * SEG:(hh // 2 + 1) * SEG]
        z = [_dot_nt(q_ref[0, :, hh * SEG:(hh + 1) * SEG], kv(k_ref, hh)) for hh in heads]
        log_beta = [jnp.minimum(zz, 0.0) - jnp.log(1.0 + jnp.exp(-jnp.abs(zz))) for zz in z]
        log_keep = [lb - zz for lb, zz in zip(log_beta, z)]
        if first:
            log_keep = [jnp.where(causal, lk, 0.0) for lk in log_keep]
        parts = [_split_bf16(lk) for lk in log_keep]
        sums = [_dot(hi, upper) + _dot(lo, upper) for hi, lo in parts]
        if first:
            w = [jnp.where(causal, jnp.exp(lb + sm[:, :T]), 0.0) for lb, sm in zip(log_beta, sums)]
            carry = [sm[:, T:] for sm in sums]
        else:
            w = [jnp.exp(lb + (sm[:, :T] + carry_ref[hh])) for hh, lb, sm in zip(heads, log_beta, sums)]
            carry = [carry_ref[hh] + sm[:, T:] for hh, sm in zip(heads, sums)]
        pv = [_dot(ww.astype(BF16), kv(v_ref, hh)) for hh, ww in zip(heads, w)]
        for hh in heads:
            acc_ref[hh] = pv[hh] if first else acc_ref[hh] + pv[hh]
            carry_ref[hh] = carry[hh]
        return jnp.max(jnp.maximum(jnp.maximum(carry[0], carry[1]), jnp.maximum(carry[2], carry[3])))

    def cond(st):
        j, mx = st
        return jnp.logical_and(j >= 0, mx > SB_EXIT)

    def body(st):
        return st[0] - 1, block(st[0], False)

    lax.while_loop(cond, body, (i - 1, block(i, True)))

    lo_half = lane < HEAD_DIM
    for p in range(2):
        o_ref[0, :, p * SEG:(p + 1) * SEG] = jnp.where(lo_half, acc_ref[2 * p], acc_ref[2 * p + 1]).astype(o_ref.dtype)


def _sb_attn(q, k, v):
    B, S, _ = k.shape
    T = SB_T
    full = pl.BlockSpec((1, S, GROUP_WIDTH), lambda b, i: (b, 0, 0))
    return pl.pallas_call(
        _sb_kernel,
        grid=(B, S // T),
        in_specs=[pl.BlockSpec((1, T, Q_PAD), lambda b, i: (b, i, 0)), full, full],
        out_specs=pl.BlockSpec((1, T, GROUP_WIDTH), lambda b, i: (b, i, 0)),
        out_shape=jax.ShapeDtypeStruct((B, S, GROUP_WIDTH), BF16),
        scratch_shapes=[pltpu.VMEM((GROUP_HEADS, T, SEG), F32), pltpu.VMEM((GROUP_HEADS, T, T), F32)],
        compiler_params=_cparams(("parallel", "arbitrary")),
        name="sb_attn",
    )(q, k, v)


def _softmax_update_all(acc_ref, m_ref, heads, scores, vt_augs):
    m_old = [m_ref[hh][0:1, :] for hh in heads]
    acc_old = [acc_ref[hh] for hh in heads]
    m_new = [jnp.maximum(mo, jnp.max(s, axis=0, keepdims=True)) for mo, s in zip(m_old, scores)]
    p = [jnp.exp2(s - mn).astype(BF16) for s, mn in zip(scores, m_new)]
    pv = [_dot(va, pp) for va, pp in zip(vt_augs, p)]
    for hh, mo, mn, r, a in zip(heads, m_old, m_new, pv, acc_old):
        acc_ref[hh] = jnp.exp2(mo - mn) * a + r
        m_ref[hh] = jnp.broadcast_to(mn, m_ref.shape[1:])


def _values_t(vt_ref, first_blk, nblk, hh):
    rows = slice(hh * HEAD_DIM, (hh + 1) * HEAD_DIM)
    v = jnp.concatenate([vt_ref[0, first_blk + r, rows, :] for r in range(nblk)], axis=1)
    return jnp.concatenate([v, jnp.ones((ONES_ROWS, v.shape[1]), v.dtype)], axis=0)


def _softmax_pipelined(n_steps, qk_pair, vt_ref, acc_ref, m_ref, s_ref, p_ref, a_ref):
    pairs = ((0, 1), (2, 3))
    s_ref[0] = jnp.full(s_ref.shape[1:], -jnp.inf, F32)
    p_ref[0] = jnp.zeros(p_ref.shape[1:], BF16)
    a_ref[0] = jnp.ones(a_ref.shape[1:], F32)

    def softmax(m_old, s):
        m_new = jnp.maximum(m_old, jnp.max(s, axis=0, keepdims=True))
        return jnp.exp2(s - m_new).astype(BF16), jnp.exp2(m_old - m_new), m_new

    def half_step(j, rd, wr, with_scores):
        prev = jnp.maximum(j - 1, 0) * MLA_WIDE
        m_old = [m_ref[hh][0:1, :] for hh in range(GROUP_HEADS)]
        acc_old = [acc_ref[hh] for hh in range(GROUP_HEADS)]
        s1_prev = [s_ref[rd, n] for n in range(2)]
        p0_prev = [p_ref[rd, n] for n in range(2)]
        a0_prev = [a_ref[rd, n][0:1, :] for n in range(2)]
        vt = [_values_t(vt_ref, prev, MLA_WIDE, hh) for hh in range(GROUP_HEADS)]

        acc = [a0_prev[n] * acc_old[hh] + _dot(vt[hh], p0_prev[n]) for n, hh in enumerate(pairs[0])]
        s0 = qk_pair(0, j * MLA_WIDE) if with_scores else None
        r1 = [softmax(m_old[hh], s1_prev[n]) for n, hh in enumerate(pairs[1])]
        s1 = qk_pair(1, j * MLA_WIDE) if with_scores else None
        acc += [r1[n][1] * acc_old[hh] + _dot(vt[hh], r1[n][0]) for n, hh in enumerate(pairs[1])]
        m_new = [None, None, r1[0][2], r1[1][2]]
        if with_scores:
            r0 = [softmax(m_old[hh], s0[n]) for n, hh in enumerate(pairs[0])]
            m_new[0:2] = [r0[0][2], r0[1][2]]

        for hh in range(GROUP_HEADS):
            acc_ref[hh] = acc[hh]
            if m_new[hh] is not None:
                m_ref[hh] = jnp.broadcast_to(m_new[hh], m_ref.shape[1:])
        if with_scores:
            for n in range(2):
                p_ref[wr, n] = r0[n][0]
                a_ref[wr, n] = jnp.broadcast_to(r0[n][1], a_ref.shape[2:])
                s_ref[wr, n] = s1[n]

    def body(u, carry):
        half_step(2 * u, 0, 1, True)
        half_step(2 * u + 1, 1, 0, True)
        return carry

    lax.fori_loop(0, n_steps // 2, body, 0)

    @pl.when(n_steps % 2 == 1)
    def _():
        half_step(n_steps - 1, 0, 1, True)
        half_step(n_steps, 1, None, False)

    @pl.when(n_steps % 2 == 0)
    def _():
        half_step(n_steps, 0, None, False)


def _softmax_finish(acc_ref, o_ref):
    for p in range(2):
        halves = [acc_ref[hh][0:HEAD_DIM, :] / acc_ref[hh][HEAD_DIM:HEAD_DIM + 1, :] for hh in (2 * p, 2 * p + 1)]
        o_ref[0, :, p * SEG:(p + 1) * SEG] = jnp.concatenate(halves, axis=0).T.astype(o_ref.dtype)


def _mla_kernel(q_ref, k_ref, vt_ref, o_ref, acc_ref, m_ref, s_ref, p_ref, a_ref):
    T = MLA_T
    i = pl.program_id(1)
    key = lax.broadcasted_iota(jnp.int32, (T, T), 0)
    qry = lax.broadcasted_iota(jnp.int32, (T, T), 1)
    causal = key <= qry

    def scores(hh, first_blk, nblk):
        ks = pl.ds(pl.multiple_of(first_blk * T, T), nblk * T)
        return _dot_nt(k_ref[0, ks, hh * SEG:(hh + 1) * SEG], q_ref[0, :, hh * SEG:(hh + 1) * SEG])

    def step(first_blk, nblk, masked):
        heads = range(GROUP_HEADS)
        s = [scores(hh, first_blk, nblk) for hh in heads]
        if masked:
            s = [jnp.where(causal, ss, NEG) for ss in s]
        vt = [_values_t(vt_ref, first_blk, nblk, hh) for hh in heads]
        _softmax_update_all(acc_ref, m_ref, heads, s, vt)

    m_ref[...] = jnp.full(m_ref.shape, NEG, F32)
    acc_ref[...] = jnp.zeros(acc_ref.shape, F32)
    _softmax_pipelined(i // MLA_WIDE, lambda pair, blk: [scores(2 * pair + n, blk, MLA_WIDE) for n in range(2)],
                       vt_ref, acc_ref, m_ref, s_ref, p_ref, a_ref)
    for r in range(MLA_WIDE - 1):
        @pl.when(i % MLA_WIDE > r)
        def _(r=r):
            step(i // MLA_WIDE * MLA_WIDE + r, 1, False)
    step(i, 1, True)
    _softmax_finish(acc_ref, o_ref)


def _softmax_scratch(T):
    tk = MLA_WIDE * T
    return [pltpu.VMEM((GROUP_HEADS, HEAD_DIM + ONES_ROWS, T), F32), pltpu.VMEM((GROUP_HEADS, 8, T), F32),
            pltpu.VMEM((2, 2, tk, T), F32), pltpu.VMEM((2, 2, tk, T), BF16), pltpu.VMEM((2, 2, 8, T), F32)]


def _mla_attn(q, k, vt):
    B, S, _ = k.shape
    T = MLA_T
    assert T == MOBA_BLOCK
    return pl.pallas_call(
        _mla_kernel,
        grid=(B, S // T),
        in_specs=[pl.BlockSpec((1, T, Q_PAD), lambda b, i: (b, i, 0)),
                  pl.BlockSpec((1, S, Q_PAD), lambda b, i: (b, 0, 0)),
                  pl.BlockSpec((1,) + vt.shape[1:], lambda b, i: (b, 0, 0, 0))],
        out_specs=pl.BlockSpec((1, T, GROUP_WIDTH), lambda b, i: (b, i, 0)),
        out_shape=jax.ShapeDtypeStruct((B, S, GROUP_WIDTH), BF16),
        scratch_shapes=_softmax_scratch(T),
        compiler_params=_cparams(("parallel", "arbitrary")),
        name="mla_attn",
    )(q, k, vt)


def _swa_kernel(sink_ref, q_ref, k_ref, v_ref, o_ref):
    W = WINDOW
    i = pl.program_id(1)
    lane = lax.broadcasted_iota(jnp.int32, (1, LANES), 1)
    lo_half = lane < HEAD_DIM
    heads = range(GROUP_HEADS)
    sinks = [sink_ref[hh] for hh in heads]
    chains, ks, vs, bands = [], [], [], []
    for b in range(q_ref.shape[1] // W):
        q0 = i * q_ref.shape[1] + b * W
        start = pl.multiple_of(jnp.maximum(q0 - W, 0), W)
        ks.append(k_ref[0, pl.ds(start, 2 * W), :])
        vs.append(v_ref[0, pl.ds(start, 2 * W), :])
        qpos = q0 + lax.broadcasted_iota(jnp.int32, (W, 2 * W), 0)
        kpos = start + lax.broadcasted_iota(jnp.int32, (W, 2 * W), 1)
        bands.append(jnp.logical_and(kpos <= qpos, qpos - kpos < W))
        chains += [(b, hh) for hh in heads]
    s = [jnp.where(bands[b], _dot_nt(q_ref[0, b * W:(b + 1) * W, hh * SEG:(hh + 1) * SEG], ks[b]), NEG)
         for b, hh in chains]
    m = [jnp.maximum(jnp.max(ss, axis=1, keepdims=True), sinks[hh]) for ss, (b, hh) in zip(s, chains)]
    p = [jnp.exp(ss - mm) for ss, mm in zip(s, m)]
    denom = [jnp.sum(pp, axis=1, keepdims=True) + jnp.exp(sinks[hh] - mm) for pp, mm, (b, hh) in zip(p, m, chains)]
    outs = [_dot(pp.astype(BF16), vs[b]) / dd for pp, dd, (b, hh) in zip(p, denom, chains)]
    for b in range(q_ref.shape[1] // W):
        o = outs[b * GROUP_HEADS:(b + 1) * GROUP_HEADS]
        rows = slice(b * W, (b + 1) * W)
        o_ref[0, rows, 0:SEG] = jnp.where(lo_half, o[0], pltpu.roll(o[1], HEAD_DIM, 1)).astype(o_ref.dtype)
        o_ref[0, rows, SEG:2 * SEG] = jnp.where(lo_half, pltpu.roll(o[2], HEAD_DIM, 1), o[3]).astype(o_ref.dtype)


def _swa_attn(q, k, v, sinks):
    B, S, _ = k.shape
    W = SWA_T
    full = pl.BlockSpec((1, S, 2 * HEAD_DIM), lambda b, i: (b, 0, 0))
    return pl.pallas_call(
        _swa_kernel,
        grid=(B, S // W),
        in_specs=[pl.BlockSpec(memory_space=pltpu.SMEM),
                  pl.BlockSpec((1, W, Q_PAD), lambda b, i: (b, i, 0)), full, full],
        out_specs=pl.BlockSpec((1, W, GROUP_WIDTH), lambda b, i: (b, i, 0)),
        out_shape=jax.ShapeDtypeStruct((B, S, GROUP_WIDTH), BF16),
        compiler_params=_cparams(("parallel", "parallel")),
        name="swa_attn",
    )(sinks, q, k, v)


def _moba_kernel(q_ref, q32_ref, k_ref, vt_ref, kmean_ref, o_ref, acc_ref, m_ref, s_ref, p_ref, a_ref, qaug_ref):
    T = MOBA_BLOCK
    own = pl.program_id(1)
    nbr = kmean_ref.shape[1]
    key = lax.broadcasted_iota(jnp.int32, (T, T), 0)
    qry = lax.broadcasted_iota(jnp.int32, (T, T), 1)
    causal = key <= qry
    lo_half = lax.broadcasted_iota(jnp.int32, (1, LANES), 1) < HEAD_DIM
    blk = lax.broadcasted_iota(jnp.int32, (nbr, T), 0).astype(F32)
    kblk = lax.broadcasted_iota(jnp.int32, (T, LANES), 1)
    own_f = own.astype(F32)

    m_ref[...] = jnp.full(m_ref.shape, NEG, F32)
    acc_ref[...] = jnp.zeros(acc_ref.shape, F32)
    own_ks = pl.ds(pl.multiple_of(own * T, T), T)

    heads = range(GROUP_HEADS)
    pair_sl = [slice((hh // 2) * SEG, (hh // 2 + 1) * SEG) for hh in heads]

    s = [jnp.where(causal, _dot_nt(k_ref[0, own_ks, pair_sl[hh]], q_ref[0, :, hh * SEG:(hh + 1) * SEG]), NEG)
         for hh in heads]
    _softmax_update_all(acc_ref, m_ref, heads, s, [_values_t(vt_ref, own, 1, hh) for hh in heads])

    g = []
    for hh in heads:
        head_lanes = lo_half if hh % 2 == 0 else jnp.logical_not(lo_half)
        qh_hi, qh_lo = _split_bf16(jnp.where(head_lanes, q32_ref[0, :, pair_sl[hh]], 0.0))
        km_hi, km_lo = _split_bf16(kmean_ref[0, :, pair_sl[hh]])
        gate = _dot_nt(km_hi, qh_hi) + (_dot_nt(km_lo, qh_hi) + _dot_nt(km_hi, qh_lo))
        g.append(jnp.where(blk < own_f, gate, NEG))
    sel = [jnp.zeros((nbr, T), F32) for _ in heads]
    for t in range(MOBA_TOPK):
        mx = [jnp.max(gg, axis=0, keepdims=True) for gg in g]
        first = [jnp.min(jnp.where(gg == m, blk, float(nbr)), axis=0, keepdims=True)
                 for gg, m in zip(g, mx)]
        hit = [blk == f for f in first]
        sel = [jnp.where(jnp.logical_and(h, t < own), 1.0, sl) for h, sl in zip(hit, sel)]
        g = [jnp.where(h, -jnp.inf, gg) for h, gg in zip(hit, g)]
    for hh in heads:
        neg_t = jnp.concatenate([(1.0 - sel[hh]) * NEG, jnp.zeros((LANES - nbr, T), F32)], axis=0)
        qaug_ref[hh, :, 0:SEG] = q_ref[0, :, hh * SEG:(hh + 1) * SEG]
        qaug_ref[hh, :, SEG:2 * SEG] = neg_t.T.astype(BF16)

    def scores_pair(pair, n0, nblk):
        ks = pl.ds(pl.multiple_of(n0 * T, T), nblk * T)
        onehot = jnp.concatenate([jnp.where(kblk == n0 + r, 1.0, 0.0) for r in range(nblk)], axis=0).astype(BF16)
        k_aug = jnp.concatenate([k_ref[0, ks, pair * SEG:(pair + 1) * SEG], onehot], axis=1)
        return [_dot_nt(k_aug, qaug_ref[2 * pair + n]) for n in range(2)]

    def step(n0, nblk):
        s = scores_pair(0, n0, nblk) + scores_pair(1, n0, nblk)
        _softmax_update_all(acc_ref, m_ref, heads, s, [_values_t(vt_ref, n0, nblk, hh) for hh in heads])

    _softmax_pipelined(own // MLA_WIDE, lambda pair, blk: scores_pair(pair, blk, MLA_WIDE),
                       vt_ref, acc_ref, m_ref, s_ref, p_ref, a_ref)
    for r in range(MLA_WIDE - 1):
        @pl.when(own % MLA_WIDE > r)
        def _(r=r):
            step(own // MLA_WIDE * MLA_WIDE + r, 1)
    _softmax_finish(acc_ref, o_ref)


def _moba_attn(q, q32, k, vt, kmean):
    B, S, _ = k.shape
    T = MOBA_BLOCK
    assert S // T <= LANES
    nbr = -(-(S // T) // 16) * 16
    kmean = jnp.pad(kmean.reshape(B, S // T, GROUP_WIDTH), ((0, 0), (0, nbr - S // T), (0, 0)))
    return pl.pallas_call(
        _moba_kernel,
        grid=(B, S // T),
        in_specs=[pl.BlockSpec((1, T, Q_PAD), lambda b, i: (b, i, 0)),
                  pl.BlockSpec((1, T, GROUP_WIDTH), lambda b, i: (b, i, 0)),
                  pl.BlockSpec((1, S, GROUP_WIDTH), lambda b, i: (b, 0, 0)),
                  pl.BlockSpec((1,) + vt.shape[1:], lambda b, i: (b, 0, 0, 0)),
                  pl.BlockSpec((1, nbr, GROUP_WIDTH), lambda b, i: (b, 0, 0))],
        out_specs=pl.BlockSpec((1, T, GROUP_WIDTH), lambda b, i: (b, i, 0)),
        out_shape=jax.ShapeDtypeStruct((B, S, GROUP_WIDTH), BF16),
        scratch_shapes=_softmax_scratch(T) + [pltpu.VMEM((GROUP_HEADS, T, 2 * SEG), BF16)],
        compiler_params=_cparams(("parallel", "arbitrary")),
        name="moba_attn",
    )(q, q32, k, vt, kmean)


def _oproj_kernel(oa_ref, ob_ref, oc_ref, od_ref, gn_ref, wo_ref, x_ref, y_ref):
    acc = x_ref[0]
    for g, ref in enumerate((oa_ref, ob_ref, oc_ref, od_ref)):
        o = ref[0].astype(F32)
        ms = jnp.mean(o * o, axis=-1, keepdims=True)
        n = (o * lax.rsqrt(ms + EPS) * gn_ref[:, g * GROUP_WIDTH:(g + 1) * GROUP_WIDTH]).astype(BF16)
        acc = acc + _dot(n, wo_ref[g * GROUP_WIDTH:(g + 1) * GROUP_WIDTH, :])
    y_ref[0] = acc


def _oproj(groups, group_norm, w_o, x):
    B, S, D = x.shape
    ts = OUT_TS
    tok = lambda w: pl.BlockSpec((1, ts, w), lambda b, i: (b, i, 0))
    return pl.pallas_call(
        _oproj_kernel,
        grid=(B, S // ts),
        in_specs=[tok(GROUP_WIDTH)] * 4 + [_const_spec((1, D)), _const_spec(w_o.shape), tok(D)],
        out_specs=tok(D),
        out_shape=jax.ShapeDtypeStruct((B, S, D), F32),
        compiler_params=_cparams(("parallel", "parallel")),
        name="oproj",
    )(*groups, group_norm.reshape(1, D), w_o, x)


def _ffn_kernel(x_ref, p_ref, fn_ref, wup_ref, cw_ref, cb_ref, wdn_ref, pproj_ref, pgate_ref, y_ref,
                halo_ref, ubuf0_ref, ubuf1_ref, acc_ref, h_ref):
    ts = x_ref.shape[1]
    fc2 = wup_ref.shape[2]
    fc = fc2 // 2
    n_chunks = wup_ref.shape[0]
    i = pl.program_id(1)

    @pl.when(i == 0)
    def _():
        halo_ref[...] = jnp.zeros(halo_ref.shape, F32)

    x = x_ref[0]
    ms = jnp.mean(x * x, axis=-1, keepdims=True)
    h_ref[...] = (x * lax.rsqrt(ms + EPS) * fn_ref[...]).astype(BF16)
    acc_ref[...] = x

    def up(c, ubuf_ref):
        u = _dot(h_ref[...], wup_ref[c])
        ubuf_ref[0:HALO, :] = halo_ref[c]
        ubuf_ref[HALO:HALO + ts, :] = u
        halo_ref[c] = u[ts - HALO:ts, :]

    def down(c, ubuf_ref):
        cw = cw_ref[c]
        y = (cw[0:1, :] * ubuf_ref[HALO - 2:HALO - 2 + ts, :] + cw[1:2, :] * ubuf_ref[HALO - 1:HALO - 1 + ts, :]
             + cw[2:3, :] * ubuf_ref[HALO:HALO + ts, :] + cb_ref[c])
        ya = y[:, :fc]
        g = ya * (1.0 / (1.0 + jnp.exp(-ya))) * y[:, fc:]
        acc_ref[...] += _dot(g.astype(BF16), wdn_ref[c])

    up(0, ubuf0_ref)

    def pair(cc, carry):
        up(2 * cc + 1, ubuf1_ref)
        down(2 * cc, ubuf0_ref)
        up(2 * cc + 2, ubuf0_ref)
        down(2 * cc + 1, ubuf1_ref)
        return carry

    assert n_chunks % 2 == 1
    lax.fori_loop(0, n_chunks // 2, pair, 0)
    down(n_chunks - 1, ubuf0_ref)
    x2 = acc_ref[...]
    gate = _dot(x2.astype(BF16), pgate_ref[...])
    emb = _dot(p_ref[0, 0].astype(BF16), pproj_ref[...])
    y_ref[0] = x2 + emb * (1.0 / (1.0 + jnp.exp(-gate)))


def _ffn_weights(ffn_norm, w_up, conv_w, conv_b, w_down, ple_proj, ple_gate):
    fc = FFN_FC
    nc = D_FF // fc
    D = w_up.shape[0]
    pair = lambda a: jnp.concatenate([a[..., :D_FF].reshape(a.shape[:-1] + (nc, fc)),
                                      a[..., D_FF:].reshape(a.shape[:-1] + (nc, fc))], axis=-1)
    wup = jnp.moveaxis(pair(w_up), 1, 0).astype(BF16)
    cw = jnp.pad(jnp.moveaxis(pair(conv_w), 1, 0), ((0, 0), (0, 8 - CONV_WIDTH), (0, 0)))
    cb = jnp.moveaxis(pair(conv_b.reshape(1, -1)), 1, 0)
    wdn = w_down.reshape(nc, fc, D).astype(BF16)
    return ffn_norm.reshape(1, D), wup, cw, cb, wdn, ple_proj.astype(BF16), ple_gate.astype(BF16)


def _ffn(x, p, layer, weights):
    B, S, D = x.shape
    ts = FFN_TS
    fn, wup, cw, cb, wdn, pproj, pgate = weights
    nc, _, fc2 = wup.shape
    tok = lambda w: pl.BlockSpec((1, ts, w), lambda b, i: (b, i, 0))
    once = lambda a: pl.BlockSpec(a.shape, lambda *_: (0,) * a.ndim, pipeline_mode=pl.Buffered(1))
    return pl.pallas_call(
        _ffn_kernel,
        grid=(B, S // ts),
        in_specs=[tok(D), pl.BlockSpec((1, 1, ts, PLE_DIM), lambda b, i: (layer, b, i, 0))] + [once(a) for a in (fn, wup, cw, cb, wdn, pproj, pgate)],
        out_specs=tok(D),
        out_shape=jax.ShapeDtypeStruct((B, S, D), F32),
        scratch_shapes=[pltpu.VMEM((nc, HALO, fc2), F32), pltpu.VMEM((ts + HALO, fc2), F32),
                        pltpu.VMEM((ts + HALO, fc2), F32), pltpu.VMEM((ts, D), F32), pltpu.VMEM((ts, D), BF16)],
        compiler_params=_cparams(("arbitrary", "arbitrary")),
        name="ffn_ple",
    )(x, p, fn, wup, cw, cb, wdn, pproj, pgate)


def kernel(x, p, positions, attn_norm, w_in, mla_q_norm, mla_w_uq, mla_kv_norm, mla_w_ukv, mla_q_gain, mla_k_gain, swa_q_gain, swa_k_gain, swa_sinks, moba_q_gain, moba_k_gain, group_norm, w_o, ffn_norm, w_up, conv_w, conv_b, w_down, ple_proj, ple_gate):
    B, S, D = x.shape
    depth = w_in.shape[0]
    assert D == D_MODEL and S % 512 == 0
    tables = _rope_tables(positions)
    for i in range(depth):
        pw = _prep_weights(w_in[i], mla_q_norm[i], mla_w_uq[i], mla_kv_norm[i], mla_w_ukv[i], mla_q_gain[i],
                           mla_k_gain[i], swa_q_gain[i], swa_k_gain[i], moba_q_gain[i], moba_k_gain[i])
        (qa, ka, va, qm, km, vm, qc, kc, vc, qd, qd32, kd, vd, kmean) = _prep(x, attn_norm[i], tables, pw)
        o_a = _sb_attn(qa, ka, va)
        o_b = _mla_attn(qm, km, vm)
        o_c = _swa_attn(qc, kc, vc, swa_sinks[i])
        o_d = _moba_attn(qd, qd32, kd, vd, kmean)
        x = _oproj((o_a, o_b, o_c, o_d), group_norm[i], w_o[i].astype(BF16), x)
        fw = _ffn_weights(ffn_norm[i], w_up[i], conv_w[i], conv_b[i], w_down[i], ple_proj[i], ple_gate[i])
        x = _ffn(x, p, i, fw)
    return x
```

```python
import functools

import jax
import jax.numpy as jnp
from jax import lax
from jax.experimental import pallas as pl
from jax.experimental.pallas import tpu as pltpu

F32 = jnp.float32
BF16 = jnp.bfloat16

D_MODEL = 1024
HEAD_DIM = 64
GROUP_HEADS = 4
GROUP_WIDTH = GROUP_HEADS * HEAD_DIM
N_GROUPS = 4
ROPE_THETA = 10000.0
EPS = 1e-6
NEG = -1e30
LOG2E = 1.4426950408889634

MLA_Q_RANK = 192
MLA_KV_RANK = 128
MLA_NOPE = 64
MLA_ROPE = 32
MLA_V = 64
MLA_QK = MLA_NOPE + MLA_ROPE
SWA_KV_HEADS = 2
WINDOW = 128
MOBA_BLOCK = 256
MOBA_TOPK = 3
D_FF = 2816
CONV_WIDTH = 3
PLE_DIM = 256

SB_COLS = 3 * GROUP_WIDTH
MLA_COLS = MLA_Q_RANK + MLA_KV_RANK + MLA_ROPE
SWA_COLS = GROUP_WIDTH + 2 * SWA_KV_HEADS * HEAD_DIM
MOBA_COLS = 3 * GROUP_WIDTH
OFF_SB = 0
OFF_MLA = OFF_SB + SB_COLS
OFF_SWA = OFF_MLA + MLA_COLS
OFF_MOBA = OFF_SWA + SWA_COLS

LANES = 128
SEG = LANES
Q_PAD = GROUP_HEADS * SEG
N_IN_PAD = 2560
VMEM_LIMIT = 56 * 1024 * 1024

PREP_TS = 512
OUT_TS = 512
FFN_TS = 512
FFN_FC = 256
SWA_T = 512
SB_T = 128
SB_SUB = 2
SB_EXIT = -45.0
MLA_T = 256
MLA_WIDE = 2
HALO = 8
ONES_ROWS = 16


def _dot(a, b):
    return jnp.dot(a, b, preferred_element_type=F32)


def _dot_nt(a, b):
    return lax.dot_general(a, b, (((1,), (1,)), ((), ())), preferred_element_type=F32)


def _split_bf16(x):
    hi = x.astype(BF16)
    lo = (x - hi.astype(F32)).astype(BF16)
    return hi, lo


def _cparams(sem):
    return pltpu.CompilerParams(dimension_semantics=sem, vmem_limit_bytes=VMEM_LIMIT)


def _const_spec(shape):
    nd = len(shape)
    return pl.BlockSpec(shape, lambda *_: (0,) * nd)


def _rope_tables_kernel(pos_ref, inv_ref, c64_ref, s64_ref, cm_ref, sm_ref):
    pos = pos_ref[0].astype(F32)
    lane = lax.broadcasted_iota(jnp.int32, (1, LANES), 1)
    h64, hm = HEAD_DIM // 2, MLA_ROPE // 2
    ang = pos * inv_ref[0:1, :]
    c, s = jnp.cos(ang), jnp.sin(ang)
    c0, s0 = jnp.where(lane < h64, c, 0.0), jnp.where(lane < h64, s, 0.0)
    spread = lambda v: v + pltpu.roll(v, h64, 1) + pltpu.roll(v, 2 * h64, 1) + pltpu.roll(v, 3 * h64, 1)
    first = (lane & (HEAD_DIM - 1)) < h64
    c64_ref[0] = spread(c0)
    s64_ref[0] = jnp.where(first, -spread(s0), spread(s0))
    in_m = (lane >= h64) & (lane < h64 + hm)
    ca, sa = jnp.where(in_m, c, 0.0), jnp.where(in_m, s, 0.0)
    lo_m = (lane >= MLA_NOPE) & (lane < MLA_NOPE + hm)
    hi_m = (lane >= MLA_NOPE + hm) & (lane < MLA_QK)
    to_lo = lambda v: pltpu.roll(v, MLA_NOPE - h64, 1)
    to_hi = lambda v: pltpu.roll(v, MLA_NOPE - h64 + hm, 1)
    cm_ref[0] = jnp.where(lo_m, to_lo(ca), jnp.where(hi_m, to_hi(ca), 1.0))
    sm_ref[0] = jnp.where(lo_m, -to_lo(sa), jnp.where(hi_m, to_hi(sa), 0.0))


def _rope_tables(positions):
    B, S = positions.shape
    ts = 512
    half64 = HEAD_DIM // 2
    halfm = MLA_ROPE // 2
    inv64 = ROPE_THETA ** (-jnp.arange(half64, dtype=F32) / half64)
    invm = ROPE_THETA ** (-jnp.arange(halfm, dtype=F32) / halfm)
    row0 = jnp.concatenate([inv64, invm, jnp.zeros((LANES - half64 - halfm,), F32)])
    inv = jnp.zeros((8, LANES), F32).at[0].set(row0)
    tab = jax.ShapeDtypeStruct((B, S, LANES), F32)
    spec = pl.BlockSpec((1, ts, LANES), lambda b, i: (b, i, 0))
    return pl.pallas_call(
        _rope_tables_kernel,
        grid=(B, S // ts),
        in_specs=[pl.BlockSpec((1, ts, 1), lambda b, i: (b, i, 0)), _const_spec((8, LANES))],
        out_specs=[spec] * 4,
        out_shape=[tab] * 4,
        compiler_params=_cparams(("parallel", "parallel")),
        name="rope_tables",
    )(positions.reshape(B, S, 1), inv)


def _rope64(x, cos, sin_signed, lane):
    first = (lane & (HEAD_DIM - 1)) < (HEAD_DIM // 2)
    partner = jnp.where(first, pltpu.roll(x, LANES - HEAD_DIM // 2, 1), pltpu.roll(x, HEAD_DIM // 2, 1))
    return x * cos + partner * sin_signed


def _rope_mla(x, cos, sin_signed, lane):
    first = lane < MLA_NOPE + MLA_ROPE // 2
    partner = jnp.where(first, pltpu.roll(x, LANES - MLA_ROPE // 2, 1), pltpu.roll(x, MLA_ROPE // 2, 1))
    return x * cos + partner * sin_signed


def _ms64(x, bsel):
    return _dot(jnp.concatenate(_split_bf16(x * x), axis=1), bsel) * (1.0 / HEAD_DIM)


def _prep_kernel(x_ref, an_ref, win_ref, c64_ref, s64_ref, cm_ref, sm_ref,
                 qn_ref, wuq_ref, kvn_ref, wukv_ref, qg_ref, kg_ref, g64_ref,
                 qa_ref, ka_ref, va_ref, qm_ref, km_ref, vm_ref, qc_ref, kc_ref, vc_ref,
                 qd_ref, qd32_ref, kd_ref, vd_ref, kmean_ref):
    ts = x_ref.shape[1]
    subs = [slice(r0, r0 + MOBA_BLOCK) for r0 in range(0, ts, MOBA_BLOCK)]
    projs = []
    for rows in subs:
        x = x_ref[0, rows, :]
        ms = jnp.mean(x * x, axis=-1, keepdims=True)
        h = (x * lax.rsqrt(ms + EPS) * an_ref[...]).astype(BF16)
        projs.append(_dot(h, win_ref[...]))

    lane = lax.broadcasted_iota(jnp.int32, (1, LANES), 1)
    lo_half = lane < HEAD_DIM
    r = (lax.broadcasted_iota(jnp.int32, (2 * LANES, LANES), 0) & (LANES - 1)) // HEAD_DIM
    c = lax.broadcasted_iota(jnp.int32, (2 * LANES, LANES), 1) // HEAD_DIM
    bsel = jnp.where(r == c, 1.0, 0.0).astype(BF16)
    scale64 = HEAD_DIM ** -0.5
    scale_m = MLA_QK ** -0.5 * LOG2E
    for rows, proj in zip(subs, projs):
        _prep_finish(proj, rows, lane, lo_half, bsel, scale64, scale_m, c64_ref, s64_ref, cm_ref, sm_ref,
                     qn_ref, wuq_ref, kvn_ref, wukv_ref, qg_ref, kg_ref, g64_ref,
                     qa_ref, ka_ref, va_ref, qm_ref, km_ref, vm_ref, qc_ref, kc_ref, vc_ref,
                     qd_ref, qd32_ref, kd_ref, vd_ref, kmean_ref)


def _prep_finish(proj, rows, lane, lo_half, bsel, scale64, scale_m, c64_ref, s64_ref, cm_ref, sm_ref,
                 qn_ref, wuq_ref, kvn_ref, wukv_ref, qg_ref, kg_ref, g64_ref,
                 qa_ref, ka_ref, va_ref, qm_ref, km_ref, vm_ref, qc_ref, kc_ref, vc_ref,
                 qd_ref, qd32_ref, kd_ref, vd_ref, kmean_ref):
    blk = rows.start // MOBA_BLOCK
    c64, s64, cm, sm = c64_ref[0, rows, :], s64_ref[0, rows, :], cm_ref[0, rows, :], sm_ref[0, rows, :]

    def seg(off, j):
        return proj[:, off + j * SEG: off + (j + 1) * SEG]

    def store_pair(ref, val, seg_lo, seg_hi):
        ref[0, rows, seg_lo * SEG:(seg_lo + 1) * SEG] = jnp.where(lo_half, val, 0.0).astype(BF16)
        ref[0, rows, seg_hi * SEG:(seg_hi + 1) * SEG] = jnp.where(lo_half, 0.0, val).astype(BF16)

    for p in range(2):
        store_pair(qa_ref, seg(0, p) * scale64, 2 * p, 2 * p + 1)
    ka_ref[0, rows, :] = proj[:, 256:512].astype(BF16)
    va_ref[0, rows, :] = proj[:, 512:768].astype(BF16)

    cq = proj[:, 768:1024]
    cqn = (cq * lax.rsqrt(jnp.sum(cq * cq, axis=-1, keepdims=True) * (1.0 / MLA_Q_RANK) + EPS)
           * qn_ref[...]).astype(BF16)
    qm_raw = _dot(cqn, wuq_ref[...])
    ckv = proj[:, 1024:1152]
    ckvn = (ckv * lax.rsqrt(jnp.mean(ckv * ckv, axis=-1, keepdims=True) + EPS) * kvn_ref[...]).astype(BF16)
    kv = _dot(ckvn, wukv_ref[...])
    kpe = proj[:, 1152:1280]

    def store_transposed(ref, v):
        ref[0, blk] = v.T.astype(BF16)

    store_transposed(vm_ref, kv[:, 512:768])
    for hh in range(GROUP_HEADS):
        sl = slice(hh * SEG, (hh + 1) * SEG)
        qs = qm_raw[:, sl]
        qs = qs * lax.rsqrt(jnp.sum(qs * qs, axis=-1, keepdims=True) * (1.0 / MLA_QK) + EPS) * qg_ref[...]
        qm_ref[0, rows, sl] = (_rope_mla(qs, cm, sm, lane) * scale_m).astype(BF16)
        ks = kv[:, sl] + kpe
        ks = ks * lax.rsqrt(jnp.sum(ks * ks, axis=-1, keepdims=True) * (1.0 / MLA_QK) + EPS) * kg_ref[...]
        km_ref[0, rows, sl] = _rope_mla(ks, cm, sm, lane).astype(BF16)

    def norm_rope64(v, gain_row):
        v = v * lax.rsqrt(_ms64(v, bsel) + EPS) * g64_ref[gain_row:gain_row + 1, :]
        return _rope64(v, c64, s64, lane)

    qa_pair = norm_rope64(seg(1280, 0), 0) * scale64
    qb_pair = norm_rope64(seg(1280, 1), 0) * scale64
    store_pair(qc_ref, qa_pair, 0, 2)
    store_pair(qc_ref, qb_pair, 1, 3)
    kc_ref[0, rows, :] = norm_rope64(seg(1536, 0), 1).astype(BF16)
    vc_ref[0, rows, :] = seg(1664, 0).astype(BF16)

    for p in range(2):
        qd = norm_rope64(seg(1792, p), 2)
        qd32_ref[0, rows, p * SEG:(p + 1) * SEG] = qd
        store_pair(qd_ref, qd * (scale64 * LOG2E), 2 * p, 2 * p + 1)
        kd = norm_rope64(seg(2048, p), 3)
        kd_ref[0, rows, p * SEG:(p + 1) * SEG] = kd.astype(BF16)
        kmean_ref[0, blk, :, p * SEG:(p + 1) * SEG] = jnp.mean(kd, axis=0, keepdims=True)
    store_transposed(vd_ref, proj[:, 2304:2560])


def _prep_weights(w_in, mla_q_norm, mla_w_uq, mla_kv_norm, mla_w_ukv, mla_q_gain, mla_k_gain,
                  swa_q_gain, swa_k_gain, moba_q_gain, moba_k_gain):
    D = w_in.shape[0]
    z = lambda n: jnp.zeros((D, n), F32)
    mla = w_in[:, OFF_MLA:OFF_MLA + MLA_COLS]
    swa = w_in[:, OFF_SWA:OFF_SWA + SWA_COLS]
    swa_q = swa[:, :GROUP_WIDTH].reshape(D, GROUP_HEADS, HEAD_DIM)[:, jnp.array([0, 2, 1, 3])].reshape(D, GROUP_WIDTH)
    win = jnp.concatenate([
        w_in[:, OFF_SB:OFF_SB + SB_COLS],
        mla[:, :MLA_Q_RANK], z(256 - MLA_Q_RANK),
        mla[:, MLA_Q_RANK:MLA_Q_RANK + MLA_KV_RANK],
        z(MLA_NOPE), mla[:, MLA_Q_RANK + MLA_KV_RANK:], z(SEG - MLA_QK),
        swa_q, swa[:, GROUP_WIDTH:],
        w_in[:, OFF_MOBA:OFF_MOBA + MOBA_COLS],
    ], axis=1).astype(BF16)
    assert win.shape[1] == N_IN_PAD
    qn = jnp.pad(mla_q_norm, (0, 256 - MLA_Q_RANK)).reshape(1, 256)
    wuq = mla_w_uq.reshape(MLA_Q_RANK, GROUP_HEADS, MLA_QK)
    wuq = jnp.pad(wuq, ((0, 256 - MLA_Q_RANK), (0, 0), (0, SEG - MLA_QK))).reshape(256, Q_PAD).astype(BF16)
    wukv = mla_w_ukv.reshape(MLA_KV_RANK, GROUP_HEADS, MLA_NOPE + MLA_V)
    wkn = jnp.pad(wukv[:, :, :MLA_NOPE], ((0, 0), (0, 0), (0, SEG - MLA_NOPE))).reshape(MLA_KV_RANK, Q_PAD)
    wv = wukv[:, :, MLA_NOPE:].reshape(MLA_KV_RANK, GROUP_WIDTH)
    wukv_r = jnp.concatenate([wkn, wv], axis=1).astype(BF16)
    qg = jnp.pad(mla_q_gain, (0, SEG - MLA_QK)).reshape(1, SEG)
    kg = jnp.pad(mla_k_gain, (0, SEG - MLA_QK)).reshape(1, SEG)
    g64 = jnp.zeros((8, LANES), F32)
    for row, g in enumerate((swa_q_gain, swa_k_gain, moba_q_gain, moba_k_gain)):
        g64 = g64.at[row].set(jnp.tile(g, LANES // HEAD_DIM))
    return win, qn, wuq, mla_kv_norm.reshape(1, MLA_KV_RANK), wukv_r, qg, kg, g64


def _prep(x, attn_norm, tables, weights):
    B, S, D = x.shape
    ts = PREP_TS
    win, qn, wuq, kvn, wukv, qg, kg, g64 = weights
    tok = lambda w: pl.BlockSpec((1, ts, w), lambda b, i: (b, i, 0))
    nb_t = ts // MOBA_BLOCK
    VT = "channel-major value blocks"
    out_widths = [(Q_PAD, BF16), (256, BF16), (256, BF16),
                  (Q_PAD, BF16), (Q_PAD, BF16), (VT, BF16),
                  (Q_PAD, BF16), (128, BF16), (128, BF16),
                  (Q_PAD, BF16), (256, F32), (256, BF16), (VT, BF16)]
    vt_shape = jax.ShapeDtypeStruct((B, S // MOBA_BLOCK, GROUP_WIDTH, MOBA_BLOCK), BF16)
    vt_spec = pl.BlockSpec((1, nb_t, GROUP_WIDTH, MOBA_BLOCK), lambda b, i: (b, i, 0, 0))
    out_shape = [vt_shape if w is VT else jax.ShapeDtypeStruct((B, S, w), dt) for w, dt in out_widths]
    out_specs = [vt_spec if w is VT else tok(w) for w, _ in out_widths]
    out_shape.append(jax.ShapeDtypeStruct((B, S // MOBA_BLOCK, 1, 256), F32))
    out_specs.append(pl.BlockSpec((1, nb_t, 1, 256), lambda b, i: (b, i, 0, 0)))
    consts = [attn_norm.reshape(1, D), win]
    tail = [qn, wuq, kvn, wukv, qg, kg, g64]
    in_specs = ([tok(D)] + [_const_spec(a.shape) for a in consts] + [tok(LANES)] * 4
                + [_const_spec(a.shape) for a in tail])
    return pl.pallas_call(
        _prep_kernel,
        grid=(B, S // ts),
        in_specs=in_specs,
        out_specs=out_specs,
        out_shape=out_shape,
        compiler_params=_cparams(("parallel", "parallel")),
        name="prep",
    )(x, *consts, *tables, *tail)


def _sb_kernel(q_ref, k_ref, v_ref, o_ref, acc_ref, carry_ref):
    T = SB_T
    i = pl.program_id(1)
    row = lax.broadcasted_iota(jnp.int32, (T, T), 0)
    col = lax.broadcasted_iota(jnp.int32, (T, T), 1)
    upper = jnp.concatenate([jnp.where(row > col, 1.0, 0.0), jnp.ones((T, T), F32)], axis=1).astype(BF16)
    upper = jnp.concatenate([upper, upper], axis=0)
    causal = col < row
    lane = lax.broadcasted_iota(jnp.int32, (1, LANES), 1)

    n_sub = q_ref.shape[1] // T
    chains = [(sub, hh) for sub in range(n_sub) for hh in range(GROUP_HEADS)]

    def block(t, first):
        kb = [i * n_sub + sub - t for sub in range(n_sub)]
        ks = [pl.ds(pl.multiple_of(jnp.maximum(b, 0) * T, T), T) for b in kb]
        kv = lambda ref, sub, hh: ref[0, ks[sub], (hh // 2) * SEG:(hh // 2 + 1) * SEG]
        z = [_dot_nt(q_ref[0, sub * T:(sub + 1) * T, hh * SEG:(hh + 1) * SEG], kv(k_ref, sub, hh))
             for sub, hh in chains]
        log_beta = [jnp.minimum(zz, 0.0) - jnp.log(1.0 + jnp.exp(-jnp.abs(zz))) for zz in z]
        log_keep = [lb - zz for lb, zz in zip(log_beta, z)]
        if first:
            log_keep = [jnp.where(causal, lk, 0.0) for lk in log_keep]
        parts = [jnp.concatenate(_split_bf16(lk), axis=1) for lk in log_keep]
        sums = [_dot(hl, upper) for hl in parts]
        if first:
            w = [jnp.where(causal, jnp.exp(lb + sm[:, :T]), 0.0) for lb, sm in zip(log_beta, sums)]
            carry = [sm[:, T:] for sm in sums]
        else:
            live = [kb[sub] >= 0 for sub, hh in chains]
            old = [carry_ref[c] for c in range(len(chains))]
            w = [jnp.where(lv, jnp.exp(lb + (sm[:, :T] + cr)), 0.0) for lv, lb, sm, cr in zip(live, log_beta, sums, old)]
            carry = [jnp.where(lv, cr + sm[:, T:], NEG) for lv, sm, cr in zip(live, sums, old)]
        pv = [_dot(ww.astype(BF16), kv(v_ref, sub, hh)) for (sub, hh), ww in zip(chains, w)]
        for c in range(len(chains)):
            acc_ref[c] = pv[c] if first else acc_ref[c] + pv[c]
            carry_ref[c] = carry[c]
        top = carry[0]
        for cr in carry[1:]:
            top = jnp.maximum(top, cr)
        return jnp.max(top)

    def cond(st):
        t, mx = st
        return jnp.logical_and(t <= i * n_sub + n_sub - 1, mx > SB_EXIT)

    def body(st):
        return st[0] + 1, block(st[0], False)

    lax.while_loop(cond, body, (1, block(0, True)))

    lo_half = lane < HEAD_DIM
    for sub in range(n_sub):
        for p in range(2):
            c = sub * GROUP_HEADS + 2 * p
            o_ref[0, sub * T:(sub + 1) * T, p * SEG:(p + 1) * SEG] = jnp.where(
                lo_half, acc_ref[c], acc_ref[c + 1]).astype(o_ref.dtype)


def _sb_attn(q, k, v):
    B, S, _ = k.shape
    T = SB_T
    rows = SB_SUB * T
    n_chains = SB_SUB * GROUP_HEADS
    full = pl.BlockSpec((1, S, GROUP_WIDTH), lambda b, i: (b, 0, 0))
    return pl.pallas_call(
        _sb_kernel,
        grid=(B, S // rows),
        in_specs=[pl.BlockSpec((1, rows, Q_PAD), lambda b, i: (b, i, 0)), full, full],
        out_specs=pl.BlockSpec((1, rows, GROUP_WIDTH), lambda b, i: (b, i, 0)),
        out_shape=jax.ShapeDtypeStruct((B, S, GROUP_WIDTH), BF16),
        scratch_shapes=[pltpu.VMEM((n_chains, T, SEG), F32), pltpu.VMEM((n_chains, T, T), F32)],
        compiler_params=_cparams(("parallel", "arbitrary")),
        name="sb_attn",
    )(q, k, v)


def _softmax_update_all(acc_ref, m_ref, heads, scores, vt_augs):
    m_old = [m_ref[hh][0:1, :] for hh in heads]
    acc_old = [acc_ref[hh] for hh in heads]
    m_new = [jnp.maximum(mo, jnp.max(s, axis=0, keepdims=True)) for mo, s in zip(m_old, scores)]
    p = [jnp.exp2(s - mn).astype(BF16) for s, mn in zip(scores, m_new)]
    pv = [_dot(va, pp) for va, pp in zip(vt_augs, p)]
    for hh, mo, mn, r, a in zip(heads, m_old, m_new, pv, acc_old):
        acc_ref[hh] = jnp.exp2(mo - mn) * a + r
        m_ref[hh] = jnp.broadcast_to(mn, m_ref.shape[1:])


def _values_t(vt_ref, first_blk, nblk, hh):
    rows = slice(hh * HEAD_DIM, (hh + 1) * HEAD_DIM)
    v = jnp.concatenate([vt_ref[0, first_blk + r, rows, :] for r in range(nblk)], axis=1)
    return jnp.concatenate([v, jnp.ones((ONES_ROWS, v.shape[1]), v.dtype)], axis=0)


def _softmax_pipelined(n_steps, qk_pair, vt_ref, acc_ref, m_ref, s_ref, p_ref, a_ref):
    pairs = ((0, 1), (2, 3))
    s_ref[0] = jnp.full(s_ref.shape[1:], -jnp.inf, F32)
    p_ref[0] = jnp.zeros(p_ref.shape[1:], BF16)
    a_ref[0] = jnp.ones(a_ref.shape[1:], F32)

    def softmax(m_old, s):
        m_new = jnp.maximum(m_old, jnp.max(s, axis=0, keepdims=True))
        return jnp.exp2(s - m_new).astype(BF16), jnp.exp2(m_old - m_new), m_new

    def half_step(j, rd, wr, with_scores):
        prev = jnp.maximum(j - 1, 0) * MLA_WIDE
        m_old = [m_ref[hh][0:1, :] for hh in range(GROUP_HEADS)]
        acc_old = [acc_ref[hh] for hh in range(GROUP_HEADS)]
        s1_prev = [s_ref[rd, n] for n in range(2)]
        p0_prev = [p_ref[rd, n] for n in range(2)]
        a0_prev = [a_ref[rd, n][0:1, :] for n in range(2)]
        vt = [_values_t(vt_ref, prev, MLA_WIDE, hh) for hh in range(GROUP_HEADS)]

        acc = [a0_prev[n] * acc_old[hh] + _dot(vt[hh], p0_prev[n]) for n, hh in enumerate(pairs[0])]
        s0 = qk_pair(0, j * MLA_WIDE) if with_scores else None
        r1 = [softmax(m_old[hh], s1_prev[n]) for n, hh in enumerate(pairs[1])]
        s1 = qk_pair(1, j * MLA_WIDE) if with_scores else None
        acc += [r1[n][1] * acc_old[hh] + _dot(vt[hh], r1[n][0]) for n, hh in enumerate(pairs[1])]
        m_new = [None, None, r1[0][2], r1[1][2]]
        if with_scores:
            r0 = [softmax(m_old[hh], s0[n]) for n, hh in enumerate(pairs[0])]
            m_new[0:2] = [r0[0][2], r0[1][2]]

        for hh in range(GROUP_HEADS):
            acc_ref[hh] = acc[hh]
            if m_new[hh] is not None:
                m_ref[hh] = jnp.broadcast_to(m_new[hh], m_ref.shape[1:])
        if with_scores:
            for n in range(2):
                p_ref[wr, n] = r0[n][0]
                a_ref[wr, n] = jnp.broadcast_to(r0[n][1], a_ref.shape[2:])
                s_ref[wr, n] = s1[n]

    def body(u, carry):
        half_step(2 * u, 0, 1, True)
        half_step(2 * u + 1, 1, 0, True)
        return carry

    lax.fori_loop(0, n_steps // 2, body, 0)

    @pl.when(n_steps % 2 == 1)
    def _():
        half_step(n_steps - 1, 0, 1, True)
        half_step(n_steps, 1, None, False)

    @pl.when(n_steps % 2 == 0)
    def _():
        half_step(n_steps, 0, None, False)


def _softmax_finish(acc_ref, o_ref):
    for p in range(2):
        halves = [acc_ref[hh][0:HEAD_DIM, :] / acc_ref[hh][HEAD_DIM:HEAD_DIM + 1, :] for hh in (2 * p, 2 * p + 1)]
        o_ref[0, :, p * SEG:(p + 1) * SEG] = jnp.concatenate(halves, axis=0).T.astype(o_ref.dtype)


def _mla_kernel(q_ref, k_ref, vt_ref, o_ref, acc_ref, m_ref, s_ref, p_ref, a_ref):
    T = MLA_T
    i = pl.program_id(1)
    key = lax.broadcasted_iota(jnp.int32, (T, T), 0)
    qry = lax.broadcasted_iota(jnp.int32, (T, T), 1)
    causal = key <= qry

    def scores(hh, first_blk, nblk):
        ks = pl.ds(pl.multiple_of(first_blk * T, T), nblk * T)
        return _dot_nt(k_ref[0, ks, hh * SEG:(hh + 1) * SEG], q_ref[0, :, hh * SEG:(hh + 1) * SEG])

    def step(first_blk, nblk, masked):
        heads = range(GROUP_HEADS)
        s = [scores(hh, first_blk, nblk) for hh in heads]
        if masked:
            s = [jnp.where(causal, ss, NEG) for ss in s]
        vt = [_values_t(vt_ref, first_blk, nblk, hh) for hh in heads]
        _softmax_update_all(acc_ref, m_ref, heads, s, vt)

    m_ref[...] = jnp.full(m_ref.shape, NEG, F32)
    acc_ref[...] = jnp.zeros(acc_ref.shape, F32)
    _softmax_pipelined(i // MLA_WIDE, lambda pair, blk: [scores(2 * pair + n, blk, MLA_WIDE) for n in range(2)],
                       vt_ref, acc_ref, m_ref, s_ref, p_ref, a_ref)
    for r in range(MLA_WIDE - 1):
        @pl.when(i % MLA_WIDE > r)
        def _(r=r):
            step(i // MLA_WIDE * MLA_WIDE + r, 1, False)
    step(i, 1, True)
    _softmax_finish(acc_ref, o_ref)


def _softmax_scratch(T):
    tk = MLA_WIDE * T
    return [pltpu.VMEM((GROUP_HEADS, HEAD_DIM + ONES_ROWS, T), F32), pltpu.VMEM((GROUP_HEADS, 8, T), F32),
            pltpu.VMEM((2, 2, tk, T), F32), pltpu.VMEM((2, 2, tk, T), BF16), pltpu.VMEM((2, 2, 8, T), F32)]


def _mla_attn(q, k, vt):
    B, S, _ = k.shape
    T = MLA_T
    assert T == MOBA_BLOCK
    return pl.pallas_call(
        _mla_kernel,
        grid=(B, S // T),
        in_specs=[pl.BlockSpec((1, T, Q_PAD), lambda b, i: (b, i, 0)),
                  pl.BlockSpec((1, S, Q_PAD), lambda b, i: (b, 0, 0)),
                  pl.BlockSpec((1,) + vt.shape[1:], lambda b, i: (b, 0, 0, 0))],
        out_specs=pl.BlockSpec((1, T, GROUP_WIDTH), lambda b, i: (b, i, 0)),
        out_shape=jax.ShapeDtypeStruct((B, S, GROUP_WIDTH), BF16),
        scratch_shapes=_softmax_scratch(T),
        compiler_params=_cparams(("parallel", "arbitrary")),
        name="mla_attn",
    )(q, k, vt)


def _swa_kernel(sink_ref, q_ref, k_ref, v_ref, o_ref):
    W = WINDOW
    i = pl.program_id(1)
    lane = lax.broadcasted_iota(jnp.int32, (1, LANES), 1)
    lo_half = lane < HEAD_DIM
    heads = range(GROUP_HEADS)
    sinks = [sink_ref[hh] for hh in heads]
    chains, ks, vs, bands = [], [], [], []
    for b in range(q_ref.shape[1] // W):
        q0 = i * q_ref.shape[1] + b * W
        start = pl.multiple_of(jnp.maximum(q0 - W, 0), W)
        ks.append(k_ref[0, pl.ds(start, 2 * W), :])
        vs.append(v_ref[0, pl.ds(start, 2 * W), :])
        qpos = q0 + lax.broadcasted_iota(jnp.int32, (W, 2 * W), 0)
        kpos = start + lax.broadcasted_iota(jnp.int32, (W, 2 * W), 1)
        bands.append(jnp.logical_and(kpos <= qpos, qpos - kpos < W))
        chains += [(b, hh) for hh in heads]
    s = [jnp.where(bands[b], _dot_nt(q_ref[0, b * W:(b + 1) * W, hh * SEG:(hh + 1) * SEG], ks[b]), NEG)
         for b, hh in chains]
    m = [jnp.maximum(jnp.max(ss, axis=1, keepdims=True), sinks[hh]) for ss, (b, hh) in zip(s, chains)]
    p = [jnp.exp(ss - mm) for ss, mm in zip(s, m)]
    denom = [jnp.sum(pp, axis=1, keepdims=True) + jnp.exp(sinks[hh] - mm) for pp, mm, (b, hh) in zip(p, m, chains)]
    outs = [_dot(pp.astype(BF16), vs[b]) / dd for pp, dd, (b, hh) in zip(p, denom, chains)]
    for b in range(q_ref.shape[1] // W):
        o = outs[b * GROUP_HEADS:(b + 1) * GROUP_HEADS]
        rows = slice(b * W, (b + 1) * W)
        o_ref[0, rows, 0:SEG] = jnp.where(lo_half, o[0], pltpu.roll(o[1], HEAD_DIM, 1)).astype(o_ref.dtype)
        o_ref[0, rows, SEG:2 * SEG] = jnp.where(lo_half, pltpu.roll(o[2], HEAD_DIM, 1), o[3]).astype(o_ref.dtype)


def _swa_attn(q, k, v, sinks):
    B, S, _ = k.shape
    W = SWA_T
    full = pl.BlockSpec((1, S, 2 * HEAD_DIM), lambda b, i: (b, 0, 0))
    return pl.pallas_call(
        _swa_kernel,
        grid=(B, S // W),
        in_specs=[pl.BlockSpec(memory_space=pltpu.SMEM),
                  pl.BlockSpec((1, W, Q_PAD), lambda b, i: (b, i, 0)), full, full],
        out_specs=pl.BlockSpec((1, W, GROUP_WIDTH), lambda b, i: (b, i, 0)),
        out_shape=jax.ShapeDtypeStruct((B, S, GROUP_WIDTH), BF16),
        compiler_params=_cparams(("parallel", "parallel")),
        name="swa_attn",
    )(sinks, q, k, v)


def _moba_kernel(q_ref, q32_ref, k_ref, vt_ref, kmean_ref, o_ref, acc_ref, m_ref, s_ref, p_ref, a_ref, qaug_ref):
    T = MOBA_BLOCK
    own = pl.program_id(1)
    nbr = kmean_ref.shape[1]
    key = lax.broadcasted_iota(jnp.int32, (T, T), 0)
    qry = lax.broadcasted_iota(jnp.int32, (T, T), 1)
    causal = key <= qry
    lo_half = lax.broadcasted_iota(jnp.int32, (1, LANES), 1) < HEAD_DIM
    blk = lax.broadcasted_iota(jnp.int32, (nbr, T), 0).astype(F32)
    kblk = lax.broadcasted_iota(jnp.int32, (T, LANES), 1)
    own_f = own.astype(F32)

    m_ref[...] = jnp.full(m_ref.shape, NEG, F32)
    acc_ref[...] = jnp.zeros(acc_ref.shape, F32)
    own_ks = pl.ds(pl.multiple_of(own * T, T), T)

    heads = range(GROUP_HEADS)
    pair_sl = [slice((hh // 2) * SEG, (hh // 2 + 1) * SEG) for hh in heads]

    s = [jnp.where(causal, _dot_nt(k_ref[0, own_ks, pair_sl[hh]], q_ref[0, :, hh * SEG:(hh + 1) * SEG]), NEG)
         for hh in heads]
    _softmax_update_all(acc_ref, m_ref, heads, s, [_values_t(vt_ref, own, 1, hh) for hh in heads])

    g = []
    for hh in heads:
        head_lanes = lo_half if hh % 2 == 0 else jnp.logical_not(lo_half)
        qh_hi, qh_lo = _split_bf16(jnp.where(head_lanes, q32_ref[0, :, pair_sl[hh]], 0.0))
        km_hi, km_lo = _split_bf16(kmean_ref[0, :, pair_sl[hh]])
        gate = _dot_nt(km_hi, qh_hi) + (_dot_nt(km_lo, qh_hi) + _dot_nt(km_hi, qh_lo))
        g.append(jnp.where(blk < own_f, gate, NEG))
    sel = [jnp.zeros((nbr, T), F32) for _ in heads]
    for t in range(MOBA_TOPK):
        mx = [jnp.max(gg, axis=0, keepdims=True) for gg in g]
        first = [jnp.min(jnp.where(gg == m, blk, float(nbr)), axis=0, keepdims=True)
                 for gg, m in zip(g, mx)]
        hit = [blk == f for f in first]
        sel = [jnp.where(jnp.logical_and(h, t < own), 1.0, sl) for h, sl in zip(hit, sel)]
        g = [jnp.where(h, -jnp.inf, gg) for h, gg in zip(hit, g)]
    for hh in heads:
        neg_t = jnp.concatenate([(1.0 - sel[hh]) * NEG, jnp.zeros((LANES - nbr, T), F32)], axis=0)
        qaug_ref[hh, :, 0:SEG] = q_ref[0, :, hh * SEG:(hh + 1) * SEG]
        qaug_ref[hh, :, SEG:2 * SEG] = neg_t.T.astype(BF16)

    def scores_pair(pair, n0, nblk):
        ks = pl.ds(pl.multiple_of(n0 * T, T), nblk * T)
        onehot = jnp.concatenate([jnp.where(kblk == n0 + r, 1.0, 0.0) for r in range(nblk)], axis=0).astype(BF16)
        k_aug = jnp.concatenate([k_ref[0, ks, pair * SEG:(pair + 1) * SEG], onehot], axis=1)
        return [_dot_nt(k_aug, qaug_ref[2 * pair + n]) for n in range(2)]

    def step(n0, nblk):
        s = scores_pair(0, n0, nblk) + scores_pair(1, n0, nblk)
        _softmax_update_all(acc_ref, m_ref, heads, s, [_values_t(vt_ref, n0, nblk, hh) for hh in heads])

    _softmax_pipelined(own // MLA_WIDE, lambda pair, blk: scores_pair(pair, blk, MLA_WIDE),
                       vt_ref, acc_ref, m_ref, s_ref, p_ref, a_ref)
    for r in range(MLA_WIDE - 1):
        @pl.when(own % MLA_WIDE > r)
        def _(r=r):
            step(own // MLA_WIDE * MLA_WIDE + r, 1)
    _softmax_finish(acc_ref, o_ref)


def _moba_attn(q, q32, k, vt, kmean):
    B, S, _ = k.shape
    T = MOBA_BLOCK
    assert S // T <= LANES
    nbr = -(-(S // T) // 16) * 16
    kmean = jnp.pad(kmean.reshape(B, S // T, GROUP_WIDTH), ((0, 0), (0, nbr - S // T), (0, 0)))
    return pl.pallas_call(
        _moba_kernel,
        grid=(B, S // T),
        in_specs=[pl.BlockSpec((1, T, Q_PAD), lambda b, i: (b, i, 0)),
                  pl.BlockSpec((1, T, GROUP_WIDTH), lambda b, i: (b, i, 0)),
                  pl.BlockSpec((1, S, GROUP_WIDTH), lambda b, i: (b, 0, 0)),
                  pl.BlockSpec((1,) + vt.shape[1:], lambda b, i: (b, 0, 0, 0)),
                  pl.BlockSpec((1, nbr, GROUP_WIDTH), lambda b, i: (b, 0, 0))],
        out_specs=pl.BlockSpec((1, T, GROUP_WIDTH), lambda b, i: (b, i, 0)),
        out_shape=jax.ShapeDtypeStruct((B, S, GROUP_WIDTH), BF16),
        scratch_shapes=_softmax_scratch(T) + [pltpu.VMEM((GROUP_HEADS, T, 2 * SEG), BF16)],
        compiler_params=_cparams(("parallel", "arbitrary")),
        name="moba_attn",
    )(q, q32, k, vt, kmean)


def _oproj_kernel(oa_ref, ob_ref, oc_ref, od_ref, gn_ref, wo_ref, x_ref, y_ref):
    acc = x_ref[0]
    for g, ref in enumerate((oa_ref, ob_ref, oc_ref, od_ref)):
        o = ref[0].astype(F32)
        ms = jnp.mean(o * o, axis=-1, keepdims=True)
        n = (o * lax.rsqrt(ms + EPS) * gn_ref[:, g * GROUP_WIDTH:(g + 1) * GROUP_WIDTH]).astype(BF16)
        acc = acc + _dot(n, wo_ref[g * GROUP_WIDTH:(g + 1) * GROUP_WIDTH, :])
    y_ref[0] = acc


def _oproj(groups, group_norm, w_o, x):
    B, S, D = x.shape
    ts = OUT_TS
    tok = lambda w: pl.BlockSpec((1, ts, w), lambda b, i: (b, i, 0))
    return pl.pallas_call(
        _oproj_kernel,
        grid=(B, S // ts),
        in_specs=[tok(GROUP_WIDTH)] * 4 + [_const_spec((1, D)), _const_spec(w_o.shape), tok(D)],
        out_specs=tok(D),
        out_shape=jax.ShapeDtypeStruct((B, S, D), F32),
        compiler_params=_cparams(("parallel", "parallel")),
        name="oproj",
    )(*groups, group_norm.reshape(1, D), w_o, x)


def _ffn_kernel(x_ref, p_ref, fn_ref, wup_ref, cw_ref, cb_ref, wdn_ref, pproj_ref, pgate_ref, y_ref,
                halo_ref, ubuf0_ref, ubuf1_ref, acc_ref, h_ref):
    ts = x_ref.shape[1]
    fc2 = wup_ref.shape[2]
    fc = fc2 // 2
    n_chunks = wup_ref.shape[0]
    i = pl.program_id(1)

    @pl.when(i == 0)
    def _():
        halo_ref[...] = jnp.zeros(halo_ref.shape, F32)

    x = x_ref[0]
    ms = jnp.mean(x * x, axis=-1, keepdims=True)
    h_ref[...] = (x * lax.rsqrt(ms + EPS) * fn_ref[...]).astype(BF16)
    acc_ref[...] = x

    def up(c, ubuf_ref):
        u = _dot(h_ref[...], wup_ref[c])
        ubuf_ref[0:HALO, :] = halo_ref[c]
        ubuf_ref[HALO:HALO + ts, :] = u
        halo_ref[c] = u[ts - HALO:ts, :]

    def down(c, ubuf_ref):
        cw = cw_ref[c]
        y = (cw[0:1, :] * ubuf_ref[HALO - 2:HALO - 2 + ts, :] + cw[1:2, :] * ubuf_ref[HALO - 1:HALO - 1 + ts, :]
             + cw[2:3, :] * ubuf_ref[HALO:HALO + ts, :] + cb_ref[c])
        ya = y[:, :fc]
        g = ya * (1.0 / (1.0 + jnp.exp(-ya))) * y[:, fc:]
        acc_ref[...] += _dot(g.astype(BF16), wdn_ref[c])

    up(0, ubuf0_ref)

    def pair(cc, carry):
        up(2 * cc + 1, ubuf1_ref)
        down(2 * cc, ubuf0_ref)
        up(2 * cc + 2, ubuf0_ref)
        down(2 * cc + 1, ubuf1_ref)
        return carry

    assert n_chunks % 2 == 1
    lax.fori_loop(0, n_chunks // 2, pair, 0)
    down(n_chunks - 1, ubuf0_ref)
    x2 = acc_ref[...]
    gate = _dot(x2.astype(BF16), pgate_ref[...])
    emb = _dot(p_ref[0, 0].astype(BF16), pproj_ref[...])
    y_ref[0] = x2 + emb * (1.0 / (1.0 + jnp.exp(-gate)))


def _ffn_weights(ffn_norm, w_up, conv_w, conv_b, w_down, ple_proj, ple_gate):
    fc = FFN_FC
    nc = D_FF // fc
    D = w_up.shape[0]
    pair = lambda a: jnp.concatenate([a[..., :D_FF].reshape(a.shape[:-1] + (nc, fc)),
                                      a[..., D_FF:].reshape(a.shape[:-1] + (nc, fc))], axis=-1)
    wup = jnp.moveaxis(pair(w_up), 1, 0).astype(BF16)
    cw = jnp.pad(jnp.moveaxis(pair(conv_w), 1, 0), ((0, 0), (0, 8 - CONV_WIDTH), (0, 0)))
    cb = jnp.moveaxis(pair(conv_b.reshape(1, -1)), 1, 0)
    wdn = w_down.reshape(nc, fc, D).astype(BF16)
    return ffn_norm.reshape(1, D), wup, cw, cb, wdn, ple_proj.astype(BF16), ple_gate.astype(BF16)


def _ffn(x, p, layer, weights):
    B, S, D = x.shape
    ts = FFN_TS
    fn, wup, cw, cb, wdn, pproj, pgate = weights
    nc, _, fc2 = wup.shape
    tok = lambda w: pl.BlockSpec((1, ts, w), lambda b, i: (b, i, 0))
    once = lambda a: pl.BlockSpec(a.shape, lambda *_: (0,) * a.ndim, pipeline_mode=pl.Buffered(1))
    return pl.pallas_call(
        _ffn_kernel,
        grid=(B, S // ts),
        in_specs=[tok(D), pl.BlockSpec((1, 1, ts, PLE_DIM), lambda b, i: (layer, b, i, 0))] + [once(a) for a in (fn, wup, cw, cb, wdn, pproj, pgate)],
        out_specs=tok(D),
        out_shape=jax.ShapeDtypeStruct((B, S, D), F32),
        scratch_shapes=[pltpu.VMEM((nc, HALO, fc2), F32), pltpu.VMEM((ts + HALO, fc2), F32),
                        pltpu.VMEM((ts + HALO, fc2), F32), pltpu.VMEM((ts, D), F32), pltpu.VMEM((ts, D), BF16)],
        compiler_params=_cparams(("arbitrary", "arbitrary")),
        name="ffn_ple",
    )(x, p, fn, wup, cw, cb, wdn, pproj, pgate)


def kernel(x, p, positions, attn_norm, w_in, mla_q_norm, mla_w_uq, mla_kv_norm, mla_w_ukv, mla_q_gain, mla_k_gain, swa_q_gain, swa_k_gain, swa_sinks, moba_q_gain, moba_k_gain, group_norm, w_o, ffn_norm, w_up, conv_w, conv_b, w_down, ple_proj, ple_gate):
    B, S, D = x.shape
    depth = w_in.shape[0]
    assert D == D_MODEL and S % 512 == 0
    tables = _rope_tables(positions)
    for i in range(depth):
        pw = _prep_weights(w_in[i], mla_q_norm[i], mla_w_uq[i], mla_kv_norm[i], mla_w_ukv[i], mla_q_gain[i],
                           mla_k_gain[i], swa_q_gain[i], swa_k_gain[i], moba_q_gain[i], moba_k_gain[i])
        (qa, ka, va, qm, km, vm, qc, kc, vc, qd, qd32, kd, vd, kmean) = _prep(x, attn_norm[i], tables, pw)
        o_a = _sb_attn(qa, ka, va)
        o_b = _mla_attn(qm, km, vm)
        o_c = _swa_attn(qc, kc, vc, swa_sinks[i])
        o_d = _moba_attn(qd, qd32, kd, vd, kmean)
        x = _oproj((o_a, o_b, o_c, o_d), group_norm[i], w_o[i].astype(BF16), x)
        fw = _ffn_weights(ffn_norm[i], w_up[i], conv_w[i], conv_b[i], w_down[i], ple_proj[i], ple_gate[i])
        x = _ffn(x, p, i, fw)
    return x
```

```python
import functools

import jax
import jax.numpy as jnp
from jax import lax
from jax.experimental import pallas as pl
from jax.experimental.pallas import tpu as pltpu

F32 = jnp.float32
BF16 = jnp.bfloat16

D_MODEL = 1024
HEAD_DIM = 64
GROUP_HEADS = 4
GROUP_WIDTH = GROUP_HEADS * HEAD_DIM
N_GROUPS = 4
ROPE_THETA = 10000.0
EPS = 1e-6
NEG = -1e30
LOG2E = 1.4426950408889634

MLA_Q_RANK = 192
MLA_KV_RANK = 128
MLA_NOPE = 64
MLA_ROPE = 32
MLA_V = 64
MLA_QK = MLA_NOPE + MLA_ROPE
SWA_KV_HEADS = 2
WINDOW = 128
MOBA_BLOCK = 256
MOBA_TOPK = 3
D_FF = 2816
CONV_WIDTH = 3
PLE_DIM = 256

SB_COLS = 3 * GROUP_WIDTH
MLA_COLS = MLA_Q_RANK + MLA_KV_RANK + MLA_ROPE
SWA_COLS = GROUP_WIDTH + 2 * SWA_KV_HEADS * HEAD_DIM
MOBA_COLS = 3 * GROUP_WIDTH
OFF_SB = 0
OFF_MLA = OFF_SB + SB_COLS
OFF_SWA = OFF_MLA + MLA_COLS
OFF_MOBA = OFF_SWA + SWA_COLS

LANES = 128
SEG = LANES
Q_PAD = GROUP_HEADS * SEG
N_IN_PAD = 2560
VMEM_LIMIT = 56 * 1024 * 1024

PREP_TS = 512
FFN_TS = 512
FFN_FC = 256
SWA_T = 512
SB_T = 128
SB_SUB = 2
SB_EXIT = -45.0
MLA_T = 256
MLA_WIDE = 2
HALO = 8
ONES_ROWS = 16


def _dot(a, b):
    return jnp.dot(a, b, preferred_element_type=F32)


def _dot_nt(a, b):
    return lax.dot_general(a, b, (((1,), (1,)), ((), ())), preferred_element_type=F32)


def _split_bf16(x):
    hi = x.astype(BF16)
    lo = (x - hi.astype(F32)).astype(BF16)
    return hi, lo


def _cparams(sem):
    return pltpu.CompilerParams(dimension_semantics=sem, vmem_limit_bytes=VMEM_LIMIT)


def _const_spec(shape):
    nd = len(shape)
    return pl.BlockSpec(shape, lambda *_: (0,) * nd)


def _rope_tables_kernel(pos_ref, inv_ref, c64_ref, s64_ref, cm_ref, sm_ref):
    pos = pos_ref[0].astype(F32)
    lane = lax.broadcasted_iota(jnp.int32, (1, LANES), 1)
    h64, hm = HEAD_DIM // 2, MLA_ROPE // 2
    ang = pos * inv_ref[0:1, :]
    c, s = jnp.cos(ang), jnp.sin(ang)
    c0, s0 = jnp.where(lane < h64, c, 0.0), jnp.where(lane < h64, s, 0.0)
    spread = lambda v: v + pltpu.roll(v, h64, 1) + pltpu.roll(v, 2 * h64, 1) + pltpu.roll(v, 3 * h64, 1)
    first = (lane & (HEAD_DIM - 1)) < h64
    c64_ref[0] = spread(c0)
    s64_ref[0] = jnp.where(first, -spread(s0), spread(s0))
    in_m = (lane >= h64) & (lane < h64 + hm)
    ca, sa = jnp.where(in_m, c, 0.0), jnp.where(in_m, s, 0.0)
    lo_m = (lane >= MLA_NOPE) & (lane < MLA_NOPE + hm)
    hi_m = (lane >= MLA_NOPE + hm) & (lane < MLA_QK)
    to_lo = lambda v: pltpu.roll(v, MLA_NOPE - h64, 1)
    to_hi = lambda v: pltpu.roll(v, MLA_NOPE - h64 + hm, 1)
    cm_ref[0] = jnp.where(lo_m, to_lo(ca), jnp.where(hi_m, to_hi(ca), 1.0))
    sm_ref[0] = jnp.where(lo_m, -to_lo(sa), jnp.where(hi_m, to_hi(sa), 0.0))


def _rope_tables(positions):
    B, S = positions.shape
    ts = 512
    half64 = HEAD_DIM // 2
    halfm = MLA_ROPE // 2
    inv64 = ROPE_THETA ** (-jnp.arange(half64, dtype=F32) / half64)
    invm = ROPE_THETA ** (-jnp.arange(halfm, dtype=F32) / halfm)
    row0 = jnp.concatenate([inv64, invm, jnp.zeros((LANES - half64 - halfm,), F32)])
    inv = jnp.zeros((8, LANES), F32).at[0].set(row0)
    tab = jax.ShapeDtypeStruct((B, S, LANES), F32)
    spec = pl.BlockSpec((1, ts, LANES), lambda b, i: (b, i, 0))
    return pl.pallas_call(
        _rope_tables_kernel,
        grid=(B, S // ts),
        in_specs=[pl.BlockSpec((1, ts, 1), lambda b, i: (b, i, 0)), _const_spec((8, LANES))],
        out_specs=[spec] * 4,
        out_shape=[tab] * 4,
        compiler_params=_cparams(("parallel", "parallel")),
        name="rope_tables",
    )(positions.reshape(B, S, 1), inv)


def _rope64(x, cos, sin_signed, lane):
    first = (lane & (HEAD_DIM - 1)) < (HEAD_DIM // 2)
    partner = jnp.where(first, pltpu.roll(x, LANES - HEAD_DIM // 2, 1), pltpu.roll(x, HEAD_DIM // 2, 1))
    return x * cos + partner * sin_signed


def _rope_mla(x, cos, sin_signed, lane):
    first = lane < MLA_NOPE + MLA_ROPE // 2
    partner = jnp.where(first, pltpu.roll(x, LANES - MLA_ROPE // 2, 1), pltpu.roll(x, MLA_ROPE // 2, 1))
    return x * cos + partner * sin_signed


def _ms64(x, bsel):
    return _dot(jnp.concatenate(_split_bf16(x * x), axis=1), bsel) * (1.0 / HEAD_DIM)


def _prep_kernel(x_ref, an_ref, win_ref, c64_ref, s64_ref, cm_ref, sm_ref,
                 qn_ref, wuq_ref, kvn_ref, wukv_ref, qg_ref, kg_ref, g64_ref,
                 qa_ref, ka_ref, va_ref, qm_ref, km_ref, vm_ref, qc_ref, kc_ref, vc_ref,
                 qd_ref, qd32_ref, kd_ref, vd_ref, kmean_ref):
    ts = x_ref.shape[1]
    subs = [slice(r0, r0 + MOBA_BLOCK) for r0 in range(0, ts, MOBA_BLOCK)]
    projs = []
    for rows in subs:
        x = x_ref[0, rows, :]
        ms = jnp.mean(x * x, axis=-1, keepdims=True)
        h = (x * lax.rsqrt(ms + EPS) * an_ref[...]).astype(BF16)
        projs.append(_dot(h, win_ref[...]))

    lane = lax.broadcasted_iota(jnp.int32, (1, LANES), 1)
    lo_half = lane < HEAD_DIM
    r = (lax.broadcasted_iota(jnp.int32, (2 * LANES, LANES), 0) & (LANES - 1)) // HEAD_DIM
    c = lax.broadcasted_iota(jnp.int32, (2 * LANES, LANES), 1) // HEAD_DIM
    bsel = jnp.where(r == c, 1.0, 0.0).astype(BF16)
    scale64 = HEAD_DIM ** -0.5
    scale_m = MLA_QK ** -0.5 * LOG2E
    for rows, proj in zip(subs, projs):
        _prep_finish(proj, rows, lane, lo_half, bsel, scale64, scale_m, c64_ref, s64_ref, cm_ref, sm_ref,
                     qn_ref, wuq_ref, kvn_ref, wukv_ref, qg_ref, kg_ref, g64_ref,
                     qa_ref, ka_ref, va_ref, qm_ref, km_ref, vm_ref, qc_ref, kc_ref, vc_ref,
                     qd_ref, qd32_ref, kd_ref, vd_ref, kmean_ref)


def _prep_finish(proj, rows, lane, lo_half, bsel, scale64, scale_m, c64_ref, s64_ref, cm_ref, sm_ref,
                 qn_ref, wuq_ref, kvn_ref, wukv_ref, qg_ref, kg_ref, g64_ref,
                 qa_ref, ka_ref, va_ref, qm_ref, km_ref, vm_ref, qc_ref, kc_ref, vc_ref,
                 qd_ref, qd32_ref, kd_ref, vd_ref, kmean_ref):
    blk = rows.start // MOBA_BLOCK
    c64, s64, cm, sm = c64_ref[0, rows, :], s64_ref[0, rows, :], cm_ref[0, rows, :], sm_ref[0, rows, :]

    def seg(off, j):
        return proj[:, off + j * SEG: off + (j + 1) * SEG]

    def store_pair(ref, val, seg_lo, seg_hi):
        ref[0, rows, seg_lo * SEG:(seg_lo + 1) * SEG] = jnp.where(lo_half, val, 0.0).astype(BF16)
        ref[0, rows, seg_hi * SEG:(seg_hi + 1) * SEG] = jnp.where(lo_half, 0.0, val).astype(BF16)

    for p in range(2):
        store_pair(qa_ref, seg(0, p) * scale64, 2 * p, 2 * p + 1)
    ka_ref[0, rows, :] = proj[:, 256:512].astype(BF16)
    va_ref[0, rows, :] = proj[:, 512:768].astype(BF16)

    cq = proj[:, 768:1024]
    cqn = (cq * lax.rsqrt(jnp.sum(cq * cq, axis=-1, keepdims=True) * (1.0 / MLA_Q_RANK) + EPS)
           * qn_ref[...]).astype(BF16)
    qm_raw = _dot(cqn, wuq_ref[...])
    ckv = proj[:, 1024:1152]
    ckvn = (ckv * lax.rsqrt(jnp.mean(ckv * ckv, axis=-1, keepdims=True) + EPS) * kvn_ref[...]).astype(BF16)
    kv = _dot(ckvn, wukv_ref[...])
    kpe = proj[:, 1152:1280]

    def store_transposed(ref, v):
        ref[0, blk] = v.T.astype(BF16)

    store_transposed(vm_ref, kv[:, 512:768])
    for hh in range(GROUP_HEADS):
        sl = slice(hh * SEG, (hh + 1) * SEG)
        qs = qm_raw[:, sl]
        qs = qs * lax.rsqrt(jnp.sum(qs * qs, axis=-1, keepdims=True) * (1.0 / MLA_QK) + EPS) * qg_ref[...]
        qm_ref[0, rows, sl] = (_rope_mla(qs, cm, sm, lane) * scale_m).astype(BF16)
        ks = kv[:, sl] + kpe
        ks = ks * lax.rsqrt(jnp.sum(ks * ks, axis=-1, keepdims=True) * (1.0 / MLA_QK) + EPS) * kg_ref[...]
        km_ref[0, rows, sl] = _rope_mla(ks, cm, sm, lane).astype(BF16)

    def norm_rope64(v, gain_row):
        v = v * lax.rsqrt(_ms64(v, bsel) + EPS) * g64_ref[gain_row:gain_row + 1, :]
        return _rope64(v, c64, s64, lane)

    qa_pair = norm_rope64(seg(1280, 0), 0) * scale64
    qb_pair = norm_rope64(seg(1280, 1), 0) * scale64
    store_pair(qc_ref, qa_pair, 0, 2)
    store_pair(qc_ref, qb_pair, 1, 3)
    kc_ref[0, rows, :] = norm_rope64(seg(1536, 0), 1).astype(BF16)
    vc_ref[0, rows, :] = seg(1664, 0).astype(BF16)

    for p in range(2):
        qd = norm_rope64(seg(1792, p), 2)
        qd32_ref[0, rows, p * SEG:(p + 1) * SEG] = qd
        store_pair(qd_ref, qd * (scale64 * LOG2E), 2 * p, 2 * p + 1)
        kd = norm_rope64(seg(2048, p), 3)
        kd_ref[0, rows, p * SEG:(p + 1) * SEG] = kd.astype(BF16)
        kmean_ref[0, blk, :, p * SEG:(p + 1) * SEG] = jnp.mean(kd, axis=0, keepdims=True)
    store_transposed(vd_ref, proj[:, 2304:2560])


def _prep_weights(w_in, mla_q_norm, mla_w_uq, mla_kv_norm, mla_w_ukv, mla_q_gain, mla_k_gain,
                  swa_q_gain, swa_k_gain, moba_q_gain, moba_k_gain):
    D = w_in.shape[0]
    z = lambda n: jnp.zeros((D, n), F32)
    mla = w_in[:, OFF_MLA:OFF_MLA + MLA_COLS]
    swa = w_in[:, OFF_SWA:OFF_SWA + SWA_COLS]
    swa_q = swa[:, :GROUP_WIDTH].reshape(D, GROUP_HEADS, HEAD_DIM)[:, jnp.array([0, 2, 1, 3])].reshape(D, GROUP_WIDTH)
    win = jnp.concatenate([
        w_in[:, OFF_SB:OFF_SB + SB_COLS],
        mla[:, :MLA_Q_RANK], z(256 - MLA_Q_RANK),
        mla[:, MLA_Q_RANK:MLA_Q_RANK + MLA_KV_RANK],
        z(MLA_NOPE), mla[:, MLA_Q_RANK + MLA_KV_RANK:], z(SEG - MLA_QK),
        swa_q, swa[:, GROUP_WIDTH:],
        w_in[:, OFF_MOBA:OFF_MOBA + MOBA_COLS],
    ], axis=1).astype(BF16)
    assert win.shape[1] == N_IN_PAD
    qn = jnp.pad(mla_q_norm, (0, 256 - MLA_Q_RANK)).reshape(1, 256)
    wuq = mla_w_uq.reshape(MLA_Q_RANK, GROUP_HEADS, MLA_QK)
    wuq = jnp.pad(wuq, ((0, 256 - MLA_Q_RANK), (0, 0), (0, SEG - MLA_QK))).reshape(256, Q_PAD).astype(BF16)
    wukv = mla_w_ukv.reshape(MLA_KV_RANK, GROUP_HEADS, MLA_NOPE + MLA_V)
    wkn = jnp.pad(wukv[:, :, :MLA_NOPE], ((0, 0), (0, 0), (0, SEG - MLA_NOPE))).reshape(MLA_KV_RANK, Q_PAD)
    wv = wukv[:, :, MLA_NOPE:].reshape(MLA_KV_RANK, GROUP_WIDTH)
    wukv_r = jnp.concatenate([wkn, wv], axis=1).astype(BF16)
    qg = jnp.pad(mla_q_gain, (0, SEG - MLA_QK)).reshape(1, SEG)
    kg = jnp.pad(mla_k_gain, (0, SEG - MLA_QK)).reshape(1, SEG)
    g64 = jnp.zeros((8, LANES), F32)
    for row, g in enumerate((swa_q_gain, swa_k_gain, moba_q_gain, moba_k_gain)):
        g64 = g64.at[row].set(jnp.tile(g, LANES // HEAD_DIM))
    return win, qn, wuq, mla_kv_norm.reshape(1, MLA_KV_RANK), wukv_r, qg, kg, g64


def _prep(x, attn_norm, tables, weights):
    B, S, D = x.shape
    ts = PREP_TS
    win, qn, wuq, kvn, wukv, qg, kg, g64 = weights
    tok = lambda w: pl.BlockSpec((1, ts, w), lambda b, i: (b, i, 0))
    nb_t = ts // MOBA_BLOCK
    VT = "channel-major value blocks"
    out_widths = [(Q_PAD, BF16), (256, BF16), (256, BF16),
                  (Q_PAD, BF16), (Q_PAD, BF16), (VT, BF16),
                  (Q_PAD, BF16), (128, BF16), (128, BF16),
                  (Q_PAD, BF16), (256, F32), (256, BF16), (VT, BF16)]
    vt_shape = jax.ShapeDtypeStruct((B, S // MOBA_BLOCK, GROUP_WIDTH, MOBA_BLOCK), BF16)
    vt_spec = pl.BlockSpec((1, nb_t, GROUP_WIDTH, MOBA_BLOCK), lambda b, i: (b, i, 0, 0))
    out_shape = [vt_shape if w is VT else jax.ShapeDtypeStruct((B, S, w), dt) for w, dt in out_widths]
    out_specs = [vt_spec if w is VT else tok(w) for w, _ in out_widths]
    out_shape.append(jax.ShapeDtypeStruct((B, S // MOBA_BLOCK, 1, 256), F32))
    out_specs.append(pl.BlockSpec((1, nb_t, 1, 256), lambda b, i: (b, i, 0, 0)))
    consts = [attn_norm.reshape(1, D), win]
    tail = [qn, wuq, kvn, wukv, qg, kg, g64]
    in_specs = ([tok(D)] + [_const_spec(a.shape) for a in consts] + [tok(LANES)] * 4
                + [_const_spec(a.shape) for a in tail])
    return pl.pallas_call(
        _prep_kernel,
        grid=(B, S // ts),
        in_specs=in_specs,
        out_specs=out_specs,
        out_shape=out_shape,
        compiler_params=_cparams(("parallel", "parallel")),
        name="prep",
    )(x, *consts, *tables, *tail)


def _sb_kernel(q_ref, k_ref, v_ref, o_ref, acc_ref, carry_ref):
    T = SB_T
    i = pl.program_id(1)
    row = lax.broadcasted_iota(jnp.int32, (T, T), 0)
    col = lax.broadcasted_iota(jnp.int32, (T, T), 1)
    upper = jnp.concatenate([jnp.where(row > col, 1.0, 0.0), jnp.ones((T, T), F32)], axis=1).astype(BF16)
    upper = jnp.concatenate([upper, upper], axis=0)
    causal = col < row
    lane = lax.broadcasted_iota(jnp.int32, (1, LANES), 1)

    n_sub = q_ref.shape[1] // T
    chains = [(sub, hh) for sub in range(n_sub) for hh in range(GROUP_HEADS)]

    def block(t, first):
        kb = [i * n_sub + sub - t for sub in range(n_sub)]
        ks = [pl.ds(pl.multiple_of(jnp.maximum(b, 0) * T, T), T) for b in kb]
        kv = lambda ref, sub, hh: ref[0, ks[sub], (hh // 2) * SEG:(hh // 2 + 1) * SEG]
        z = [_dot_nt(q_ref[0, sub * T:(sub + 1) * T, hh * SEG:(hh + 1) * SEG], kv(k_ref, sub, hh))
             for sub, hh in chains]
        log_beta = [jnp.minimum(zz, 0.0) - jnp.log(1.0 + jnp.exp(-jnp.abs(zz))) for zz in z]
        log_keep = [lb - zz for lb, zz in zip(log_beta, z)]
        if first:
            log_keep = [jnp.where(causal, lk, 0.0) for lk in log_keep]
        parts = [jnp.concatenate(_split_bf16(lk), axis=1) for lk in log_keep]
        sums = [_dot(hl, upper) for hl in parts]
        if first:
            w = [jnp.where(causal, jnp.exp(lb + sm[:, :T]), 0.0) for lb, sm in zip(log_beta, sums)]
            carry = [sm[:, T:] for sm in sums]
        else:
            live = [kb[sub] >= 0 for sub, hh in chains]
            old = [carry_ref[c] for c in range(len(chains))]
            w = [jnp.where(lv, jnp.exp(lb + (sm[:, :T] + cr)), 0.0) for lv, lb, sm, cr in zip(live, log_beta, sums, old)]
            carry = [jnp.where(lv, cr + sm[:, T:], NEG) for lv, sm, cr in zip(live, sums, old)]
        pv = [_dot(ww.astype(BF16), kv(v_ref, sub, hh)) for (sub, hh), ww in zip(chains, w)]
        for c in range(len(chains)):
            acc_ref[c] = pv[c] if first else acc_ref[c] + pv[c]
            carry_ref[c] = carry[c]
        top = carry[0]
        for cr in carry[1:]:
            top = jnp.maximum(top, cr)
        return jnp.max(top)

    def cond(st):
        t, mx = st
        return jnp.logical_and(t <= i * n_sub + n_sub - 1, mx > SB_EXIT)

    def body(st):
        return st[0] + 1, block(st[0], False)

    lax.while_loop(cond, body, (1, block(0, True)))

    lo_half = lane < HEAD_DIM
    for sub in range(n_sub):
        for p in range(2):
            c = sub * GROUP_HEADS + 2 * p
            o_ref[0, sub * T:(sub + 1) * T, p * SEG:(p + 1) * SEG] = jnp.where(
                lo_half, acc_ref[c], acc_ref[c + 1]).astype(o_ref.dtype)


def _sb_attn(q, k, v):
    B, S, _ = k.shape
    T = SB_T
    rows = SB_SUB * T
    n_chains = SB_SUB * GROUP_HEADS
    full = pl.BlockSpec((1, S, GROUP_WIDTH), lambda b, i: (b, 0, 0))
    return pl.pallas_call(
        _sb_kernel,
        grid=(B, S // rows),
        in_specs=[pl.BlockSpec((1, rows, Q_PAD), lambda b, i: (b, i, 0)), full, full],
        out_specs=pl.BlockSpec((1, rows, GROUP_WIDTH), lambda b, i: (b, i, 0)),
        out_shape=jax.ShapeDtypeStruct((B, S, GROUP_WIDTH), BF16),
        scratch_shapes=[pltpu.VMEM((n_chains, T, SEG), F32), pltpu.VMEM((n_chains, T, T), F32)],
        compiler_params=_cparams(("parallel", "arbitrary")),
        name="sb_attn",
    )(q, k, v)


def _softmax_update_all(acc_ref, m_ref, heads, scores, vt_augs):
    m_old = [m_ref[hh][0:1, :] for hh in heads]
    acc_old = [acc_ref[hh] for hh in heads]
    m_new = [jnp.maximum(mo, jnp.max(s, axis=0, keepdims=True)) for mo, s in zip(m_old, scores)]
    p = [jnp.exp2(s - mn).astype(BF16) for s, mn in zip(scores, m_new)]
    pv = [_dot(va, pp) for va, pp in zip(vt_augs, p)]
    for hh, mo, mn, r, a in zip(heads, m_old, m_new, pv, acc_old):
        acc_ref[hh] = jnp.exp2(mo - mn) * a + r
        m_ref[hh] = jnp.broadcast_to(mn, m_ref.shape[1:])


def _values_t(vt_ref, first_blk, nblk, hh):
    rows = slice(hh * HEAD_DIM, (hh + 1) * HEAD_DIM)
    v = jnp.concatenate([vt_ref[0, first_blk + r, rows, :] for r in range(nblk)], axis=1)
    return jnp.concatenate([v, jnp.ones((ONES_ROWS, v.shape[1]), v.dtype)], axis=0)


def _softmax_pipelined(n_steps, qk_pair, vt_ref, acc_ref, m_ref, s_ref, p_ref, a_ref):
    pairs = ((0, 1), (2, 3))
    s_ref[0] = jnp.full(s_ref.shape[1:], -jnp.inf, F32)
    p_ref[0] = jnp.zeros(p_ref.shape[1:], BF16)
    a_ref[0] = jnp.ones(a_ref.shape[1:], F32)

    def softmax(m_old, s):
        m_new = jnp.maximum(m_old, jnp.max(s, axis=0, keepdims=True))
        return jnp.exp2(s - m_new).astype(BF16), jnp.exp2(m_old - m_new), m_new

    def half_step(j, rd, wr, with_scores):
        prev = jnp.maximum(j - 1, 0) * MLA_WIDE
        m_old = [m_ref[hh][0:1, :] for hh in range(GROUP_HEADS)]
        acc_old = [acc_ref[hh] for hh in range(GROUP_HEADS)]
        s1_prev = [s_ref[rd, n] for n in range(2)]
        p0_prev = [p_ref[rd, n] for n in range(2)]
        a0_prev = [a_ref[rd, n][0:1, :] for n in range(2)]
        vt = [_values_t(vt_ref, prev, MLA_WIDE, hh) for hh in range(GROUP_HEADS)]

        acc = [a0_prev[n] * acc_old[hh] + _dot(vt[hh], p0_prev[n]) for n, hh in enumerate(pairs[0])]
        s0 = qk_pair(0, j * MLA_WIDE) if with_scores else None
        r1 = [softmax(m_old[hh], s1_prev[n]) for n, hh in enumerate(pairs[1])]
        s1 = qk_pair(1, j * MLA_WIDE) if with_scores else None
        acc += [r1[n][1] * acc_old[hh] + _dot(vt[hh], r1[n][0]) for n, hh in enumerate(pairs[1])]
        m_new = [None, None, r1[0][2], r1[1][2]]
        if with_scores:
            r0 = [softmax(m_old[hh], s0[n]) for n, hh in enumerate(pairs[0])]
            m_new[0:2] = [r0[0][2], r0[1][2]]

        for hh in range(GROUP_HEADS):
            acc_ref[hh] = acc[hh]
            if m_new[hh] is not None:
                m_ref[hh] = jnp.broadcast_to(m_new[hh], m_ref.shape[1:])
        if with_scores:
            for n in range(2):
                p_ref[wr, n] = r0[n][0]
                a_ref[wr, n] = jnp.broadcast_to(r0[n][1], a_ref.shape[2:])
                s_ref[wr, n] = s1[n]

    def body(u, carry):
        half_step(2 * u, 0, 1, True)
        half_step(2 * u + 1, 1, 0, True)
        return carry

    lax.fori_loop(0, n_steps // 2, body, 0)

    @pl.when(n_steps % 2 == 1)
    def _():
        half_step(n_steps - 1, 0, 1, True)
        half_step(n_steps, 1, None, False)

    @pl.when(n_steps % 2 == 0)
    def _():
        half_step(n_steps, 0, None, False)


def _softmax_finish(acc_ref, o_ref):
    for p in range(2):
        halves = [acc_ref[hh][0:HEAD_DIM, :] / acc_ref[hh][HEAD_DIM:HEAD_DIM + 1, :] for hh in (2 * p, 2 * p + 1)]
        o_ref[0, :, p * SEG:(p + 1) * SEG] = jnp.concatenate(halves, axis=0).T.astype(o_ref.dtype)


def _mla_kernel(q_ref, k_ref, vt_ref, o_ref, acc_ref, m_ref, s_ref, p_ref, a_ref):
    T = MLA_T
    i = pl.program_id(1)
    key = lax.broadcasted_iota(jnp.int32, (T, T), 0)
    qry = lax.broadcasted_iota(jnp.int32, (T, T), 1)
    causal = key <= qry

    def scores(hh, first_blk, nblk):
        ks = pl.ds(pl.multiple_of(first_blk * T, T), nblk * T)
        return _dot_nt(k_ref[0, ks, hh * SEG:(hh + 1) * SEG], q_ref[0, :, hh * SEG:(hh + 1) * SEG])

    def step(first_blk, nblk, masked):
        heads = range(GROUP_HEADS)
        s = [scores(hh, first_blk, nblk) for hh in heads]
        if masked:
            s = [jnp.where(causal, ss, NEG) for ss in s]
        vt = [_values_t(vt_ref, first_blk, nblk, hh) for hh in heads]
        _softmax_update_all(acc_ref, m_ref, heads, s, vt)

    m_ref[...] = jnp.full(m_ref.shape, NEG, F32)
    acc_ref[...] = jnp.zeros(acc_ref.shape, F32)
    _softmax_pipelined(i // MLA_WIDE, lambda pair, blk: [scores(2 * pair + n, blk, MLA_WIDE) for n in range(2)],
                       vt_ref, acc_ref, m_ref, s_ref, p_ref, a_ref)
    for r in range(MLA_WIDE - 1):
        @pl.when(i % MLA_WIDE > r)
        def _(r=r):
            step(i // MLA_WIDE * MLA_WIDE + r, 1, False)
    step(i, 1, True)
    _softmax_finish(acc_ref, o_ref)


def _softmax_scratch(T):
    tk = MLA_WIDE * T
    return [pltpu.VMEM((GROUP_HEADS, HEAD_DIM + ONES_ROWS, T), F32), pltpu.VMEM((GROUP_HEADS, 8, T), F32),
            pltpu.VMEM((2, 2, tk, T), F32), pltpu.VMEM((2, 2, tk, T), BF16), pltpu.VMEM((2, 2, 8, T), F32)]


def _mla_attn(q, k, vt):
    B, S, _ = k.shape
    T = MLA_T
    assert T == MOBA_BLOCK
    return pl.pallas_call(
        _mla_kernel,
        grid=(B, S // T),
        in_specs=[pl.BlockSpec((1, T, Q_PAD), lambda b, i: (b, i, 0)),
                  pl.BlockSpec((1, S, Q_PAD), lambda b, i: (b, 0, 0)),
                  pl.BlockSpec((1,) + vt.shape[1:], lambda b, i: (b, 0, 0, 0))],
        out_specs=pl.BlockSpec((1, T, GROUP_WIDTH), lambda b, i: (b, i, 0)),
        out_shape=jax.ShapeDtypeStruct((B, S, GROUP_WIDTH), BF16),
        scratch_shapes=_softmax_scratch(T),
        compiler_params=_cparams(("parallel", "arbitrary")),
        name="mla_attn",
    )(q, k, vt)


def _swa_kernel(sink_ref, q_ref, k_ref, v_ref, o_ref):
    W = WINDOW
    i = pl.program_id(1)
    lane = lax.broadcasted_iota(jnp.int32, (1, LANES), 1)
    lo_half = lane < HEAD_DIM
    heads = range(GROUP_HEADS)
    sinks = [sink_ref[hh] for hh in heads]
    chains, ks, vs, bands = [], [], [], []
    for b in range(q_ref.shape[1] // W):
        q0 = i * q_ref.shape[1] + b * W
        start = pl.multiple_of(jnp.maximum(q0 - W, 0), W)
        ks.append(k_ref[0, pl.ds(start, 2 * W), :])
        vs.append(v_ref[0, pl.ds(start, 2 * W), :])
        qpos = q0 + lax.broadcasted_iota(jnp.int32, (W, 2 * W), 0)
        kpos = start + lax.broadcasted_iota(jnp.int32, (W, 2 * W), 1)
        bands.append(jnp.logical_and(kpos <= qpos, qpos - kpos < W))
        chains += [(b, hh) for hh in heads]
    s = [jnp.where(bands[b], _dot_nt(q_ref[0, b * W:(b + 1) * W, hh * SEG:(hh + 1) * SEG], ks[b]), NEG)
         for b, hh in chains]
    m = [jnp.maximum(jnp.max(ss, axis=1, keepdims=True), sinks[hh]) for ss, (b, hh) in zip(s, chains)]
    p = [jnp.exp(ss - mm) for ss, mm in zip(s, m)]
    denom = [jnp.sum(pp, axis=1, keepdims=True) + jnp.exp(sinks[hh] - mm) for pp, mm, (b, hh) in zip(p, m, chains)]
    outs = [_dot(pp.astype(BF16), vs[b]) / dd for pp, dd, (b, hh) in zip(p, denom, chains)]
    for b in range(q_ref.shape[1] // W):
        o = outs[b * GROUP_HEADS:(b + 1) * GROUP_HEADS]
        rows = slice(b * W, (b + 1) * W)
        o_ref[0, rows, 0:SEG] = jnp.where(lo_half, o[0], pltpu.roll(o[1], HEAD_DIM, 1)).astype(o_ref.dtype)
        o_ref[0, rows, SEG:2 * SEG] = jnp.where(lo_half, pltpu.roll(o[2], HEAD_DIM, 1), o[3]).astype(o_ref.dtype)


def _swa_attn(q, k, v, sinks):
    B, S, _ = k.shape
    W = SWA_T
    full = pl.BlockSpec((1, S, 2 * HEAD_DIM), lambda b, i: (b, 0, 0))
    return pl.pallas_call(
        _swa_kernel,
        grid=(B, S // W),
        in_specs=[pl.BlockSpec(memory_space=pltpu.SMEM),
                  pl.BlockSpec((1, W, Q_PAD), lambda b, i: (b, i, 0)), full, full],
        out_specs=pl.BlockSpec((1, W, GROUP_WIDTH), lambda b, i: (b, i, 0)),
        out_shape=jax.ShapeDtypeStruct((B, S, GROUP_WIDTH), BF16),
        compiler_params=_cparams(("parallel", "parallel")),
        name="swa_attn",
    )(sinks, q, k, v)


def _moba_kernel(q_ref, q32_ref, k_ref, vt_ref, kmean_ref, o_ref, acc_ref, m_ref, s_ref, p_ref, a_ref, qaug_ref):
    T = MOBA_BLOCK
    own = pl.program_id(1)
    nbr = kmean_ref.shape[1]
    key = lax.broadcasted_iota(jnp.int32, (T, T), 0)
    qry = lax.broadcasted_iota(jnp.int32, (T, T), 1)
    causal = key <= qry
    lo_half = lax.broadcasted_iota(jnp.int32, (1, LANES), 1) < HEAD_DIM
    blk = lax.broadcasted_iota(jnp.int32, (nbr, T), 0).astype(F32)
    kblk = lax.broadcasted_iota(jnp.int32, (T, LANES), 1)
    own_f = own.astype(F32)

    m_ref[...] = jnp.full(m_ref.shape, NEG, F32)
    acc_ref[...] = jnp.zeros(acc_ref.shape, F32)
    own_ks = pl.ds(pl.multiple_of(own * T, T), T)

    heads = range(GROUP_HEADS)
    pair_sl = [slice((hh // 2) * SEG, (hh // 2 + 1) * SEG) for hh in heads]

    s = [jnp.where(causal, _dot_nt(k_ref[0, own_ks, pair_sl[hh]], q_ref[0, :, hh * SEG:(hh + 1) * SEG]), NEG)
         for hh in heads]
    _softmax_update_all(acc_ref, m_ref, heads, s, [_values_t(vt_ref, own, 1, hh) for hh in heads])

    g = []
    for hh in heads:
        head_lanes = lo_half if hh % 2 == 0 else jnp.logical_not(lo_half)
        qh_hi, qh_lo = _split_bf16(jnp.where(head_lanes, q32_ref[0, :, pair_sl[hh]], 0.0))
        km_hi, km_lo = _split_bf16(kmean_ref[0, :, pair_sl[hh]])
        gate = _dot_nt(km_hi, qh_hi) + (_dot_nt(km_lo, qh_hi) + _dot_nt(km_hi, qh_lo))
        g.append(jnp.where(blk < own_f, gate, NEG))
    sel = [jnp.zeros((nbr, T), F32) for _ in heads]
    for t in range(MOBA_TOPK):
        mx = [jnp.max(gg, axis=0, keepdims=True) for gg in g]
        first = [jnp.min(jnp.where(gg == m, blk, float(nbr)), axis=0, keepdims=True)
                 for gg, m in zip(g, mx)]
        hit = [blk == f for f in first]
        sel = [jnp.where(jnp.logical_and(h, t < own), 1.0, sl) for h, sl in zip(hit, sel)]
        g = [jnp.where(h, -jnp.inf, gg) for h, gg in zip(hit, g)]
    for hh in heads:
        neg_t = jnp.concatenate([(1.0 - sel[hh]) * NEG, jnp.zeros((LANES - nbr, T), F32)], axis=0)
        qaug_ref[hh, :, 0:SEG] = q_ref[0, :, hh * SEG:(hh + 1) * SEG]
        qaug_ref[hh, :, SEG:2 * SEG] = neg_t.T.astype(BF16)

    def scores_pair(pair, n0, nblk):
        ks = pl.ds(pl.multiple_of(n0 * T, T), nblk * T)
        onehot = jnp.concatenate([jnp.where(kblk == n0 + r, 1.0, 0.0) for r in range(nblk)], axis=0).astype(BF16)
        k_aug = jnp.concatenate([k_ref[0, ks, pair * SEG:(pair + 1) * SEG], onehot], axis=1)
        return [_dot_nt(k_aug, qaug_ref[2 * pair + n]) for n in range(2)]

    def step(n0, nblk):
        s = scores_pair(0, n0, nblk) + scores_pair(1, n0, nblk)
        _softmax_update_all(acc_ref, m_ref, heads, s, [_values_t(vt_ref, n0, nblk, hh) for hh in heads])

    _softmax_pipelined(own // MLA_WIDE, lambda pair, blk: scores_pair(pair, blk, MLA_WIDE),
                       vt_ref, acc_ref, m_ref, s_ref, p_ref, a_ref)
    for r in range(MLA_WIDE - 1):
        @pl.when(own % MLA_WIDE > r)
        def _(r=r):
            step(own // MLA_WIDE * MLA_WIDE + r, 1)
    _softmax_finish(acc_ref, o_ref)


def _moba_attn(q, q32, k, vt, kmean):
    B, S, _ = k.shape
    T = MOBA_BLOCK
    assert S // T <= LANES
    nbr = -(-(S // T) // 16) * 16
    kmean = jnp.pad(kmean.reshape(B, S // T, GROUP_WIDTH), ((0, 0), (0, nbr - S // T), (0, 0)))
    return pl.pallas_call(
        _moba_kernel,
        grid=(B, S // T),
        in_specs=[pl.BlockSpec((1, T, Q_PAD), lambda b, i: (b, i, 0)),
                  pl.BlockSpec((1, T, GROUP_WIDTH), lambda b, i: (b, i, 0)),
                  pl.BlockSpec((1, S, GROUP_WIDTH), lambda b, i: (b, 0, 0)),
                  pl.BlockSpec((1,) + vt.shape[1:], lambda b, i: (b, 0, 0, 0)),
                  pl.BlockSpec((1, nbr, GROUP_WIDTH), lambda b, i: (b, 0, 0))],
        out_specs=pl.BlockSpec((1, T, GROUP_WIDTH), lambda b, i: (b, i, 0)),
        out_shape=jax.ShapeDtypeStruct((B, S, GROUP_WIDTH), BF16),
        scratch_shapes=_softmax_scratch(T) + [pltpu.VMEM((GROUP_HEADS, T, 2 * SEG), BF16)],
        compiler_params=_cparams(("parallel", "arbitrary")),
        name="moba_attn",
    )(q, q32, k, vt, kmean)


def _mix_groups(x, group_refs, gn_ref, wo_ref):
    for g, ref in enumerate(group_refs):
        o = ref[0].astype(F32)
        ms = jnp.mean(o * o, axis=-1, keepdims=True)
        n = (o * lax.rsqrt(ms + EPS) * gn_ref[:, g * GROUP_WIDTH:(g + 1) * GROUP_WIDTH]).astype(BF16)
        x = x + _dot(n, wo_ref[g * GROUP_WIDTH:(g + 1) * GROUP_WIDTH, :])
    return x


def _ffn_kernel(x_ref, oa_ref, ob_ref, oc_ref, od_ref, p_ref, gn_ref, wo_ref,
                fn_ref, wup_ref, cw_ref, cb_ref, wdn_ref, pproj_ref, pgate_ref, y_ref,
                halo_ref, ubuf0_ref, ubuf1_ref, h_ref):
    ts = x_ref.shape[1]
    fc2 = wup_ref.shape[2]
    fc = fc2 // 2
    n_chunks = wup_ref.shape[0]
    i = pl.program_id(1)

    @pl.when(i == 0)
    def _():
        halo_ref[...] = jnp.zeros(halo_ref.shape, F32)

    x = _mix_groups(x_ref[0], (oa_ref, ob_ref, oc_ref, od_ref), gn_ref, wo_ref)
    ms = jnp.mean(x * x, axis=-1, keepdims=True)
    h_ref[...] = (x * lax.rsqrt(ms + EPS) * fn_ref[...]).astype(BF16)
    y_ref[0] = x

    def up(c, ubuf_ref):
        u = _dot(h_ref[...], wup_ref[c])
        ubuf_ref[0:HALO, :] = halo_ref[c]
        ubuf_ref[HALO:HALO + ts, :] = u
        halo_ref[c] = u[ts - HALO:ts, :]

    def down(c, ubuf_ref):
        cw = cw_ref[c]
        y = (cw[0:1, :] * ubuf_ref[HALO - 2:HALO - 2 + ts, :] + cw[1:2, :] * ubuf_ref[HALO - 1:HALO - 1 + ts, :]
             + cw[2:3, :] * ubuf_ref[HALO:HALO + ts, :] + cb_ref[c])
        ya = y[:, :fc]
        g = ya * (1.0 / (1.0 + jnp.exp(-ya))) * y[:, fc:]
        y_ref[0] += _dot(g.astype(BF16), wdn_ref[c])

    up(0, ubuf0_ref)

    def pair(cc, carry):
        up(2 * cc + 1, ubuf1_ref)
        down(2 * cc, ubuf0_ref)
        up(2 * cc + 2, ubuf0_ref)
        down(2 * cc + 1, ubuf1_ref)
        return carry

    assert n_chunks % 2 == 1
    lax.fori_loop(0, n_chunks // 2, pair, 0)
    down(n_chunks - 1, ubuf0_ref)
    x2 = y_ref[0]
    gate = _dot(x2.astype(BF16), pgate_ref[...])
    emb = _dot(p_ref[0, 0].astype(BF16), pproj_ref[...])
    y_ref[0] = x2 + emb * (1.0 / (1.0 + jnp.exp(-gate)))


def _ffn_weights(ffn_norm, w_up, conv_w, conv_b, w_down, ple_proj, ple_gate):
    fc = FFN_FC
    nc = D_FF // fc
    D = w_up.shape[0]
    pair = lambda a: jnp.concatenate([a[..., :D_FF].reshape(a.shape[:-1] + (nc, fc)),
                                      a[..., D_FF:].reshape(a.shape[:-1] + (nc, fc))], axis=-1)
    wup = jnp.moveaxis(pair(w_up), 1, 0).astype(BF16)
    cw = jnp.pad(jnp.moveaxis(pair(conv_w), 1, 0), ((0, 0), (0, 8 - CONV_WIDTH), (0, 0)))
    cb = jnp.moveaxis(pair(conv_b.reshape(1, -1)), 1, 0)
    wdn = w_down.reshape(nc, fc, D).astype(BF16)
    return ffn_norm.reshape(1, D), wup, cw, cb, wdn, ple_proj.astype(BF16), ple_gate.astype(BF16)


def _ffn(x, groups, p, layer, group_norm, w_o, weights):
    B, S, D = x.shape
    ts = FFN_TS
    fn, wup, cw, cb, wdn, pproj, pgate = weights
    consts = (group_norm.reshape(1, D), w_o) + tuple(weights)
    nc, _, fc2 = wup.shape
    tok = lambda w: pl.BlockSpec((1, ts, w), lambda b, i: (b, i, 0))
    once = lambda a: pl.BlockSpec(a.shape, lambda *_: (0,) * a.ndim, pipeline_mode=pl.Buffered(1))
    return pl.pallas_call(
        _ffn_kernel,
        grid=(B, S // ts),
        in_specs=([tok(D)] + [tok(GROUP_WIDTH)] * 4
                  + [pl.BlockSpec((1, 1, ts, PLE_DIM), lambda b, i: (layer, b, i, 0))] + [once(a) for a in consts]),
        out_specs=tok(D),
        out_shape=jax.ShapeDtypeStruct((B, S, D), F32),
        scratch_shapes=[pltpu.VMEM((nc, HALO, fc2), F32), pltpu.VMEM((ts + HALO, fc2), F32),
                        pltpu.VMEM((ts + HALO, fc2), F32), pltpu.VMEM((ts, D), BF16)],
        compiler_params=_cparams(("arbitrary", "arbitrary")),
        name="ffn_ple",
    )(x, *groups, p, *consts)


def kernel(x, p, positions, attn_norm, w_in, mla_q_norm, mla_w_uq, mla_kv_norm, mla_w_ukv, mla_q_gain, mla_k_gain, swa_q_gain, swa_k_gain, swa_sinks, moba_q_gain, moba_k_gain, group_norm, w_o, ffn_norm, w_up, conv_w, conv_b, w_down, ple_proj, ple_gate):
    B, S, D = x.shape
    depth = w_in.shape[0]
    assert D == D_MODEL and S % 512 == 0
    tables = _rope_tables(positions)
    for i in range(depth):
        pw = _prep_weights(w_in[i], mla_q_norm[i], mla_w_uq[i], mla_kv_norm[i], mla_w_ukv[i], mla_q_gain[i],
                           mla_k_gain[i], swa_q_gain[i], swa_k_gain[i], moba_q_gain[i], moba_k_gain[i])
        (qa, ka, va, qm, km, vm, qc, kc, vc, qd, qd32, kd, vd, kmean) = _prep(x, attn_norm[i], tables, pw)
        o_a = _sb_attn(qa, ka, va)
        o_b = _mla_attn(qm, km, vm)
        o_c = _swa_attn(qc, kc, vc, swa_sinks[i])
        o_d = _moba_attn(qd, qd32, kd, vd, kmean)
        fw = _ffn_weights(ffn_norm[i], w_up[i], conv_w[i], conv_b[i], w_down[i], ple_proj[i], ple_gate[i])
        x = _ffn(x, (o_a, o_b, o_c, o_d), p, i, group_norm[i], w_o[i].astype(BF16), fw)
    return x
```

```python
import functools

import jax
import jax.numpy as jnp
from jax import lax
from jax.experimental import pallas as pl
from jax.experimental.pallas import tpu as pltpu

F32 = jnp.float32
BF16 = jnp.bfloat16

D_MODEL = 1024
HEAD_DIM = 64
GROUP_HEADS = 4
GROUP_WIDTH = GROUP_HEADS * HEAD_DIM
N_GROUPS = 4
ROPE_THETA = 10000.0
EPS = 1e-6
NEG = -1e30
LOG2E = 1.4426950408889634

MLA_Q_RANK = 192
MLA_KV_RANK = 128
MLA_NOPE = 64
MLA_ROPE = 32
MLA_V = 64
MLA_QK = MLA_NOPE + MLA_ROPE
SWA_KV_HEADS = 2
WINDOW = 128
MOBA_BLOCK = 256
MOBA_TOPK = 3
D_FF = 2816
CONV_WIDTH = 3
PLE_DIM = 256

SB_COLS = 3 * GROUP_WIDTH
MLA_COLS = MLA_Q_RANK + MLA_KV_RANK + MLA_ROPE
SWA_COLS = GROUP_WIDTH + 2 * SWA_KV_HEADS * HEAD_DIM
MOBA_COLS = 3 * GROUP_WIDTH
OFF_SB = 0
OFF_MLA = OFF_SB + SB_COLS
OFF_SWA = OFF_MLA + MLA_COLS
OFF_MOBA = OFF_SWA + SWA_COLS

LANES = 128
SEG = LANES
Q_PAD = GROUP_HEADS * SEG
N_IN_PAD = 2560
VMEM_LIMIT = 56 * 1024 * 1024

PREP_TS = 512
FFN_TS = 512
FFN_FC = 256
SWA_T = 512
SB_T = 128
SB_SUB = 4
SB_EXIT = -45.0
MLA_T = 256
MLA_WIDE = 2
HALO = 8
ONES_ROWS = 16


def _dot(a, b):
    return jnp.dot(a, b, preferred_element_type=F32)


def _dot_nt(a, b):
    return lax.dot_general(a, b, (((1,), (1,)), ((), ())), preferred_element_type=F32)


def _split_bf16(x):
    hi = x.astype(BF16)
    lo = (x - hi.astype(F32)).astype(BF16)
    return hi, lo


def _cparams(sem):
    return pltpu.CompilerParams(dimension_semantics=sem, vmem_limit_bytes=VMEM_LIMIT)


def _const_spec(shape):
    nd = len(shape)
    return pl.BlockSpec(shape, lambda *_: (0,) * nd)


def _rope_tables_kernel(pos_ref, inv_ref, c64_ref, s64_ref, cm_ref, sm_ref):
    pos = pos_ref[0].astype(F32)
    lane = lax.broadcasted_iota(jnp.int32, (1, LANES), 1)
    h64, hm = HEAD_DIM // 2, MLA_ROPE // 2
    ang = pos * inv_ref[0:1, :]
    c, s = jnp.cos(ang), jnp.sin(ang)
    c0, s0 = jnp.where(lane < h64, c, 0.0), jnp.where(lane < h64, s, 0.0)
    spread = lambda v: v + pltpu.roll(v, h64, 1) + pltpu.roll(v, 2 * h64, 1) + pltpu.roll(v, 3 * h64, 1)
    first = (lane & (HEAD_DIM - 1)) < h64
    c64_ref[0] = spread(c0)
    s64_ref[0] = jnp.where(first, -spread(s0), spread(s0))
    in_m = (lane >= h64) & (lane < h64 + hm)
    ca, sa = jnp.where(in_m, c, 0.0), jnp.where(in_m, s, 0.0)
    lo_m = (lane >= MLA_NOPE) & (lane < MLA_NOPE + hm)
    hi_m = (lane >= MLA_NOPE + hm) & (lane < MLA_QK)
    to_lo = lambda v: pltpu.roll(v, MLA_NOPE - h64, 1)
    to_hi = lambda v: pltpu.roll(v, MLA_NOPE - h64 + hm, 1)
    cm_ref[0] = jnp.where(lo_m, to_lo(ca), jnp.where(hi_m, to_hi(ca), 1.0))
    sm_ref[0] = jnp.where(lo_m, -to_lo(sa), jnp.where(hi_m, to_hi(sa), 0.0))


def _rope_tables(positions):
    B, S = positions.shape
    ts = 512
    half64 = HEAD_DIM // 2
    halfm = MLA_ROPE // 2
    inv64 = ROPE_THETA ** (-jnp.arange(half64, dtype=F32) / half64)
    invm = ROPE_THETA ** (-jnp.arange(halfm, dtype=F32) / halfm)
    row0 = jnp.concatenate([inv64, invm, jnp.zeros((LANES - half64 - halfm,), F32)])
    inv = jnp.zeros((8, LANES), F32).at[0].set(row0)
    tab = jax.ShapeDtypeStruct((B, S, LANES), F32)
    spec = pl.BlockSpec((1, ts, LANES), lambda b, i: (b, i, 0))
    return pl.pallas_call(
        _rope_tables_kernel,
        grid=(B, S // ts),
        in_specs=[pl.BlockSpec((1, ts, 1), lambda b, i: (b, i, 0)), _const_spec((8, LANES))],
        out_specs=[spec] * 4,
        out_shape=[tab] * 4,
        compiler_params=_cparams(("parallel", "parallel")),
        name="rope_tables",
    )(positions.reshape(B, S, 1), inv)


def _rope64(x, cos, sin_signed, lane):
    first = (lane & (HEAD_DIM - 1)) < (HEAD_DIM // 2)
    partner = jnp.where(first, pltpu.roll(x, LANES - HEAD_DIM // 2, 1), pltpu.roll(x, HEAD_DIM // 2, 1))
    return x * cos + partner * sin_signed


def _rope_mla(x, cos, sin_signed, lane):
    first = lane < MLA_NOPE + MLA_ROPE // 2
    partner = jnp.where(first, pltpu.roll(x, LANES - MLA_ROPE // 2, 1), pltpu.roll(x, MLA_ROPE // 2, 1))
    return x * cos + partner * sin_signed


def _ms64(x, bsel):
    return _dot(jnp.concatenate(_split_bf16(x * x), axis=1), bsel) * (1.0 / HEAD_DIM)


def _prep_kernel(x_ref, an_ref, win_ref, c64_ref, s64_ref, cm_ref, sm_ref,
                 qn_ref, wuq_ref, kvn_ref, wukv_ref, qg_ref, kg_ref, g64_ref,
                 qa_ref, ka_ref, va_ref, qm_ref, km_ref, vm_ref, qc_ref, kc_ref, vc_ref,
                 qd_ref, qd32_ref, kd_ref, vd_ref, kmean_ref):
    ts = x_ref.shape[1]
    subs = [slice(r0, r0 + MOBA_BLOCK) for r0 in range(0, ts, MOBA_BLOCK)]
    projs = []
    for rows in subs:
        x = x_ref[0, rows, :]
        ms = jnp.mean(x * x, axis=-1, keepdims=True)
        h = (x * lax.rsqrt(ms + EPS) * an_ref[...]).astype(BF16)
        projs.append(_dot(h, win_ref[...]))

    lane = lax.broadcasted_iota(jnp.int32, (1, LANES), 1)
    lo_half = lane < HEAD_DIM
    r = (lax.broadcasted_iota(jnp.int32, (2 * LANES, LANES), 0) & (LANES - 1)) // HEAD_DIM
    c = lax.broadcasted_iota(jnp.int32, (2 * LANES, LANES), 1) // HEAD_DIM
    bsel = jnp.where(r == c, 1.0, 0.0).astype(BF16)
    scale64 = HEAD_DIM ** -0.5
    scale_m = MLA_QK ** -0.5 * LOG2E
    for rows, proj in zip(subs, projs):
        _prep_finish(proj, rows, lane, lo_half, bsel, scale64, scale_m, c64_ref, s64_ref, cm_ref, sm_ref,
                     qn_ref, wuq_ref, kvn_ref, wukv_ref, qg_ref, kg_ref, g64_ref,
                     qa_ref, ka_ref, va_ref, qm_ref, km_ref, vm_ref, qc_ref, kc_ref, vc_ref,
                     qd_ref, qd32_ref, kd_ref, vd_ref, kmean_ref)


def _prep_finish(proj, rows, lane, lo_half, bsel, scale64, scale_m, c64_ref, s64_ref, cm_ref, sm_ref,
                 qn_ref, wuq_ref, kvn_ref, wukv_ref, qg_ref, kg_ref, g64_ref,
                 qa_ref, ka_ref, va_ref, qm_ref, km_ref, vm_ref, qc_ref, kc_ref, vc_ref,
                 qd_ref, qd32_ref, kd_ref, vd_ref, kmean_ref):
    blk = rows.start // MOBA_BLOCK
    c64, s64, cm, sm = c64_ref[0, rows, :], s64_ref[0, rows, :], cm_ref[0, rows, :], sm_ref[0, rows, :]

    def seg(off, j):
        return proj[:, off + j * SEG: off + (j + 1) * SEG]

    def store_pair(ref, val, seg_lo, seg_hi):
        ref[0, rows, seg_lo * SEG:(seg_lo + 1) * SEG] = jnp.where(lo_half, val, 0.0).astype(BF16)
        ref[0, rows, seg_hi * SEG:(seg_hi + 1) * SEG] = jnp.where(lo_half, 0.0, val).astype(BF16)

    for p in range(2):
        store_pair(qa_ref, seg(0, p) * scale64, 2 * p, 2 * p + 1)
    ka_ref[0, rows, :] = proj[:, 256:512].astype(BF16)
    va_ref[0, rows, :] = proj[:, 512:768].astype(BF16)

    cq = proj[:, 768:1024]
    cqn = (cq * lax.rsqrt(jnp.sum(cq * cq, axis=-1, keepdims=True) * (1.0 / MLA_Q_RANK) + EPS)
           * qn_ref[...]).astype(BF16)
    qm_raw = _dot(cqn, wuq_ref[...])
    ckv = proj[:, 1024:1152]
    ckvn = (ckv * lax.rsqrt(jnp.mean(ckv * ckv, axis=-1, keepdims=True) + EPS) * kvn_ref[...]).astype(BF16)
    kv = _dot(ckvn, wukv_ref[...])
    kpe = proj[:, 1152:1280]

    def store_transposed(ref, v):
        ref[0, blk] = v.T.astype(BF16)

    store_transposed(vm_ref, kv[:, 512:768])
    for hh in range(GROUP_HEADS):
        sl = slice(hh * SEG, (hh + 1) * SEG)
        qs = qm_raw[:, sl]
        qs = qs * lax.rsqrt(jnp.sum(qs * qs, axis=-1, keepdims=True) * (1.0 / MLA_QK) + EPS) * qg_ref[...]
        qm_ref[0, rows, sl] = (_rope_mla(qs, cm, sm, lane) * scale_m).astype(BF16)
        ks = kv[:, sl] + kpe
        ks = ks * lax.rsqrt(jnp.sum(ks * ks, axis=-1, keepdims=True) * (1.0 / MLA_QK) + EPS) * kg_ref[...]
        km_ref[0, rows, sl] = _rope_mla(ks, cm, sm, lane).astype(BF16)

    def norm_rope64(v, gain_row):
        v = v * lax.rsqrt(_ms64(v, bsel) + EPS) * g64_ref[gain_row:gain_row + 1, :]
        return _rope64(v, c64, s64, lane)

    qa_pair = norm_rope64(seg(1280, 0), 0) * scale64
    qb_pair = norm_rope64(seg(1280, 1), 0) * scale64
    store_pair(qc_ref, qa_pair, 0, 2)
    store_pair(qc_ref, qb_pair, 1, 3)
    kc_ref[0, rows, :] = norm_rope64(seg(1536, 0), 1).astype(BF16)
    vc_ref[0, rows, :] = seg(1664, 0).astype(BF16)

    for p in range(2):
        qd = norm_rope64(seg(1792, p), 2)
        qd32_ref[0, rows, p * SEG:(p + 1) * SEG] = qd
        store_pair(qd_ref, qd * (scale64 * LOG2E), 2 * p, 2 * p + 1)
        kd = norm_rope64(seg(2048, p), 3)
        kd_ref[0, rows, p * SEG:(p + 1) * SEG] = kd.astype(BF16)
        kmean_ref[0, blk, :, p * SEG:(p + 1) * SEG] = jnp.mean(kd, axis=0, keepdims=True)
    store_transposed(vd_ref, proj[:, 2304:2560])


def _prep_weights(w_in, mla_q_norm, mla_w_uq, mla_kv_norm, mla_w_ukv, mla_q_gain, mla_k_gain,
                  swa_q_gain, swa_k_gain, moba_q_gain, moba_k_gain):
    D = w_in.shape[0]
    z = lambda n: jnp.zeros((D, n), F32)
    mla = w_in[:, OFF_MLA:OFF_MLA + MLA_COLS]
    swa = w_in[:, OFF_SWA:OFF_SWA + SWA_COLS]
    swa_q = swa[:, :GROUP_WIDTH].reshape(D, GROUP_HEADS, HEAD_DIM)[:, jnp.array([0, 2, 1, 3])].reshape(D, GROUP_WIDTH)
    win = jnp.concatenate([
        w_in[:, OFF_SB:OFF_SB + SB_COLS],
        mla[:, :MLA_Q_RANK], z(256 - MLA_Q_RANK),
        mla[:, MLA_Q_RANK:MLA_Q_RANK + MLA_KV_RANK],
        z(MLA_NOPE), mla[:, MLA_Q_RANK + MLA_KV_RANK:], z(SEG - MLA_QK),
        swa_q, swa[:, GROUP_WIDTH:],
        w_in[:, OFF_MOBA:OFF_MOBA + MOBA_COLS],
    ], axis=1).astype(BF16)
    assert win.shape[1] == N_IN_PAD
    qn = jnp.pad(mla_q_norm, (0, 256 - MLA_Q_RANK)).reshape(1, 256)
    wuq = mla_w_uq.reshape(MLA_Q_RANK, GROUP_HEADS, MLA_QK)
    wuq = jnp.pad(wuq, ((0, 256 - MLA_Q_RANK), (0, 0), (0, SEG - MLA_QK))).reshape(256, Q_PAD).astype(BF16)
    wukv = mla_w_ukv.reshape(MLA_KV_RANK, GROUP_HEADS, MLA_NOPE + MLA_V)
    wkn = jnp.pad(wukv[:, :, :MLA_NOPE], ((0, 0), (0, 0), (0, SEG - MLA_NOPE))).reshape(MLA_KV_RANK, Q_PAD)
    wv = wukv[:, :, MLA_NOPE:].reshape(MLA_KV_RANK, GROUP_WIDTH)
    wukv_r = jnp.concatenate([wkn, wv], axis=1).astype(BF16)
    qg = jnp.pad(mla_q_gain, (0, SEG - MLA_QK)).reshape(1, SEG)
    kg = jnp.pad(mla_k_gain, (0, SEG - MLA_QK)).reshape(1, SEG)
    g64 = jnp.zeros((8, LANES), F32)
    for row, g in enumerate((swa_q_gain, swa_k_gain, moba_q_gain, moba_k_gain)):
        g64 = g64.at[row].set(jnp.tile(g, LANES // HEAD_DIM))
    return win, qn, wuq, mla_kv_norm.reshape(1, MLA_KV_RANK), wukv_r, qg, kg, g64


def _prep(x, attn_norm, tables, weights):
    B, S, D = x.shape
    ts = PREP_TS
    win, qn, wuq, kvn, wukv, qg, kg, g64 = weights
    tok = lambda w: pl.BlockSpec((1, ts, w), lambda b, i: (b, i, 0))
    nb_t = ts // MOBA_BLOCK
    VT = "channel-major value blocks"
    out_widths = [(Q_PAD, BF16), (256, BF16), (256, BF16),
                  (Q_PAD, BF16), (Q_PAD, BF16), (VT, BF16),
                  (Q_PAD, BF16), (128, BF16), (128, BF16),
                  (Q_PAD, BF16), (256, F32), (256, BF16), (VT, BF16)]
    vt_shape = jax.ShapeDtypeStruct((B, S // MOBA_BLOCK, GROUP_WIDTH, MOBA_BLOCK), BF16)
    vt_spec = pl.BlockSpec((1, nb_t, GROUP_WIDTH, MOBA_BLOCK), lambda b, i: (b, i, 0, 0))
    out_shape = [vt_shape if w is VT else jax.ShapeDtypeStruct((B, S, w), dt) for w, dt in out_widths]
    out_specs = [vt_spec if w is VT else tok(w) for w, _ in out_widths]
    out_shape.append(jax.ShapeDtypeStruct((B, S // MOBA_BLOCK, 1, 256), F32))
    out_specs.append(pl.BlockSpec((1, nb_t, 1, 256), lambda b, i: (b, i, 0, 0)))
    consts = [attn_norm.reshape(1, D), win]
    tail = [qn, wuq, kvn, wukv, qg, kg, g64]
    in_specs = ([tok(D)] + [_const_spec(a.shape) for a in consts] + [tok(LANES)] * 4
                + [_const_spec(a.shape) for a in tail])
    return pl.pallas_call(
        _prep_kernel,
        grid=(B, S // ts),
        in_specs=in_specs,
        out_specs=out_specs,
        out_shape=out_shape,
        compiler_params=_cparams(("parallel", "parallel")),
        name="prep",
    )(x, *consts, *tables, *tail)


def _sb_kernel(q_ref, k_ref, v_ref, o_ref, acc_ref, carry_ref):
    T = SB_T
    i = pl.program_id(1)
    row = lax.broadcasted_iota(jnp.int32, (T, T), 0)
    col = lax.broadcasted_iota(jnp.int32, (T, T), 1)
    upper = jnp.concatenate([jnp.where(row > col, 1.0, 0.0), jnp.ones((T, T), F32)], axis=1).astype(BF16)
    upper = jnp.concatenate([upper, upper], axis=0)
    causal = col < row
    lane = lax.broadcasted_iota(jnp.int32, (1, LANES), 1)

    n_sub = q_ref.shape[1] // T
    chains = [(sub, hh) for sub in range(n_sub) for hh in range(GROUP_HEADS)]

    def block(t, first):
        kb = [i * n_sub + sub - t for sub in range(n_sub)]
        ks = [pl.ds(pl.multiple_of(jnp.maximum(b, 0) * T, T), T) for b in kb]
        kv = lambda ref, sub, hh: ref[0, ks[sub], (hh // 2) * SEG:(hh // 2 + 1) * SEG]
        z = [_dot_nt(q_ref[0, sub * T:(sub + 1) * T, hh * SEG:(hh + 1) * SEG], kv(k_ref, sub, hh))
             for sub, hh in chains]
        log_beta = [jnp.minimum(zz, 0.0) - jnp.log(1.0 + jnp.exp(-jnp.abs(zz))) for zz in z]
        log_keep = [lb - zz for lb, zz in zip(log_beta, z)]
        if first:
            log_keep = [jnp.where(causal, lk, 0.0) for lk in log_keep]
        parts = [jnp.concatenate(_split_bf16(lk), axis=1) for lk in log_keep]
        sums = [_dot(hl, upper) for hl in parts]
        if first:
            w = [jnp.where(causal, jnp.exp(lb + sm[:, :T]), 0.0) for lb, sm in zip(log_beta, sums)]
            carry = [sm[:, T:] for sm in sums]
        else:
            live = [kb[sub] >= 0 for sub, hh in chains]
            old = [carry_ref[c] for c in range(len(chains))]
            w = [jnp.where(lv, jnp.exp(lb + (sm[:, :T] + cr)), 0.0) for lv, lb, sm, cr in zip(live, log_beta, sums, old)]
            carry = [jnp.where(lv, cr + sm[:, T:], NEG) for lv, sm, cr in zip(live, sums, old)]
        pv = [_dot(ww.astype(BF16), kv(v_ref, sub, hh)) for (sub, hh), ww in zip(chains, w)]
        for c in range(len(chains)):
            acc_ref[c] = pv[c] if first else acc_ref[c] + pv[c]
            carry_ref[c] = carry[c]
        top = carry[0]
        for cr in carry[1:]:
            top = jnp.maximum(top, cr)
        return jnp.max(top)

    def cond(st):
        t, mx = st
        return jnp.logical_and(t <= i * n_sub + n_sub - 1, mx > SB_EXIT)

    def body(st):
        return st[0] + 1, block(st[0], False)

    lax.while_loop(cond, body, (1, block(0, True)))

    lo_half = lane < HEAD_DIM
    for sub in range(n_sub):
        for p in range(2):
            c = sub * GROUP_HEADS + 2 * p
            o_ref[0, sub * T:(sub + 1) * T, p * SEG:(p + 1) * SEG] = jnp.where(
                lo_half, acc_ref[c], acc_ref[c + 1]).astype(o_ref.dtype)


def _sb_attn(q, k, v):
    B, S, _ = k.shape
    T = SB_T
    rows = SB_SUB * T
    n_chains = SB_SUB * GROUP_HEADS
    full = pl.BlockSpec((1, S, GROUP_WIDTH), lambda b, i: (b, 0, 0))
    return pl.pallas_call(
        _sb_kernel,
        grid=(B, S // rows),
        in_specs=[pl.BlockSpec((1, rows, Q_PAD), lambda b, i: (b, i, 0)), full, full],
        out_specs=pl.BlockSpec((1, rows, GROUP_WIDTH), lambda b, i: (b, i, 0)),
        out_shape=jax.ShapeDtypeStruct((B, S, GROUP_WIDTH), BF16),
        scratch_shapes=[pltpu.VMEM((n_chains, T, SEG), F32), pltpu.VMEM((n_chains, T, T), F32)],
        compiler_params=_cparams(("parallel", "arbitrary")),
        name="sb_attn",
    )(q, k, v)


def _softmax_update_all(acc_ref, m_ref, heads, scores, vt_augs):
    m_old = [m_ref[hh][0:1, :] for hh in heads]
    acc_old = [acc_ref[hh] for hh in heads]
    m_new = [jnp.maximum(mo, jnp.max(s, axis=0, keepdims=True)) for mo, s in zip(m_old, scores)]
    p = [jnp.exp2(s - mn).astype(BF16) for s, mn in zip(scores, m_new)]
    pv = [_dot(va, pp) for va, pp in zip(vt_augs, p)]
    for hh, mo, mn, r, a in zip(heads, m_old, m_new, pv, acc_old):
        acc_ref[hh] = jnp.exp2(mo - mn) * a + r
        m_ref[hh] = jnp.broadcast_to(mn, m_ref.shape[1:])


def _values_t(vt_ref, first_blk, nblk, hh):
    rows = slice(hh * HEAD_DIM, (hh + 1) * HEAD_DIM)
    v = jnp.concatenate([vt_ref[0, first_blk + r, rows, :] for r in range(nblk)], axis=1)
    return jnp.concatenate([v, jnp.ones((ONES_ROWS, v.shape[1]), v.dtype)], axis=0)


def _softmax_pipelined(n_steps, qk_pair, vt_ref, acc_ref, m_ref, s_ref, p_ref, a_ref):
    pairs = ((0, 1), (2, 3))
    s_ref[0] = jnp.full(s_ref.shape[1:], -jnp.inf, F32)
    p_ref[0] = jnp.zeros(p_ref.shape[1:], BF16)
    a_ref[0] = jnp.ones(a_ref.shape[1:], F32)

    def softmax(m_old, s):
        m_new = jnp.maximum(m_old, jnp.max(s, axis=0, keepdims=True))
        return jnp.exp2(s - m_new).astype(BF16), jnp.exp2(m_old - m_new), m_new

    def half_step(j, rd, wr, with_scores):
        prev = jnp.maximum(j - 1, 0) * MLA_WIDE
        m_old = [m_ref[hh][0:1, :] for hh in range(GROUP_HEADS)]
        acc_old = [acc_ref[hh] for hh in range(GROUP_HEADS)]
        s1_prev = [s_ref[rd, n] for n in range(2)]
        p0_prev = [p_ref[rd, n] for n in range(2)]
        a0_prev = [a_ref[rd, n][0:1, :] for n in range(2)]
        vt = [_values_t(vt_ref, prev, MLA_WIDE, hh) for hh in range(GROUP_HEADS)]

        acc = [a0_prev[n] * acc_old[hh] + _dot(vt[hh], p0_prev[n]) for n, hh in enumerate(pairs[0])]
        s0 = qk_pair(0, j * MLA_WIDE) if with_scores else None
        r1 = [softmax(m_old[hh], s1_prev[n]) for n, hh in enumerate(pairs[1])]
        s1 = qk_pair(1, j * MLA_WIDE) if with_scores else None
        acc += [r1[n][1] * acc_old[hh] + _dot(vt[hh], r1[n][0]) for n, hh in enumerate(pairs[1])]
        m_new = [None, None, r1[0][2], r1[1][2]]
        if with_scores:
            r0 = [softmax(m_old[hh], s0[n]) for n, hh in enumerate(pairs[0])]
            m_new[0:2] = [r0[0][2], r0[1][2]]

        for hh in range(GROUP_HEADS):
            acc_ref[hh] = acc[hh]
            if m_new[hh] is not None:
                m_ref[hh] = jnp.broadcast_to(m_new[hh], m_ref.shape[1:])
        if with_scores:
            for n in range(2):
                p_ref[wr, n] = r0[n][0]
                a_ref[wr, n] = jnp.broadcast_to(r0[n][1], a_ref.shape[2:])
                s_ref[wr, n] = s1[n]

    def body(u, carry):
        half_step(2 * u, 0, 1, True)
        half_step(2 * u + 1, 1, 0, True)
        return carry

    lax.fori_loop(0, n_steps // 2, body, 0)

    @pl.when(n_steps % 2 == 1)
    def _():
        half_step(n_steps - 1, 0, 1, True)
        half_step(n_steps, 1, None, False)

    @pl.when(n_steps % 2 == 0)
    def _():
        half_step(n_steps, 0, None, False)


def _softmax_finish(acc_ref, o_ref):
    for p in range(2):
        halves = [acc_ref[hh][0:HEAD_DIM, :] / acc_ref[hh][HEAD_DIM:HEAD_DIM + 1, :] for hh in (2 * p, 2 * p + 1)]
        o_ref[0, :, p * SEG:(p + 1) * SEG] = jnp.concatenate(halves, axis=0).T.astype(o_ref.dtype)


def _mla_kernel(q_ref, k_ref, vt_ref, o_ref, acc_ref, m_ref, s_ref, p_ref, a_ref):
    T = MLA_T
    i = pl.program_id(1)
    key = lax.broadcasted_iota(jnp.int32, (T, T), 0)
    qry = lax.broadcasted_iota(jnp.int32, (T, T), 1)
    causal = key <= qry

    def scores(hh, first_blk, nblk):
        ks = pl.ds(pl.multiple_of(first_blk * T, T), nblk * T)
        return _dot_nt(k_ref[0, ks, hh * SEG:(hh + 1) * SEG], q_ref[0, :, hh * SEG:(hh + 1) * SEG])

    def step(first_blk, nblk, masked):
        heads = range(GROUP_HEADS)
        s = [scores(hh, first_blk, nblk) for hh in heads]
        if masked:
            s = [jnp.where(causal, ss, NEG) for ss in s]
        vt = [_values_t(vt_ref, first_blk, nblk, hh) for hh in heads]
        _softmax_update_all(acc_ref, m_ref, heads, s, vt)

    m_ref[...] = jnp.full(m_ref.shape, NEG, F32)
    acc_ref[...] = jnp.zeros(acc_ref.shape, F32)
    _softmax_pipelined(i // MLA_WIDE, lambda pair, blk: [scores(2 * pair + n, blk, MLA_WIDE) for n in range(2)],
                       vt_ref, acc_ref, m_ref, s_ref, p_ref, a_ref)
    for r in range(MLA_WIDE - 1):
        @pl.when(i % MLA_WIDE > r)
        def _(r=r):
            step(i // MLA_WIDE * MLA_WIDE + r, 1, False)
    step(i, 1, True)
    _softmax_finish(acc_ref, o_ref)


def _softmax_scratch(T):
    tk = MLA_WIDE * T
    return [pltpu.VMEM((GROUP_HEADS, HEAD_DIM + ONES_ROWS, T), F32), pltpu.VMEM((GROUP_HEADS, 8, T), F32),
            pltpu.VMEM((2, 2, tk, T), F32), pltpu.VMEM((2, 2, tk, T), BF16), pltpu.VMEM((2, 2, 8, T), F32)]


def _mla_attn(q, k, vt):
    B, S, _ = k.shape
    T = MLA_T
    assert T == MOBA_BLOCK
    return pl.pallas_call(
        _mla_kernel,
        grid=(B, S // T),
        in_specs=[pl.BlockSpec((1, T, Q_PAD), lambda b, i: (b, i, 0)),
                  pl.BlockSpec((1, S, Q_PAD), lambda b, i: (b, 0, 0)),
                  pl.BlockSpec((1,) + vt.shape[1:], lambda b, i: (b, 0, 0, 0))],
        out_specs=pl.BlockSpec((1, T, GROUP_WIDTH), lambda b, i: (b, i, 0)),
        out_shape=jax.ShapeDtypeStruct((B, S, GROUP_WIDTH), BF16),
        scratch_shapes=_softmax_scratch(T),
        compiler_params=_cparams(("parallel", "arbitrary")),
        name="mla_attn",
    )(q, k, vt)


def _swa_kernel(sink_ref, q_ref, k_ref, v_ref, o_ref):
    W = WINDOW
    i = pl.program_id(1)
    lane = lax.broadcasted_iota(jnp.int32, (1, LANES), 1)
    lo_half = lane < HEAD_DIM
    heads = range(GROUP_HEADS)
    sinks = [sink_ref[hh] for hh in heads]
    chains, ks, vs, bands = [], [], [], []
    for b in range(q_ref.shape[1] // W):
        q0 = i * q_ref.shape[1] + b * W
        start = pl.multiple_of(jnp.maximum(q0 - W, 0), W)
        ks.append(k_ref[0, pl.ds(start, 2 * W), :])
        vs.append(v_ref[0, pl.ds(start, 2 * W), :])
        qpos = q0 + lax.broadcasted_iota(jnp.int32, (W, 2 * W), 0)
        kpos = start + lax.broadcasted_iota(jnp.int32, (W, 2 * W), 1)
        bands.append(jnp.logical_and(kpos <= qpos, qpos - kpos < W))
        chains += [(b, hh) for hh in heads]
    s = [jnp.where(bands[b], _dot_nt(q_ref[0, b * W:(b + 1) * W, hh * SEG:(hh + 1) * SEG], ks[b]), NEG)
         for b, hh in chains]
    m = [jnp.maximum(jnp.max(ss, axis=1, keepdims=True), sinks[hh]) for ss, (b, hh) in zip(s, chains)]
    p = [jnp.exp(ss - mm) for ss, mm in zip(s, m)]
    denom = [jnp.sum(pp, axis=1, keepdims=True) + jnp.exp(sinks[hh] - mm) for pp, mm, (b, hh) in zip(p, m, chains)]
    outs = [_dot(pp.astype(BF16), vs[b]) / dd for pp, dd, (b, hh) in zip(p, denom, chains)]
    for b in range(q_ref.shape[1] // W):
        o = outs[b * GROUP_HEADS:(b + 1) * GROUP_HEADS]
        rows = slice(b * W, (b + 1) * W)
        o_ref[0, rows, 0:SEG] = jnp.where(lo_half, o[0], pltpu.roll(o[1], HEAD_DIM, 1)).astype(o_ref.dtype)
        o_ref[0, rows, SEG:2 * SEG] = jnp.where(lo_half, pltpu.roll(o[2], HEAD_DIM, 1), o[3]).astype(o_ref.dtype)


def _swa_attn(q, k, v, sinks):
    B, S, _ = k.shape
    W = SWA_T
    full = pl.BlockSpec((1, S, 2 * HEAD_DIM), lambda b, i: (b, 0, 0))
    return pl.pallas_call(
        _swa_kernel,
        grid=(B, S // W),
        in_specs=[pl.BlockSpec(memory_space=pltpu.SMEM),
                  pl.BlockSpec((1, W, Q_PAD), lambda b, i: (b, i, 0)), full, full],
        out_specs=pl.BlockSpec((1, W, GROUP_WIDTH), lambda b, i: (b, i, 0)),
        out_shape=jax.ShapeDtypeStruct((B, S, GROUP_WIDTH), BF16),
        compiler_params=_cparams(("parallel", "parallel")),
        name="swa_attn",
    )(sinks, q, k, v)


def _moba_kernel(q_ref, q32_ref, k_ref, vt_ref, kmean_ref, o_ref, acc_ref, m_ref, s_ref, p_ref, a_ref, qaug_ref):
    T = MOBA_BLOCK
    own = pl.program_id(1)
    nbr = kmean_ref.shape[1]
    key = lax.broadcasted_iota(jnp.int32, (T, T), 0)
    qry = lax.broadcasted_iota(jnp.int32, (T, T), 1)
    causal = key <= qry
    lo_half = lax.broadcasted_iota(jnp.int32, (1, LANES), 1) < HEAD_DIM
    blk = lax.broadcasted_iota(jnp.int32, (nbr, T), 0).astype(F32)
    kblk = lax.broadcasted_iota(jnp.int32, (T, LANES), 1)
    own_f = own.astype(F32)

    m_ref[...] = jnp.full(m_ref.shape, NEG, F32)
    acc_ref[...] = jnp.zeros(acc_ref.shape, F32)
    own_ks = pl.ds(pl.multiple_of(own * T, T), T)

    heads = range(GROUP_HEADS)
    pair_sl = [slice((hh // 2) * SEG, (hh // 2 + 1) * SEG) for hh in heads]

    s = [jnp.where(causal, _dot_nt(k_ref[0, own_ks, pair_sl[hh]], q_ref[0, :, hh * SEG:(hh + 1) * SEG]), NEG)
         for hh in heads]
    _softmax_update_all(acc_ref, m_ref, heads, s, [_values_t(vt_ref, own, 1, hh) for hh in heads])

    g = []
    for hh in heads:
        head_lanes = lo_half if hh % 2 == 0 else jnp.logical_not(lo_half)
        qh_hi, qh_lo = _split_bf16(jnp.where(head_lanes, q32_ref[0, :, pair_sl[hh]], 0.0))
        km_hi, km_lo = _split_bf16(kmean_ref[0, :, pair_sl[hh]])
        gate = _dot_nt(km_hi, qh_hi) + (_dot_nt(km_lo, qh_hi) + _dot_nt(km_hi, qh_lo))
        g.append(jnp.where(blk < own_f, gate, NEG))
    sel = [jnp.zeros((nbr, T), F32) for _ in heads]
    for t in range(MOBA_TOPK):
        mx = [jnp.max(gg, axis=0, keepdims=True) for gg in g]
        first = [jnp.min(jnp.where(gg == m, blk, float(nbr)), axis=0, keepdims=True)
                 for gg, m in zip(g, mx)]
        hit = [blk == f for f in first]
        sel = [jnp.where(jnp.logical_and(h, t < own), 1.0, sl) for h, sl in zip(hit, sel)]
        g = [jnp.where(h, -jnp.inf, gg) for h, gg in zip(hit, g)]
    for hh in heads:
        neg_t = jnp.concatenate([(1.0 - sel[hh]) * NEG, jnp.zeros((LANES - nbr, T), F32)], axis=0)
        qaug_ref[hh, :, 0:SEG] = q_ref[0, :, hh * SEG:(hh + 1) * SEG]
        qaug_ref[hh, :, SEG:2 * SEG] = neg_t.T.astype(BF16)

    def scores_pair(pair, n0, nblk):
        ks = pl.ds(pl.multiple_of(n0 * T, T), nblk * T)
        onehot = jnp.concatenate([jnp.where(kblk == n0 + r, 1.0, 0.0) for r in range(nblk)], axis=0).astype(BF16)
        k_aug = jnp.concatenate([k_ref[0, ks, pair * SEG:(pair + 1) * SEG], onehot], axis=1)
        return [_dot_nt(k_aug, qaug_ref[2 * pair + n]) for n in range(2)]

    def step(n0, nblk):
        s = scores_pair(0, n0, nblk) + scores_pair(1, n0, nblk)
        _softmax_update_all(acc_ref, m_ref, heads, s, [_values_t(vt_ref, n0, nblk, hh) for hh in heads])

    _softmax_pipelined(own // MLA_WIDE, lambda pair, blk: scores_pair(pair, blk, MLA_WIDE),
                       vt_ref, acc_ref, m_ref, s_ref, p_ref, a_ref)
    for r in range(MLA_WIDE - 1):
        @pl.when(own % MLA_WIDE > r)
        def _(r=r):
            step(own // MLA_WIDE * MLA_WIDE + r, 1)
    _softmax_finish(acc_ref, o_ref)


def _moba_attn(q, q32, k, vt, kmean):
    B, S, _ = k.shape
    T = MOBA_BLOCK
    assert S // T <= LANES
    nbr = -(-(S // T) // 16) * 16
    kmean = jnp.pad(kmean.reshape(B, S // T, GROUP_WIDTH), ((0, 0), (0, nbr - S // T), (0, 0)))
    return pl.pallas_call(
        _moba_kernel,
        grid=(B, S // T),
        in_specs=[pl.BlockSpec((1, T, Q_PAD), lambda b, i: (b, i, 0)),
                  pl.BlockSpec((1, T, GROUP_WIDTH), lambda b, i: (b, i, 0)),
                  pl.BlockSpec((1, S, GROUP_WIDTH), lambda b, i: (b, 0, 0)),
                  pl.BlockSpec((1,) + vt.shape[1:], lambda b, i: (b, 0, 0, 0)),
                  pl.BlockSpec((1, nbr, GROUP_WIDTH), lambda b, i: (b, 0, 0))],
        out_specs=pl.BlockSpec((1, T, GROUP_WIDTH), lambda b, i: (b, i, 0)),
        out_shape=jax.ShapeDtypeStruct((B, S, GROUP_WIDTH), BF16),
        scratch_shapes=_softmax_scratch(T) + [pltpu.VMEM((GROUP_HEADS, T, 2 * SEG), BF16)],
        compiler_params=_cparams(("parallel", "arbitrary")),
        name="moba_attn",
    )(q, q32, k, vt, kmean)


def _mix_groups(x, group_refs, gn_ref, wo_ref):
    for g, ref in enumerate(group_refs):
        o = ref[0].astype(F32)
        ms = jnp.mean(o * o, axis=-1, keepdims=True)
        n = (o * lax.rsqrt(ms + EPS) * gn_ref[:, g * GROUP_WIDTH:(g + 1) * GROUP_WIDTH]).astype(BF16)
        x = x + _dot(n, wo_ref[g * GROUP_WIDTH:(g + 1) * GROUP_WIDTH, :])
    return x


def _ffn_kernel(x_ref, oa_ref, ob_ref, oc_ref, od_ref, p_ref, gn_ref, wo_ref,
                fn_ref, wup_ref, cw_ref, cb_ref, wdn_ref, pproj_ref, pgate_ref, y_ref,
                halo_ref, ubuf0_ref, ubuf1_ref, h_ref):
    ts = x_ref.shape[1]
    fc2 = wup_ref.shape[2]
    fc = fc2 // 2
    n_chunks = wup_ref.shape[0]
    i = pl.program_id(1)

    @pl.when(i == 0)
    def _():
        halo_ref[...] = jnp.zeros(halo_ref.shape, F32)

    x = _mix_groups(x_ref[0], (oa_ref, ob_ref, oc_ref, od_ref), gn_ref, wo_ref)
    ms = jnp.mean(x * x, axis=-1, keepdims=True)
    h_ref[...] = (x * lax.rsqrt(ms + EPS) * fn_ref[...]).astype(BF16)
    y_ref[0] = x

    def up(c, ubuf_ref):
        u = _dot(h_ref[...], wup_ref[c])
        ubuf_ref[0:HALO, :] = halo_ref[c]
        ubuf_ref[HALO:HALO + ts, :] = u
        halo_ref[c] = u[ts - HALO:ts, :]

    def down(c, ubuf_ref):
        cw = cw_ref[c]
        y = (cw[0:1, :] * ubuf_ref[HALO - 2:HALO - 2 + ts, :] + cw[1:2, :] * ubuf_ref[HALO - 1:HALO - 1 + ts, :]
             + cw[2:3, :] * ubuf_ref[HALO:HALO + ts, :] + cb_ref[c])
        ya = y[:, :fc]
        g = ya * (1.0 / (1.0 + jnp.exp(-ya))) * y[:, fc:]
        return _dot(g.astype(BF16), wdn_ref[c])

    up(0, ubuf0_ref)

    def pair(cc, carry):
        up(2 * cc + 1, ubuf1_ref)
        d0 = down(2 * cc, ubuf0_ref)
        up(2 * cc + 2, ubuf0_ref)
        y_ref[0] += d0 + down(2 * cc + 1, ubuf1_ref)
        return carry

    assert n_chunks % 2 == 1
    lax.fori_loop(0, n_chunks // 2, pair, 0)
    x2 = y_ref[0] + down(n_chunks - 1, ubuf0_ref)
    gate = _dot(x2.astype(BF16), pgate_ref[...])
    emb = _dot(p_ref[0, 0].astype(BF16), pproj_ref[...])
    y_ref[0] = x2 + emb * (1.0 / (1.0 + jnp.exp(-gate)))


def _ffn_weights(ffn_norm, w_up, conv_w, conv_b, w_down, ple_proj, ple_gate):
    fc = FFN_FC
    nc = D_FF // fc
    D = w_up.shape[0]
    pair = lambda a: jnp.concatenate([a[..., :D_FF].reshape(a.shape[:-1] + (nc, fc)),
                                      a[..., D_FF:].reshape(a.shape[:-1] + (nc, fc))], axis=-1)
    wup = jnp.moveaxis(pair(w_up), 1, 0).astype(BF16)
    cw = jnp.pad(jnp.moveaxis(pair(conv_w), 1, 0), ((0, 0), (0, 8 - CONV_WIDTH), (0, 0)))
    cb = jnp.moveaxis(pair(conv_b.reshape(1, -1)), 1, 0)
    wdn = w_down.reshape(nc, fc, D).astype(BF16)
    return ffn_norm.reshape(1, D), wup, cw, cb, wdn, ple_proj.astype(BF16), ple_gate.astype(BF16)


def _ffn(x, groups, p, layer, group_norm, w_o, weights):
    B, S, D = x.shape
    ts = FFN_TS
    fn, wup, cw, cb, wdn, pproj, pgate = weights
    consts = (group_norm.reshape(1, D), w_o) + tuple(weights)
    nc, _, fc2 = wup.shape
    tok = lambda w: pl.BlockSpec((1, ts, w), lambda b, i: (b, i, 0))
    once = lambda a: pl.BlockSpec(a.shape, lambda *_: (0,) * a.ndim, pipeline_mode=pl.Buffered(1))
    return pl.pallas_call(
        _ffn_kernel,
        grid=(B, S // ts),
        in_specs=([tok(D)] + [tok(GROUP_WIDTH)] * 4
                  + [pl.BlockSpec((1, 1, ts, PLE_DIM), lambda b, i: (layer, b, i, 0))] + [once(a) for a in consts]),
        out_specs=tok(D),
        out_shape=jax.ShapeDtypeStruct((B, S, D), F32),
        scratch_shapes=[pltpu.VMEM((nc, HALO, fc2), F32), pltpu.VMEM((ts + HALO, fc2), F32),
                        pltpu.VMEM((ts + HALO, fc2), F32), pltpu.VMEM((ts, D), BF16)],
        compiler_params=_cparams(("arbitrary", "arbitrary")),
        name="ffn_ple",
    )(x, *groups, p, *consts)


def kernel(x, p, positions, attn_norm, w_in, mla_q_norm, mla_w_uq, mla_kv_norm, mla_w_ukv, mla_q_gain, mla_k_gain, swa_q_gain, swa_k_gain, swa_sinks, moba_q_gain, moba_k_gain, group_norm, w_o, ffn_norm, w_up, conv_w, conv_b, w_down, ple_proj, ple_gate):
    B, S, D = x.shape
    depth = w_in.shape[0]
    assert D == D_MODEL and S % 512 == 0
    tables = _rope_tables(positions)
    for i in range(depth):
        pw = _prep_weights(w_in[i], mla_q_norm[i], mla_w_uq[i], mla_kv_norm[i], mla_w_ukv[i], mla_q_gain[i],
                           mla_k_gain[i], swa_q_gain[i], swa_k_gain[i], moba_q_gain[i], moba_k_gain[i])
        (qa, ka, va, qm, km, vm, qc, kc, vc, qd, qd32, kd, vd, kmean) = _prep(x, attn_norm[i], tables, pw)
        o_a = _sb_attn(qa, ka, va)
        o_b = _mla_attn(qm, km, vm)
        o_c = _swa_attn(qc, kc, vc, swa_sinks[i])
        o_d = _moba_attn(qd, qd32, kd, vd, kmean)
        fw = _ffn_weights(ffn_norm[i], w_up[i], conv_w[i], conv_b[i], w_down[i], ple_proj[i], ple_gate[i])
        x = _ffn(x, (o_a, o_b, o_c, o_d), p, i, group_norm[i], w_o[i].astype(BF16), fw)
    return x
```

```python
import functools

import jax
import jax.numpy as jnp
from jax import lax
from jax.experimental import pallas as pl
from jax.experimental.pallas import tpu as pltpu

F32 = jnp.float32
BF16 = jnp.bfloat16

D_MODEL = 1024
HEAD_DIM = 64
GROUP_HEADS = 4
GROUP_WIDTH = GROUP_HEADS * HEAD_DIM
N_GROUPS = 4
ROPE_THETA = 10000.0
EPS = 1e-6
NEG = -1e30
LOG2E = 1.4426950408889634

MLA_Q_RANK = 192
MLA_KV_RANK = 128
MLA_NOPE = 64
MLA_ROPE = 32
MLA_V = 64
MLA_QK = MLA_NOPE + MLA_ROPE
SWA_KV_HEADS = 2
WINDOW = 128
MOBA_BLOCK = 256
MOBA_TOPK = 3
D_FF = 2816
CONV_WIDTH = 3
PLE_DIM = 256

SB_COLS = 3 * GROUP_WIDTH
MLA_COLS = MLA_Q_RANK + MLA_KV_RANK + MLA_ROPE
SWA_COLS = GROUP_WIDTH + 2 * SWA_KV_HEADS * HEAD_DIM
MOBA_COLS = 3 * GROUP_WIDTH
OFF_SB = 0
OFF_MLA = OFF_SB + SB_COLS
OFF_SWA = OFF_MLA + MLA_COLS
OFF_MOBA = OFF_SWA + SWA_COLS

LANES = 128
SEG = LANES
Q_PAD = GROUP_HEADS * SEG
N_IN_PAD = 2560
VMEM_LIMIT = 56 * 1024 * 1024

PREP_TS = 512
FFN_TS = 512
FFN_FC = 256
SWA_T = 512
SB_T = 128
SB_SUB = 4
SB_EXIT = -45.0
MLA_WIDE = 2
MLA_T = 512
HALO = 8
ONES_ROWS = 16


def _dot(a, b):
    return jnp.dot(a, b, preferred_element_type=F32)


def _dot_nt(a, b):
    return lax.dot_general(a, b, (((1,), (1,)), ((), ())), preferred_element_type=F32)


def _split_bf16(x):
    hi = x.astype(BF16)
    lo = (x - hi.astype(F32)).astype(BF16)
    return hi, lo


def _cparams(sem):
    return pltpu.CompilerParams(dimension_semantics=sem, vmem_limit_bytes=VMEM_LIMIT)


def _const_spec(shape):
    nd = len(shape)
    return pl.BlockSpec(shape, lambda *_: (0,) * nd)


def _rope_tables_kernel(pos_ref, inv_ref, c64_ref, s64_ref, cm_ref, sm_ref):
    pos = pos_ref[0].astype(F32)
    lane = lax.broadcasted_iota(jnp.int32, (1, LANES), 1)
    h64, hm = HEAD_DIM // 2, MLA_ROPE // 2
    ang = pos * inv_ref[0:1, :]
    c, s = jnp.cos(ang), jnp.sin(ang)
    c0, s0 = jnp.where(lane < h64, c, 0.0), jnp.where(lane < h64, s, 0.0)
    spread = lambda v: v + pltpu.roll(v, h64, 1) + pltpu.roll(v, 2 * h64, 1) + pltpu.roll(v, 3 * h64, 1)
    first = (lane & (HEAD_DIM - 1)) < h64
    c64_ref[0] = spread(c0)
    s64_ref[0] = jnp.where(first, -spread(s0), spread(s0))
    in_m = (lane >= h64) & (lane < h64 + hm)
    ca, sa = jnp.where(in_m, c, 0.0), jnp.where(in_m, s, 0.0)
    lo_m = (lane >= MLA_NOPE) & (lane < MLA_NOPE + hm)
    hi_m = (lane >= MLA_NOPE + hm) & (lane < MLA_QK)
    to_lo = lambda v: pltpu.roll(v, MLA_NOPE - h64, 1)
    to_hi = lambda v: pltpu.roll(v, MLA_NOPE - h64 + hm, 1)
    cm_ref[0] = jnp.where(lo_m, to_lo(ca), jnp.where(hi_m, to_hi(ca), 1.0))
    sm_ref[0] = jnp.where(lo_m, -to_lo(sa), jnp.where(hi_m, to_hi(sa), 0.0))


def _rope_tables(positions):
    B, S = positions.shape
    ts = 512
    half64 = HEAD_DIM // 2
    halfm = MLA_ROPE // 2
    inv64 = ROPE_THETA ** (-jnp.arange(half64, dtype=F32) / half64)
    invm = ROPE_THETA ** (-jnp.arange(halfm, dtype=F32) / halfm)
    row0 = jnp.concatenate([inv64, invm, jnp.zeros((LANES - half64 - halfm,), F32)])
    inv = jnp.zeros((8, LANES), F32).at[0].set(row0)
    tab = jax.ShapeDtypeStruct((B, S, LANES), F32)
    spec = pl.BlockSpec((1, ts, LANES), lambda b, i: (b, i, 0))
    return pl.pallas_call(
        _rope_tables_kernel,
        grid=(B, S // ts),
        in_specs=[pl.BlockSpec((1, ts, 1), lambda b, i: (b, i, 0)), _const_spec((8, LANES))],
        out_specs=[spec] * 4,
        out_shape=[tab] * 4,
        compiler_params=_cparams(("parallel", "parallel")),
        name="rope_tables",
    )(positions.reshape(B, S, 1), inv)


def _rope64(x, cos, sin_signed, lane):
    first = (lane & (HEAD_DIM - 1)) < (HEAD_DIM // 2)
    partner = jnp.where(first, pltpu.roll(x, LANES - HEAD_DIM // 2, 1), pltpu.roll(x, HEAD_DIM // 2, 1))
    return x * cos + partner * sin_signed


def _rope_mla(x, cos, sin_signed, lane):
    first = lane < MLA_NOPE + MLA_ROPE // 2
    partner = jnp.where(first, pltpu.roll(x, LANES - MLA_ROPE // 2, 1), pltpu.roll(x, MLA_ROPE // 2, 1))
    return x * cos + partner * sin_signed


def _ms64(x, bsel):
    return _dot(jnp.concatenate(_split_bf16(x * x), axis=1), bsel) * (1.0 / HEAD_DIM)


def _prep_kernel(x_ref, an_ref, win_ref, c64_ref, s64_ref, cm_ref, sm_ref,
                 qn_ref, wuq_ref, kvn_ref, wukv_ref, qg_ref, kg_ref, g64_ref,
                 qa_ref, ka_ref, va_ref, qm_ref, km_ref, vm_ref, qc_ref, kc_ref, vc_ref,
                 qd_ref, qd32_ref, kd_ref, vd_ref, kmean_ref):
    ts = x_ref.shape[1]
    subs = [slice(r0, r0 + MOBA_BLOCK) for r0 in range(0, ts, MOBA_BLOCK)]
    projs = []
    for rows in subs:
        x = x_ref[0, rows, :]
        ms = jnp.mean(x * x, axis=-1, keepdims=True)
        h = (x * lax.rsqrt(ms + EPS) * an_ref[...]).astype(BF16)
        projs.append(_dot(h, win_ref[...]))

    lane = lax.broadcasted_iota(jnp.int32, (1, LANES), 1)
    lo_half = lane < HEAD_DIM
    r = (lax.broadcasted_iota(jnp.int32, (2 * LANES, LANES), 0) & (LANES - 1)) // HEAD_DIM
    c = lax.broadcasted_iota(jnp.int32, (2 * LANES, LANES), 1) // HEAD_DIM
    bsel = jnp.where(r == c, 1.0, 0.0).astype(BF16)
    scale64 = HEAD_DIM ** -0.5
    scale_m = MLA_QK ** -0.5 * LOG2E
    for rows, proj in zip(subs, projs):
        _prep_finish(proj, rows, lane, lo_half, bsel, scale64, scale_m, c64_ref, s64_ref, cm_ref, sm_ref,
                     qn_ref, wuq_ref, kvn_ref, wukv_ref, qg_ref, kg_ref, g64_ref,
                     qa_ref, ka_ref, va_ref, qm_ref, km_ref, vm_ref, qc_ref, kc_ref, vc_ref,
                     qd_ref, qd32_ref, kd_ref, vd_ref, kmean_ref)


def _prep_finish(proj, rows, lane, lo_half, bsel, scale64, scale_m, c64_ref, s64_ref, cm_ref, sm_ref,
                 qn_ref, wuq_ref, kvn_ref, wukv_ref, qg_ref, kg_ref, g64_ref,
                 qa_ref, ka_ref, va_ref, qm_ref, km_ref, vm_ref, qc_ref, kc_ref, vc_ref,
                 qd_ref, qd32_ref, kd_ref, vd_ref, kmean_ref):
    blk = rows.start // MOBA_BLOCK
    c64, s64, cm, sm = c64_ref[0, rows, :], s64_ref[0, rows, :], cm_ref[0, rows, :], sm_ref[0, rows, :]

    def seg(off, j):
        return proj[:, off + j * SEG: off + (j + 1) * SEG]

    def store_pair(ref, val, seg_lo, seg_hi):
        ref[0, rows, seg_lo * SEG:(seg_lo + 1) * SEG] = jnp.where(lo_half, val, 0.0).astype(BF16)
        ref[0, rows, seg_hi * SEG:(seg_hi + 1) * SEG] = jnp.where(lo_half, 0.0, val).astype(BF16)

    for p in range(2):
        store_pair(qa_ref, seg(0, p) * scale64, 2 * p, 2 * p + 1)
    ka_ref[0, rows, :] = proj[:, 256:512].astype(BF16)
    va_ref[0, rows, :] = proj[:, 512:768].astype(BF16)

    cq = proj[:, 768:1024]
    cqn = (cq * lax.rsqrt(jnp.sum(cq * cq, axis=-1, keepdims=True) * (1.0 / MLA_Q_RANK) + EPS)
           * qn_ref[...]).astype(BF16)
    qm_raw = _dot(cqn, wuq_ref[...])
    ckv = proj[:, 1024:1152]
    ckvn = (ckv * lax.rsqrt(jnp.mean(ckv * ckv, axis=-1, keepdims=True) + EPS) * kvn_ref[...]).astype(BF16)
    kv = _dot(ckvn, wukv_ref[...])
    kpe = proj[:, 1152:1280]

    def store_transposed(ref, v):
        ref[0, blk] = v.T.astype(BF16)

    store_transposed(vm_ref, kv[:, 512:768])
    for hh in range(GROUP_HEADS):
        sl = slice(hh * SEG, (hh + 1) * SEG)
        qs = qm_raw[:, sl]
        qs = qs * lax.rsqrt(jnp.sum(qs * qs, axis=-1, keepdims=True) * (1.0 / MLA_QK) + EPS) * qg_ref[...]
        qm_ref[0, rows, sl] = (_rope_mla(qs, cm, sm, lane) * scale_m).astype(BF16)
        ks = kv[:, sl] + kpe
        ks = ks * lax.rsqrt(jnp.sum(ks * ks, axis=-1, keepdims=True) * (1.0 / MLA_QK) + EPS) * kg_ref[...]
        km_ref[0, rows, sl] = _rope_mla(ks, cm, sm, lane).astype(BF16)

    def norm_rope64(v, gain_row):
        v = v * lax.rsqrt(_ms64(v, bsel) + EPS) * g64_ref[gain_row:gain_row + 1, :]
        return _rope64(v, c64, s64, lane)

    qa_pair = norm_rope64(seg(1280, 0), 0) * scale64
    qb_pair = norm_rope64(seg(1280, 1), 0) * scale64
    store_pair(qc_ref, qa_pair, 0, 2)
    store_pair(qc_ref, qb_pair, 1, 3)
    kc_ref[0, rows, :] = norm_rope64(seg(1536, 0), 1).astype(BF16)
    vc_ref[0, rows, :] = seg(1664, 0).astype(BF16)

    for p in range(2):
        qd = norm_rope64(seg(1792, p), 2)
        qd32_ref[0, rows, p * SEG:(p + 1) * SEG] = qd
        store_pair(qd_ref, qd * (scale64 * LOG2E), 2 * p, 2 * p + 1)
        kd = norm_rope64(seg(2048, p), 3)
        kd_ref[0, rows, p * SEG:(p + 1) * SEG] = kd.astype(BF16)
        kmean_ref[0, blk, :, p * SEG:(p + 1) * SEG] = jnp.mean(kd, axis=0, keepdims=True)
    store_transposed(vd_ref, proj[:, 2304:2560])


def _prep_weights(w_in, mla_q_norm, mla_w_uq, mla_kv_norm, mla_w_ukv, mla_q_gain, mla_k_gain,
                  swa_q_gain, swa_k_gain, moba_q_gain, moba_k_gain):
    D = w_in.shape[0]
    z = lambda n: jnp.zeros((D, n), F32)
    mla = w_in[:, OFF_MLA:OFF_MLA + MLA_COLS]
    swa = w_in[:, OFF_SWA:OFF_SWA + SWA_COLS]
    swa_q = swa[:, :GROUP_WIDTH].reshape(D, GROUP_HEADS, HEAD_DIM)[:, jnp.array([0, 2, 1, 3])].reshape(D, GROUP_WIDTH)
    win = jnp.concatenate([
        w_in[:, OFF_SB:OFF_SB + SB_COLS],
        mla[:, :MLA_Q_RANK], z(256 - MLA_Q_RANK),
        mla[:, MLA_Q_RANK:MLA_Q_RANK + MLA_KV_RANK],
        z(MLA_NOPE), mla[:, MLA_Q_RANK + MLA_KV_RANK:], z(SEG - MLA_QK),
        swa_q, swa[:, GROUP_WIDTH:],
        w_in[:, OFF_MOBA:OFF_MOBA + MOBA_COLS],
    ], axis=1).astype(BF16)
    assert win.shape[1] == N_IN_PAD
    qn = jnp.pad(mla_q_norm, (0, 256 - MLA_Q_RANK)).reshape(1, 256)
    wuq = mla_w_uq.reshape(MLA_Q_RANK, GROUP_HEADS, MLA_QK)
    wuq = jnp.pad(wuq, ((0, 256 - MLA_Q_RANK), (0, 0), (0, SEG - MLA_QK))).reshape(256, Q_PAD).astype(BF16)
    wukv = mla_w_ukv.reshape(MLA_KV_RANK, GROUP_HEADS, MLA_NOPE + MLA_V)
    wkn = jnp.pad(wukv[:, :, :MLA_NOPE], ((0, 0), (0, 0), (0, SEG - MLA_NOPE))).reshape(MLA_KV_RANK, Q_PAD)
    wv = wukv[:, :, MLA_NOPE:].reshape(MLA_KV_RANK, GROUP_WIDTH)
    wukv_r = jnp.concatenate([wkn, wv], axis=1).astype(BF16)
    qg = jnp.pad(mla_q_gain, (0, SEG - MLA_QK)).reshape(1, SEG)
    kg = jnp.pad(mla_k_gain, (0, SEG - MLA_QK)).reshape(1, SEG)
    g64 = jnp.zeros((8, LANES), F32)
    for row, g in enumerate((swa_q_gain, swa_k_gain, moba_q_gain, moba_k_gain)):
        g64 = g64.at[row].set(jnp.tile(g, LANES // HEAD_DIM))
    return win, qn, wuq, mla_kv_norm.reshape(1, MLA_KV_RANK), wukv_r, qg, kg, g64


def _prep(x, attn_norm, tables, weights):
    B, S, D = x.shape
    ts = PREP_TS
    win, qn, wuq, kvn, wukv, qg, kg, g64 = weights
    tok = lambda w: pl.BlockSpec((1, ts, w), lambda b, i: (b, i, 0))
    nb_t = ts // MOBA_BLOCK
    VT = "channel-major value blocks"
    out_widths = [(Q_PAD, BF16), (256, BF16), (256, BF16),
                  (Q_PAD, BF16), (Q_PAD, BF16), (VT, BF16),
                  (Q_PAD, BF16), (128, BF16), (128, BF16),
                  (Q_PAD, BF16), (256, F32), (256, BF16), (VT, BF16)]
    vt_shape = jax.ShapeDtypeStruct((B, S // MOBA_BLOCK, GROUP_WIDTH, MOBA_BLOCK), BF16)
    vt_spec = pl.BlockSpec((1, nb_t, GROUP_WIDTH, MOBA_BLOCK), lambda b, i: (b, i, 0, 0))
    out_shape = [vt_shape if w is VT else jax.ShapeDtypeStruct((B, S, w), dt) for w, dt in out_widths]
    out_specs = [vt_spec if w is VT else tok(w) for w, _ in out_widths]
    out_shape.append(jax.ShapeDtypeStruct((B, S // MOBA_BLOCK, 1, 256), F32))
    out_specs.append(pl.BlockSpec((1, nb_t, 1, 256), lambda b, i: (b, i, 0, 0)))
    consts = [attn_norm.reshape(1, D), win]
    tail = [qn, wuq, kvn, wukv, qg, kg, g64]
    in_specs = ([tok(D)] + [_const_spec(a.shape) for a in consts] + [tok(LANES)] * 4
                + [_const_spec(a.shape) for a in tail])
    return pl.pallas_call(
        _prep_kernel,
        grid=(B, S // ts),
        in_specs=in_specs,
        out_specs=out_specs,
        out_shape=out_shape,
        compiler_params=_cparams(("parallel", "parallel")),
        name="prep",
    )(x, *consts, *tables, *tail)


def _sb_kernel(q_ref, k_ref, v_ref, o_ref, acc_ref, carry_ref):
    T = SB_T
    i = pl.program_id(1)
    row = lax.broadcasted_iota(jnp.int32, (T, T), 0)
    col = lax.broadcasted_iota(jnp.int32, (T, T), 1)
    upper = jnp.concatenate([jnp.where(row > col, 1.0, 0.0), jnp.ones((T, T), F32)], axis=1).astype(BF16)
    upper = jnp.concatenate([upper, upper], axis=0)
    causal = col < row
    lane = lax.broadcasted_iota(jnp.int32, (1, LANES), 1)

    n_sub = q_ref.shape[1] // T
    chains = [(sub, hh) for sub in range(n_sub) for hh in range(GROUP_HEADS)]

    def block(t, first):
        kb = [i * n_sub + sub - t for sub in range(n_sub)]
        ks = [pl.ds(pl.multiple_of(jnp.maximum(b, 0) * T, T), T) for b in kb]
        kv = lambda ref, sub, hh: ref[0, ks[sub], (hh // 2) * SEG:(hh // 2 + 1) * SEG]
        z = [_dot_nt(q_ref[0, sub * T:(sub + 1) * T, hh * SEG:(hh + 1) * SEG], kv(k_ref, sub, hh))
             for sub, hh in chains]
        log_beta = [jnp.minimum(zz, 0.0) - jnp.log(1.0 + jnp.exp(-jnp.abs(zz))) for zz in z]
        log_keep = [lb - zz for lb, zz in zip(log_beta, z)]
        if first:
            log_keep = [jnp.where(causal, lk, 0.0) for lk in log_keep]
        parts = [jnp.concatenate(_split_bf16(lk), axis=1) for lk in log_keep]
        sums = [_dot(hl, upper) for hl in parts]
        if first:
            w = [jnp.where(causal, jnp.exp(lb + sm[:, :T]), 0.0) for lb, sm in zip(log_beta, sums)]
            carry = [sm[:, T:] for sm in sums]
        else:
            live = [kb[sub] >= 0 for sub, hh in chains]
            old = [carry_ref[c] for c in range(len(chains))]
            w = [jnp.where(lv, jnp.exp(lb + (sm[:, :T] + cr)), 0.0) for lv, lb, sm, cr in zip(live, log_beta, sums, old)]
            carry = [jnp.where(lv, cr + sm[:, T:], NEG) for lv, sm, cr in zip(live, sums, old)]
        pv = [_dot(ww.astype(BF16), kv(v_ref, sub, hh)) for (sub, hh), ww in zip(chains, w)]
        for c in range(len(chains)):
            acc_ref[c] = pv[c] if first else acc_ref[c] + pv[c]
            carry_ref[c] = carry[c]
        top = carry[0]
        for cr in carry[1:]:
            top = jnp.maximum(top, cr)
        return jnp.max(top)

    def cond(st):
        t, mx = st
        return jnp.logical_and(t <= i * n_sub + n_sub - 1, mx > SB_EXIT)

    def body(st):
        return st[0] + 1, block(st[0], False)

    lax.while_loop(cond, body, (1, block(0, True)))

    lo_half = lane < HEAD_DIM
    for sub in range(n_sub):
        for p in range(2):
            c = sub * GROUP_HEADS + 2 * p
            o_ref[0, sub * T:(sub + 1) * T, p * SEG:(p + 1) * SEG] = jnp.where(
                lo_half, acc_ref[c], acc_ref[c + 1]).astype(o_ref.dtype)


def _sb_attn(q, k, v):
    B, S, _ = k.shape
    T = SB_T
    rows = SB_SUB * T
    n_chains = SB_SUB * GROUP_HEADS
    full = pl.BlockSpec((1, S, GROUP_WIDTH), lambda b, i: (b, 0, 0))
    return pl.pallas_call(
        _sb_kernel,
        grid=(B, S // rows),
        in_specs=[pl.BlockSpec((1, rows, Q_PAD), lambda b, i: (b, i, 0)), full, full],
        out_specs=pl.BlockSpec((1, rows, GROUP_WIDTH), lambda b, i: (b, i, 0)),
        out_shape=jax.ShapeDtypeStruct((B, S, GROUP_WIDTH), BF16),
        scratch_shapes=[pltpu.VMEM((n_chains, T, SEG), F32), pltpu.VMEM((n_chains, T, T), F32)],
        compiler_params=_cparams(("parallel", "arbitrary")),
        name="sb_attn",
    )(q, k, v)


def _softmax_update_all(acc_ref, m_ref, heads, scores, vt_augs):
    m_old = [m_ref[hh][0:1, :] for hh in heads]
    acc_old = [acc_ref[hh] for hh in heads]
    m_new = [jnp.maximum(mo, jnp.max(s, axis=0, keepdims=True)) for mo, s in zip(m_old, scores)]
    p = [jnp.exp2(s - mn).astype(BF16) for s, mn in zip(scores, m_new)]
    pv = [_dot(va, pp) for va, pp in zip(vt_augs, p)]
    for hh, mo, mn, r, a in zip(heads, m_old, m_new, pv, acc_old):
        acc_ref[hh] = jnp.exp2(mo - mn) * a + r
        m_ref[hh] = jnp.broadcast_to(mn, m_ref.shape[1:])


def _values_t(vt_ref, first_blk, nblk, hh):
    rows = slice(hh * HEAD_DIM, (hh + 1) * HEAD_DIM)
    v = jnp.concatenate([vt_ref[0, first_blk + r, rows, :] for r in range(nblk)], axis=1)
    return jnp.concatenate([v, jnp.ones((ONES_ROWS, v.shape[1]), v.dtype)], axis=0)


def _softmax_pipelined(n_steps, qk_pair, vt_ref, acc_ref, m_ref, s_ref, p_ref, a_ref):
    pairs = ((0, 1), (2, 3))
    s_ref[0] = jnp.full(s_ref.shape[1:], -jnp.inf, F32)
    p_ref[0] = jnp.zeros(p_ref.shape[1:], BF16)
    a_ref[0] = jnp.ones(a_ref.shape[1:], F32)

    def softmax(m_old, s):
        m_new = jnp.maximum(m_old, jnp.max(s, axis=0, keepdims=True))
        return jnp.exp2(s - m_new).astype(BF16), jnp.exp2(m_old - m_new), m_new

    def half_step(j, rd, wr, with_scores):
        prev = jnp.maximum(j - 1, 0) * MLA_WIDE
        m_old = [m_ref[hh][0:1, :] for hh in range(GROUP_HEADS)]
        acc_old = [acc_ref[hh] for hh in range(GROUP_HEADS)]
        s1_prev = [s_ref[rd, n] for n in range(2)]
        p0_prev = [p_ref[rd, n] for n in range(2)]
        a0_prev = [a_ref[rd, n][0:1, :] for n in range(2)]
        vt = [_values_t(vt_ref, prev, MLA_WIDE, hh) for hh in range(GROUP_HEADS)]

        acc = [a0_prev[n] * acc_old[hh] + _dot(vt[hh], p0_prev[n]) for n, hh in enumerate(pairs[0])]
        s0 = qk_pair(0, j * MLA_WIDE) if with_scores else None
        r1 = [softmax(m_old[hh], s1_prev[n]) for n, hh in enumerate(pairs[1])]
        s1 = qk_pair(1, j * MLA_WIDE) if with_scores else None
        acc += [r1[n][1] * acc_old[hh] + _dot(vt[hh], r1[n][0]) for n, hh in enumerate(pairs[1])]
        m_new = [None, None, r1[0][2], r1[1][2]]
        if with_scores:
            r0 = [softmax(m_old[hh], s0[n]) for n, hh in enumerate(pairs[0])]
            m_new[0:2] = [r0[0][2], r0[1][2]]

        for hh in range(GROUP_HEADS):
            acc_ref[hh] = acc[hh]
            if m_new[hh] is not None:
                m_ref[hh] = jnp.broadcast_to(m_new[hh], m_ref.shape[1:])
        if with_scores:
            for n in range(2):
                p_ref[wr, n] = r0[n][0]
                a_ref[wr, n] = jnp.broadcast_to(r0[n][1], a_ref.shape[2:])
                s_ref[wr, n] = s1[n]

    def body(u, carry):
        half_step(2 * u, 0, 1, True)
        half_step(2 * u + 1, 1, 0, True)
        return carry

    lax.fori_loop(0, n_steps // 2, body, 0)

    @pl.when(n_steps % 2 == 1)
    def _():
        half_step(n_steps - 1, 0, 1, True)
        half_step(n_steps, 1, None, False)

    @pl.when(n_steps % 2 == 0)
    def _():
        half_step(n_steps, 0, None, False)


def _softmax_finish(acc_ref, o_ref):
    for p in range(2):
        halves = [acc_ref[hh][0:HEAD_DIM, :] / acc_ref[hh][HEAD_DIM:HEAD_DIM + 1, :] for hh in (2 * p, 2 * p + 1)]
        o_ref[0, :, p * SEG:(p + 1) * SEG] = jnp.concatenate(halves, axis=0).T.astype(o_ref.dtype)


def _mla_kernel(q_ref, k_ref, vt_ref, o_ref, acc_ref, m_ref, s_ref, p_ref, a_ref):
    T = MLA_T
    i = pl.program_id(1)
    key = lax.broadcasted_iota(jnp.int32, (T, T), 0)
    qry = lax.broadcasted_iota(jnp.int32, (T, T), 1)
    causal = key <= qry
    heads = range(GROUP_HEADS)

    def scores(hh, first_blk):
        ks = pl.ds(pl.multiple_of(first_blk * MOBA_BLOCK, MOBA_BLOCK), T)
        return _dot_nt(k_ref[0, ks, hh * SEG:(hh + 1) * SEG], q_ref[0, :, hh * SEG:(hh + 1) * SEG])

    m_ref[...] = jnp.full(m_ref.shape, NEG, F32)
    acc_ref[...] = jnp.zeros(acc_ref.shape, F32)
    _softmax_pipelined(i, lambda pair, blk: [scores(2 * pair + n, blk) for n in range(2)],
                       vt_ref, acc_ref, m_ref, s_ref, p_ref, a_ref)
    s = [jnp.where(causal, scores(hh, i * MLA_WIDE), NEG) for hh in heads]
    _softmax_update_all(acc_ref, m_ref, heads, s, [_values_t(vt_ref, i * MLA_WIDE, MLA_WIDE, hh) for hh in heads])
    _softmax_finish(acc_ref, o_ref)


def _softmax_scratch(T):
    tk = MLA_WIDE * MOBA_BLOCK
    assert tk == T
    return [pltpu.VMEM((GROUP_HEADS, HEAD_DIM + ONES_ROWS, T), F32), pltpu.VMEM((GROUP_HEADS, 8, T), F32),
            pltpu.VMEM((2, 2, tk, T), F32), pltpu.VMEM((2, 2, tk, T), BF16), pltpu.VMEM((2, 2, 8, T), F32)]


def _mla_attn(q, k, vt):
    B, S, _ = k.shape
    T = MLA_T
    return pl.pallas_call(
        _mla_kernel,
        grid=(B, S // T),
        in_specs=[pl.BlockSpec((1, T, Q_PAD), lambda b, i: (b, i, 0)),
                  pl.BlockSpec((1, S, Q_PAD), lambda b, i: (b, 0, 0)),
                  pl.BlockSpec((1,) + vt.shape[1:], lambda b, i: (b, 0, 0, 0))],
        out_specs=pl.BlockSpec((1, T, GROUP_WIDTH), lambda b, i: (b, i, 0)),
        out_shape=jax.ShapeDtypeStruct((B, S, GROUP_WIDTH), BF16),
        scratch_shapes=_softmax_scratch(T),
        compiler_params=_cparams(("parallel", "arbitrary")),
        name="mla_attn",
    )(q, k, vt)


def _swa_kernel(sink_ref, q_ref, k_ref, v_ref, o_ref):
    W = WINDOW
    i = pl.program_id(1)
    lane = lax.broadcasted_iota(jnp.int32, (1, LANES), 1)
    lo_half = lane < HEAD_DIM
    heads = range(GROUP_HEADS)
    sinks = [sink_ref[hh] for hh in heads]
    chains, ks, vs, bands = [], [], [], []
    for b in range(q_ref.shape[1] // W):
        q0 = i * q_ref.shape[1] + b * W
        start = pl.multiple_of(jnp.maximum(q0 - W, 0), W)
        ks.append(k_ref[0, pl.ds(start, 2 * W), :])
        vs.append(v_ref[0, pl.ds(start, 2 * W), :])
        qpos = q0 + lax.broadcasted_iota(jnp.int32, (W, 2 * W), 0)
        kpos = start + lax.broadcasted_iota(jnp.int32, (W, 2 * W), 1)
        bands.append(jnp.logical_and(kpos <= qpos, qpos - kpos < W))
        chains += [(b, hh) for hh in heads]
    s = [jnp.where(bands[b], _dot_nt(q_ref[0, b * W:(b + 1) * W, hh * SEG:(hh + 1) * SEG], ks[b]), NEG)
         for b, hh in chains]
    m = [jnp.maximum(jnp.max(ss, axis=1, keepdims=True), sinks[hh]) for ss, (b, hh) in zip(s, chains)]
    p = [jnp.exp(ss - mm) for ss, mm in zip(s, m)]
    denom = [jnp.sum(pp, axis=1, keepdims=True) + jnp.exp(sinks[hh] - mm) for pp, mm, (b, hh) in zip(p, m, chains)]
    outs = [_dot(pp.astype(BF16), vs[b]) / dd for pp, dd, (b, hh) in zip(p, denom, chains)]
    for b in range(q_ref.shape[1] // W):
        o = outs[b * GROUP_HEADS:(b + 1) * GROUP_HEADS]
        rows = slice(b * W, (b + 1) * W)
        o_ref[0, rows, 0:SEG] = jnp.where(lo_half, o[0], pltpu.roll(o[1], HEAD_DIM, 1)).astype(o_ref.dtype)
        o_ref[0, rows, SEG:2 * SEG] = jnp.where(lo_half, pltpu.roll(o[2], HEAD_DIM, 1), o[3]).astype(o_ref.dtype)


def _swa_attn(q, k, v, sinks):
    B, S, _ = k.shape
    W = SWA_T
    full = pl.BlockSpec((1, S, 2 * HEAD_DIM), lambda b, i: (b, 0, 0))
    return pl.pallas_call(
        _swa_kernel,
        grid=(B, S // W),
        in_specs=[pl.BlockSpec(memory_space=pltpu.SMEM),
                  pl.BlockSpec((1, W, Q_PAD), lambda b, i: (b, i, 0)), full, full],
        out_specs=pl.BlockSpec((1, W, GROUP_WIDTH), lambda b, i: (b, i, 0)),
        out_shape=jax.ShapeDtypeStruct((B, S, GROUP_WIDTH), BF16),
        compiler_params=_cparams(("parallel", "parallel")),
        name="swa_attn",
    )(sinks, q, k, v)


def _moba_kernel(q_ref, q32_ref, k_ref, vt_ref, kmean_ref, o_ref, acc_ref, m_ref, s_ref, p_ref, a_ref, qaug_ref):
    T = q_ref.shape[1]
    BLK = MOBA_BLOCK
    i = pl.program_id(1)
    nbr = kmean_ref.shape[1]
    key = lax.broadcasted_iota(jnp.int32, (T, T), 0)
    qry = lax.broadcasted_iota(jnp.int32, (T, T), 1)
    causal = key <= qry
    lo_half = lax.broadcasted_iota(jnp.int32, (1, LANES), 1) < HEAD_DIM
    blk = lax.broadcasted_iota(jnp.int32, (nbr, T), 0).astype(F32)
    kblk = lax.broadcasted_iota(jnp.int32, (BLK, LANES), 1)
    own = (i * MLA_WIDE + lax.broadcasted_iota(jnp.int32, (1, T), 1) // BLK).astype(F32)

    m_ref[...] = jnp.full(m_ref.shape, NEG, F32)
    acc_ref[...] = jnp.zeros(acc_ref.shape, F32)
    heads = range(GROUP_HEADS)
    pair_sl = [slice((hh // 2) * SEG, (hh // 2 + 1) * SEG) for hh in heads]

    g = []
    for hh in heads:
        head_lanes = lo_half if hh % 2 == 0 else jnp.logical_not(lo_half)
        qh_hi, qh_lo = _split_bf16(jnp.where(head_lanes, q32_ref[0, :, pair_sl[hh]], 0.0))
        km_hi, km_lo = _split_bf16(kmean_ref[0, :, pair_sl[hh]])
        gate = _dot_nt(km_hi, qh_hi) + (_dot_nt(km_lo, qh_hi) + _dot_nt(km_hi, qh_lo))
        g.append(jnp.where(blk < own, gate, NEG))
    sel = [jnp.where(blk == own, 1.0, 0.0) for _ in heads]
    for t in range(MOBA_TOPK):
        mx = [jnp.max(gg, axis=0, keepdims=True) for gg in g]
        first = [jnp.min(jnp.where(gg == m, blk, float(nbr)), axis=0, keepdims=True)
                 for gg, m in zip(g, mx)]
        hit = [blk == f for f in first]
        sel = [jnp.where(jnp.logical_and(h, t < own), 1.0, sl) for h, sl in zip(hit, sel)]
        g = [jnp.where(h, -jnp.inf, gg) for h, gg in zip(hit, g)]
    for hh in heads:
        neg_t = jnp.concatenate([(1.0 - sel[hh]) * NEG, jnp.zeros((LANES - nbr, T), F32)], axis=0)
        qaug_ref[hh, :, 0:SEG] = q_ref[0, :, hh * SEG:(hh + 1) * SEG]
        qaug_ref[hh, :, SEG:2 * SEG] = neg_t.T.astype(BF16)

    def scores_pair(pair, n0):
        ks = pl.ds(pl.multiple_of(n0 * BLK, BLK), MLA_WIDE * BLK)
        onehot = jnp.concatenate([jnp.where(kblk == n0 + r, 1.0, 0.0) for r in range(MLA_WIDE)],
                                 axis=0).astype(BF16)
        k_aug = jnp.concatenate([k_ref[0, ks, pair * SEG:(pair + 1) * SEG], onehot], axis=1)
        return [_dot_nt(k_aug, qaug_ref[2 * pair + n]) for n in range(2)]

    _softmax_pipelined(i, scores_pair, vt_ref, acc_ref, m_ref, s_ref, p_ref, a_ref)
    s = [jnp.where(causal, ss, NEG) for ss in scores_pair(0, i * MLA_WIDE) + scores_pair(1, i * MLA_WIDE)]
    _softmax_update_all(acc_ref, m_ref, heads, s, [_values_t(vt_ref, i * MLA_WIDE, MLA_WIDE, hh) for hh in heads])
    _softmax_finish(acc_ref, o_ref)


def _moba_attn(q, q32, k, vt, kmean):
    B, S, _ = k.shape
    T = MLA_T
    n_blocks = S // MOBA_BLOCK
    assert n_blocks <= LANES
    nbr = -(-n_blocks // 16) * 16
    kmean = jnp.pad(kmean.reshape(B, n_blocks, GROUP_WIDTH), ((0, 0), (0, nbr - n_blocks), (0, 0)))
    return pl.pallas_call(
        _moba_kernel,
        grid=(B, S // T),
        in_specs=[pl.BlockSpec((1, T, Q_PAD), lambda b, i: (b, i, 0)),
                  pl.BlockSpec((1, T, GROUP_WIDTH), lambda b, i: (b, i, 0)),
                  pl.BlockSpec((1, S, GROUP_WIDTH), lambda b, i: (b, 0, 0)),
                  pl.BlockSpec((1,) + vt.shape[1:], lambda b, i: (b, 0, 0, 0)),
                  pl.BlockSpec((1, nbr, GROUP_WIDTH), lambda b, i: (b, 0, 0))],
        out_specs=pl.BlockSpec((1, T, GROUP_WIDTH), lambda b, i: (b, i, 0)),
        out_shape=jax.ShapeDtypeStruct((B, S, GROUP_WIDTH), BF16),
        scratch_shapes=_softmax_scratch(T) + [pltpu.VMEM((GROUP_HEADS, T, 2 * SEG), BF16)],
        compiler_params=_cparams(("parallel", "arbitrary")),
        name="moba_attn",
    )(q, q32, k, vt, kmean)


def _mix_groups(x, group_refs, gn_ref, wo_ref):
    for g, ref in enumerate(group_refs):
        o = ref[0].astype(F32)
        ms = jnp.mean(o * o, axis=-1, keepdims=True)
        n = (o * lax.rsqrt(ms + EPS) * gn_ref[:, g * GROUP_WIDTH:(g + 1) * GROUP_WIDTH]).astype(BF16)
        x = x + _dot(n, wo_ref[g * GROUP_WIDTH:(g + 1) * GROUP_WIDTH, :])
    return x


def _ffn_kernel(x_ref, oa_ref, ob_ref, oc_ref, od_ref, p_ref, gn_ref, wo_ref,
                fn_ref, wup_ref, cw_ref, cb_ref, wdn_ref, pproj_ref, pgate_ref, y_ref,
                halo_ref, ubuf0_ref, ubuf1_ref, h_ref):
    ts = x_ref.shape[1]
    fc2 = wup_ref.shape[2]
    fc = fc2 // 2
    n_chunks = wup_ref.shape[0]
    i = pl.program_id(1)

    @pl.when(i == 0)
    def _():
        halo_ref[...] = jnp.zeros(halo_ref.shape, F32)

    x = _mix_groups(x_ref[0], (oa_ref, ob_ref, oc_ref, od_ref), gn_ref, wo_ref)
    ms = jnp.mean(x * x, axis=-1, keepdims=True)
    h_ref[...] = (x * lax.rsqrt(ms + EPS) * fn_ref[...]).astype(BF16)
    y_ref[0] = x

    def up(c, ubuf_ref):
        u = _dot(h_ref[...], wup_ref[c])
        ubuf_ref[0:HALO, :] = halo_ref[c]
        ubuf_ref[HALO:HALO + ts, :] = u
        halo_ref[c] = u[ts - HALO:ts, :]

    def down(c, ubuf_ref):
        cw = cw_ref[c]
        y = (cw[0:1, :] * ubuf_ref[HALO - 2:HALO - 2 + ts, :] + cw[1:2, :] * ubuf_ref[HALO - 1:HALO - 1 + ts, :]
             + cw[2:3, :] * ubuf_ref[HALO:HALO + ts, :] + cb_ref[c])
        ya = y[:, :fc]
        g = ya * (1.0 / (1.0 + jnp.exp(-ya))) * y[:, fc:]
        y_ref[0] += _dot(g.astype(BF16), wdn_ref[c])

    up(0, ubuf0_ref)

    def pair(cc, carry):
        up(2 * cc + 1, ubuf1_ref)
        down(2 * cc, ubuf0_ref)
        up(2 * cc + 2, ubuf0_ref)
        down(2 * cc + 1, ubuf1_ref)
        return carry

    assert n_chunks % 2 == 1
    lax.fori_loop(0, n_chunks // 2, pair, 0)
    down(n_chunks - 1, ubuf0_ref)
    x2 = y_ref[0]
    gate = _dot(x2.astype(BF16), pgate_ref[...])
    emb = _dot(p_ref[0, 0].astype(BF16), pproj_ref[...])
    y_ref[0] = x2 + emb * (1.0 / (1.0 + jnp.exp(-gate)))


def _ffn_weights(ffn_norm, w_up, conv_w, conv_b, w_down, ple_proj, ple_gate):
    fc = FFN_FC
    nc = D_FF // fc
    D = w_up.shape[0]
    pair = lambda a: jnp.concatenate([a[..., :D_FF].reshape(a.shape[:-1] + (nc, fc)),
                                      a[..., D_FF:].reshape(a.shape[:-1] + (nc, fc))], axis=-1)
    wup = jnp.moveaxis(pair(w_up), 1, 0).astype(BF16)
    cw = jnp.pad(jnp.moveaxis(pair(conv_w), 1, 0), ((0, 0), (0, 8 - CONV_WIDTH), (0, 0)))
    cb = jnp.moveaxis(pair(conv_b.reshape(1, -1)), 1, 0)
    wdn = w_down.reshape(nc, fc, D).astype(BF16)
    return ffn_norm.reshape(1, D), wup, cw, cb, wdn, ple_proj.astype(BF16), ple_gate.astype(BF16)


def _ffn(x, groups, p, layer, group_norm, w_o, weights):
    B, S, D = x.shape
    ts = FFN_TS
    fn, wup, cw, cb, wdn, pproj, pgate = weights
    consts = (group_norm.reshape(1, D), w_o) + tuple(weights)
    nc, _, fc2 = wup.shape
    tok = lambda w: pl.BlockSpec((1, ts, w), lambda b, i: (b, i, 0))
    once = lambda a: pl.BlockSpec(a.shape, lambda *_: (0,) * a.ndim, pipeline_mode=pl.Buffered(1))
    return pl.pallas_call(
        _ffn_kernel,
        grid=(B, S // ts),
        in_specs=([tok(D)] + [tok(GROUP_WIDTH)] * 4
                  + [pl.BlockSpec((1, 1, ts, PLE_DIM), lambda b, i: (layer, b, i, 0))] + [once(a) for a in consts]),
        out_specs=tok(D),
        out_shape=jax.ShapeDtypeStruct((B, S, D), F32),
        scratch_shapes=[pltpu.VMEM((nc, HALO, fc2), F32), pltpu.VMEM((ts + HALO, fc2), F32),
                        pltpu.VMEM((ts + HALO, fc2), F32), pltpu.VMEM((ts, D), BF16)],
        compiler_params=_cparams(("arbitrary", "arbitrary")),
        name="ffn_ple",
    )(x, *groups, p, *consts)


def kernel(x, p, positions, attn_norm, w_in, mla_q_norm, mla_w_uq, mla_kv_norm, mla_w_ukv, mla_q_gain, mla_k_gain, swa_q_gain, swa_k_gain, swa_sinks, moba_q_gain, moba_k_gain, group_norm, w_o, ffn_norm, w_up, conv_w, conv_b, w_down, ple_proj, ple_gate):
    B, S, D = x.shape
    depth = w_in.shape[0]
    assert D == D_MODEL and S % 512 == 0
    tables = _rope_tables(positions)
    for i in range(depth):
        pw = _prep_weights(w_in[i], mla_q_norm[i], mla_w_uq[i], mla_kv_norm[i], mla_w_ukv[i], mla_q_gain[i],
                           mla_k_gain[i], swa_q_gain[i], swa_k_gain[i], moba_q_gain[i], moba_k_gain[i])
        (qa, ka, va, qm, km, vm, qc, kc, vc, qd, qd32, kd, vd, kmean) = _prep(x, attn_norm[i], tables, pw)
        o_a = _sb_attn(qa, ka, va)
        o_b = _mla_attn(qm, km, vm)
        o_c = _swa_attn(qc, kc, vc, swa_sinks[i])
        o_d = _moba_attn(qd, qd32, kd, vd, kmean)
        fw = _ffn_weights(ffn_norm[i], w_up[i], conv_w[i], conv_b[i], w_down[i], ple_proj[i], ple_gate[i])
        x = _ffn(x, (o_a, o_b, o_c, o_d), p, i, group_norm[i], w_o[i].astype(BF16), fw)
    return x
```

```python
import functools

import jax
import jax.numpy as jnp
from jax import lax
from jax.experimental import pallas as pl
from jax.experimental.pallas import tpu as pltpu

F32 = jnp.float32
BF16 = jnp.bfloat16

D_MODEL = 1024
HEAD_DIM = 64
GROUP_HEADS = 4
GROUP_WIDTH = GROUP_HEADS * HEAD_DIM
N_GROUPS = 4
ROPE_THETA = 10000.0
EPS = 1e-6
NEG = -1e30
LOG2E = 1.4426950408889634

MLA_Q_RANK = 192
MLA_KV_RANK = 128
MLA_NOPE = 64
MLA_ROPE = 32
MLA_V = 64
MLA_QK = MLA_NOPE + MLA_ROPE
SWA_KV_HEADS = 2
WINDOW = 128
MOBA_BLOCK = 256
MOBA_TOPK = 3
D_FF = 2816
CONV_WIDTH = 3
PLE_DIM = 256

SB_COLS = 3 * GROUP_WIDTH
MLA_COLS = MLA_Q_RANK + MLA_KV_RANK + MLA_ROPE
SWA_COLS = GROUP_WIDTH + 2 * SWA_KV_HEADS * HEAD_DIM
MOBA_COLS = 3 * GROUP_WIDTH
OFF_SB = 0
OFF_MLA = OFF_SB + SB_COLS
OFF_SWA = OFF_MLA + MLA_COLS
OFF_MOBA = OFF_SWA + SWA_COLS

LANES = 128
SEG = LANES
Q_PAD = GROUP_HEADS * SEG
N_IN_PAD = 2560
VMEM_LIMIT = 56 * 1024 * 1024

PREP_TS = 512
FFN_TS = 512
FFN_FC = 256
SWA_T = 512
SB_T = 128
SB_SUB = 4
SB_EXIT = -45.0
MLA_WIDE = 2
MLA_T = 1024
HALO = 8
ONES_ROWS = 16


def _dot(a, b):
    return jnp.dot(a, b, preferred_element_type=F32)


def _dot_nt(a, b):
    return lax.dot_general(a, b, (((1,), (1,)), ((), ())), preferred_element_type=F32)


def _split_bf16(x):
    hi = x.astype(BF16)
    lo = (x - hi.astype(F32)).astype(BF16)
    return hi, lo


def _cparams(sem):
    return pltpu.CompilerParams(dimension_semantics=sem, vmem_limit_bytes=VMEM_LIMIT)


def _const_spec(shape):
    nd = len(shape)
    return pl.BlockSpec(shape, lambda *_: (0,) * nd)


def _rope_tables_kernel(pos_ref, inv_ref, c64_ref, s64_ref, cm_ref, sm_ref):
    pos = pos_ref[0].astype(F32)
    lane = lax.broadcasted_iota(jnp.int32, (1, LANES), 1)
    h64, hm = HEAD_DIM // 2, MLA_ROPE // 2
    ang = pos * inv_ref[0:1, :]
    c, s = jnp.cos(ang), jnp.sin(ang)
    c0, s0 = jnp.where(lane < h64, c, 0.0), jnp.where(lane < h64, s, 0.0)
    spread = lambda v: v + pltpu.roll(v, h64, 1) + pltpu.roll(v, 2 * h64, 1) + pltpu.roll(v, 3 * h64, 1)
    first = (lane & (HEAD_DIM - 1)) < h64
    c64_ref[0] = spread(c0)
    s64_ref[0] = jnp.where(first, -spread(s0), spread(s0))
    in_m = (lane >= h64) & (lane < h64 + hm)
    ca, sa = jnp.where(in_m, c, 0.0), jnp.where(in_m, s, 0.0)
    lo_m = (lane >= MLA_NOPE) & (lane < MLA_NOPE + hm)
    hi_m = (lane >= MLA_NOPE + hm) & (lane < MLA_QK)
    to_lo = lambda v: pltpu.roll(v, MLA_NOPE - h64, 1)
    to_hi = lambda v: pltpu.roll(v, MLA_NOPE - h64 + hm, 1)
    cm_ref[0] = jnp.where(lo_m, to_lo(ca), jnp.where(hi_m, to_hi(ca), 1.0))
    sm_ref[0] = jnp.where(lo_m, -to_lo(sa), jnp.where(hi_m, to_hi(sa), 0.0))


def _rope_tables(positions):
    B, S = positions.shape
    ts = 512
    half64 = HEAD_DIM // 2
    halfm = MLA_ROPE // 2
    inv64 = ROPE_THETA ** (-jnp.arange(half64, dtype=F32) / half64)
    invm = ROPE_THETA ** (-jnp.arange(halfm, dtype=F32) / halfm)
    row0 = jnp.concatenate([inv64, invm, jnp.zeros((LANES - half64 - halfm,), F32)])
    inv = jnp.zeros((8, LANES), F32).at[0].set(row0)
    tab = jax.ShapeDtypeStruct((B, S, LANES), F32)
    spec = pl.BlockSpec((1, ts, LANES), lambda b, i: (b, i, 0))
    return pl.pallas_call(
        _rope_tables_kernel,
        grid=(B, S // ts),
        in_specs=[pl.BlockSpec((1, ts, 1), lambda b, i: (b, i, 0)), _const_spec((8, LANES))],
        out_specs=[spec] * 4,
        out_shape=[tab] * 4,
        compiler_params=_cparams(("parallel", "parallel")),
        name="rope_tables",
    )(positions.reshape(B, S, 1), inv)


def _rope64(x, cos, sin_signed, lane):
    first = (lane & (HEAD_DIM - 1)) < (HEAD_DIM // 2)
    partner = jnp.where(first, pltpu.roll(x, LANES - HEAD_DIM // 2, 1), pltpu.roll(x, HEAD_DIM // 2, 1))
    return x * cos + partner * sin_signed


def _rope_mla(x, cos, sin_signed, lane):
    first = lane < MLA_NOPE + MLA_ROPE // 2
    partner = jnp.where(first, pltpu.roll(x, LANES - MLA_ROPE // 2, 1), pltpu.roll(x, MLA_ROPE // 2, 1))
    return x * cos + partner * sin_signed


def _ms64(x, bsel):
    return _dot(jnp.concatenate(_split_bf16(x * x), axis=1), bsel) * (1.0 / HEAD_DIM)


def _prep_kernel(x_ref, an_ref, win_ref, c64_ref, s64_ref, cm_ref, sm_ref,
                 qn_ref, wuq_ref, kvn_ref, wukv_ref, qg_ref, kg_ref, g64_ref,
                 qa_ref, ka_ref, va_ref, qm_ref, km_ref, vm_ref, qc_ref, kc_ref, vc_ref,
                 qd_ref, qd32_ref, kd_ref, vd_ref, kmean_ref):
    ts = x_ref.shape[1]
    subs = [slice(r0, r0 + MOBA_BLOCK) for r0 in range(0, ts, MOBA_BLOCK)]
    projs = []
    for rows in subs:
        x = x_ref[0, rows, :]
        ms = jnp.mean(x * x, axis=-1, keepdims=True)
        h = (x * lax.rsqrt(ms + EPS) * an_ref[...]).astype(BF16)
        projs.append(_dot(h, win_ref[...]))

    lane = lax.broadcasted_iota(jnp.int32, (1, LANES), 1)
    lo_half = lane < HEAD_DIM
    r = (lax.broadcasted_iota(jnp.int32, (2 * LANES, LANES), 0) & (LANES - 1)) // HEAD_DIM
    c = lax.broadcasted_iota(jnp.int32, (2 * LANES, LANES), 1) // HEAD_DIM
    bsel = jnp.where(r == c, 1.0, 0.0).astype(BF16)
    scale64 = HEAD_DIM ** -0.5
    scale_m = MLA_QK ** -0.5 * LOG2E
    for rows, proj in zip(subs, projs):
        _prep_finish(proj, rows, lane, lo_half, bsel, scale64, scale_m, c64_ref, s64_ref, cm_ref, sm_ref,
                     qn_ref, wuq_ref, kvn_ref, wukv_ref, qg_ref, kg_ref, g64_ref,
                     qa_ref, ka_ref, va_ref, qm_ref, km_ref, vm_ref, qc_ref, kc_ref, vc_ref,
                     qd_ref, qd32_ref, kd_ref, vd_ref, kmean_ref)


def _prep_finish(proj, rows, lane, lo_half, bsel, scale64, scale_m, c64_ref, s64_ref, cm_ref, sm_ref,
                 qn_ref, wuq_ref, kvn_ref, wukv_ref, qg_ref, kg_ref, g64_ref,
                 qa_ref, ka_ref, va_ref, qm_ref, km_ref, vm_ref, qc_ref, kc_ref, vc_ref,
                 qd_ref, qd32_ref, kd_ref, vd_ref, kmean_ref):
    blk = rows.start // MOBA_BLOCK
    c64, s64, cm, sm = c64_ref[0, rows, :], s64_ref[0, rows, :], cm_ref[0, rows, :], sm_ref[0, rows, :]

    def seg(off, j):
        return proj[:, off + j * SEG: off + (j + 1) * SEG]

    def store_pair(ref, val, seg_lo, seg_hi):
        ref[0, rows, seg_lo * SEG:(seg_lo + 1) * SEG] = jnp.where(lo_half, val, 0.0).astype(BF16)
        ref[0, rows, seg_hi * SEG:(seg_hi + 1) * SEG] = jnp.where(lo_half, 0.0, val).astype(BF16)

    for p in range(2):
        store_pair(qa_ref, seg(0, p) * scale64, 2 * p, 2 * p + 1)
    ka_ref[0, rows, :] = proj[:, 256:512].astype(BF16)
    va_ref[0, rows, :] = proj[:, 512:768].astype(BF16)

    cq = proj[:, 768:1024]
    cqn = (cq * lax.rsqrt(jnp.sum(cq * cq, axis=-1, keepdims=True) * (1.0 / MLA_Q_RANK) + EPS)
           * qn_ref[...]).astype(BF16)
    qm_raw = _dot(cqn, wuq_ref[...])
    ckv = proj[:, 1024:1152]
    ckvn = (ckv * lax.rsqrt(jnp.mean(ckv * ckv, axis=-1, keepdims=True) + EPS) * kvn_ref[...]).astype(BF16)
    kv = _dot(ckvn, wukv_ref[...])
    kpe = proj[:, 1152:1280]

    def store_transposed(ref, v):
        ref[0, blk] = v.T.astype(BF16)

    store_transposed(vm_ref, kv[:, 512:768])
    for hh in range(GROUP_HEADS):
        sl = slice(hh * SEG, (hh + 1) * SEG)
        qs = qm_raw[:, sl]
        qs = qs * lax.rsqrt(jnp.sum(qs * qs, axis=-1, keepdims=True) * (1.0 / MLA_QK) + EPS) * qg_ref[...]
        qm_ref[0, rows, sl] = (_rope_mla(qs, cm, sm, lane) * scale_m).astype(BF16)
        ks = kv[:, sl] + kpe
        ks = ks * lax.rsqrt(jnp.sum(ks * ks, axis=-1, keepdims=True) * (1.0 / MLA_QK) + EPS) * kg_ref[...]
        km_ref[0, rows, sl] = _rope_mla(ks, cm, sm, lane).astype(BF16)

    def norm_rope64(v, gain_row):
        v = v * lax.rsqrt(_ms64(v, bsel) + EPS) * g64_ref[gain_row:gain_row + 1, :]
        return _rope64(v, c64, s64, lane)

    qa_pair = norm_rope64(seg(1280, 0), 0) * scale64
    qb_pair = norm_rope64(seg(1280, 1), 0) * scale64
    store_pair(qc_ref, qa_pair, 0, 2)
    store_pair(qc_ref, qb_pair, 1, 3)
    kc_ref[0, rows, :] = norm_rope64(seg(1536, 0), 1).astype(BF16)
    vc_ref[0, rows, :] = seg(1664, 0).astype(BF16)

    for p in range(2):
        qd = norm_rope64(seg(1792, p), 2)
        qd32_ref[0, rows, p * SEG:(p + 1) * SEG] = qd
        store_pair(qd_ref, qd * (scale64 * LOG2E), 2 * p, 2 * p + 1)
        kd = norm_rope64(seg(2048, p), 3)
        kd_ref[0, rows, p * SEG:(p + 1) * SEG] = kd.astype(BF16)
        kmean_ref[0, blk, :, p * SEG:(p + 1) * SEG] = jnp.mean(kd, axis=0, keepdims=True)
    store_transposed(vd_ref, proj[:, 2304:2560])


def _prep_weights(w_in, mla_q_norm, mla_w_uq, mla_kv_norm, mla_w_ukv, mla_q_gain, mla_k_gain,
                  swa_q_gain, swa_k_gain, moba_q_gain, moba_k_gain):
    D = w_in.shape[0]
    z = lambda n: jnp.zeros((D, n), F32)
    mla = w_in[:, OFF_MLA:OFF_MLA + MLA_COLS]
    swa = w_in[:, OFF_SWA:OFF_SWA + SWA_COLS]
    swa_q = swa[:, :GROUP_WIDTH].reshape(D, GROUP_HEADS, HEAD_DIM)[:, jnp.array([0, 2, 1, 3])].reshape(D, GROUP_WIDTH)
    win = jnp.concatenate([
        w_in[:, OFF_SB:OFF_SB + SB_COLS],
        mla[:, :MLA_Q_RANK], z(256 - MLA_Q_RANK),
        mla[:, MLA_Q_RANK:MLA_Q_RANK + MLA_KV_RANK],
        z(MLA_NOPE), mla[:, MLA_Q_RANK + MLA_KV_RANK:], z(SEG - MLA_QK),
        swa_q, swa[:, GROUP_WIDTH:],
        w_in[:, OFF_MOBA:OFF_MOBA + MOBA_COLS],
    ], axis=1).astype(BF16)
    assert win.shape[1] == N_IN_PAD
    qn = jnp.pad(mla_q_norm, (0, 256 - MLA_Q_RANK)).reshape(1, 256)
    wuq = mla_w_uq.reshape(MLA_Q_RANK, GROUP_HEADS, MLA_QK)
    wuq = jnp.pad(wuq, ((0, 256 - MLA_Q_RANK), (0, 0), (0, SEG - MLA_QK))).reshape(256, Q_PAD).astype(BF16)
    wukv = mla_w_ukv.reshape(MLA_KV_RANK, GROUP_HEADS, MLA_NOPE + MLA_V)
    wkn = jnp.pad(wukv[:, :, :MLA_NOPE], ((0, 0), (0, 0), (0, SEG - MLA_NOPE))).reshape(MLA_KV_RANK, Q_PAD)
    wv = wukv[:, :, MLA_NOPE:].reshape(MLA_KV_RANK, GROUP_WIDTH)
    wukv_r = jnp.concatenate([wkn, wv], axis=1).astype(BF16)
    qg = jnp.pad(mla_q_gain, (0, SEG - MLA_QK)).reshape(1, SEG)
    kg = jnp.pad(mla_k_gain, (0, SEG - MLA_QK)).reshape(1, SEG)
    g64 = jnp.zeros((8, LANES), F32)
    for row, g in enumerate((swa_q_gain, swa_k_gain, moba_q_gain, moba_k_gain)):
        g64 = g64.at[row].set(jnp.tile(g, LANES // HEAD_DIM))
    return win, qn, wuq, mla_kv_norm.reshape(1, MLA_KV_RANK), wukv_r, qg, kg, g64


def _prep(x, attn_norm, tables, weights):
    B, S, D = x.shape
    ts = PREP_TS
    win, qn, wuq, kvn, wukv, qg, kg, g64 = weights
    tok = lambda w: pl.BlockSpec((1, ts, w), lambda b, i: (b, i, 0))
    nb_t = ts // MOBA_BLOCK
    VT = "channel-major value blocks"
    out_widths = [(Q_PAD, BF16), (256, BF16), (256, BF16),
                  (Q_PAD, BF16), (Q_PAD, BF16), (VT, BF16),
                  (Q_PAD, BF16), (128, BF16), (128, BF16),
                  (Q_PAD, BF16), (256, F32), (256, BF16), (VT, BF16)]
    vt_shape = jax.ShapeDtypeStruct((B, S // MOBA_BLOCK, GROUP_WIDTH, MOBA_BLOCK), BF16)
    vt_spec = pl.BlockSpec((1, nb_t, GROUP_WIDTH, MOBA_BLOCK), lambda b, i: (b, i, 0, 0))
    out_shape = [vt_shape if w is VT else jax.ShapeDtypeStruct((B, S, w), dt) for w, dt in out_widths]
    out_specs = [vt_spec if w is VT else tok(w) for w, _ in out_widths]
    out_shape.append(jax.ShapeDtypeStruct((B, S // MOBA_BLOCK, 1, 256), F32))
    out_specs.append(pl.BlockSpec((1, nb_t, 1, 256), lambda b, i: (b, i, 0, 0)))
    consts = [attn_norm.reshape(1, D), win]
    tail = [qn, wuq, kvn, wukv, qg, kg, g64]
    in_specs = ([tok(D)] + [_const_spec(a.shape) for a in consts] + [tok(LANES)] * 4
                + [_const_spec(a.shape) for a in tail])
    return pl.pallas_call(
        _prep_kernel,
        grid=(B, S // ts),
        in_specs=in_specs,
        out_specs=out_specs,
        out_shape=out_shape,
        compiler_params=_cparams(("parallel", "parallel")),
        name="prep",
    )(x, *consts, *tables, *tail)


def _sb_kernel(q_ref, k_ref, v_ref, o_ref, acc_ref, carry_ref):
    T = SB_T
    i = pl.program_id(1)
    row = lax.broadcasted_iota(jnp.int32, (T, T), 0)
    col = lax.broadcasted_iota(jnp.int32, (T, T), 1)
    upper = jnp.concatenate([jnp.where(row > col, 1.0, 0.0), jnp.ones((T, T), F32)], axis=1).astype(BF16)
    upper = jnp.concatenate([upper, upper], axis=0)
    causal = col < row
    lane = lax.broadcasted_iota(jnp.int32, (1, LANES), 1)

    n_sub = q_ref.shape[1] // T
    chains = [(sub, hh) for sub in range(n_sub) for hh in range(GROUP_HEADS)]

    def block(t, first):
        kb = [i * n_sub + sub - t for sub in range(n_sub)]
        ks = [pl.ds(pl.multiple_of(jnp.maximum(b, 0) * T, T), T) for b in kb]
        kv = lambda ref, sub, hh: ref[0, ks[sub], (hh // 2) * SEG:(hh // 2 + 1) * SEG]
        z = [_dot_nt(q_ref[0, sub * T:(sub + 1) * T, hh * SEG:(hh + 1) * SEG], kv(k_ref, sub, hh))
             for sub, hh in chains]
        log_beta = [jnp.minimum(zz, 0.0) - jnp.log(1.0 + jnp.exp(-jnp.abs(zz))) for zz in z]
        log_keep = [lb - zz for lb, zz in zip(log_beta, z)]
        if first:
            log_keep = [jnp.where(causal, lk, 0.0) for lk in log_keep]
        parts = [jnp.concatenate(_split_bf16(lk), axis=1) for lk in log_keep]
        sums = [_dot(hl, upper) for hl in parts]
        if first:
            w = [jnp.where(causal, jnp.exp(lb + sm[:, :T]), 0.0) for lb, sm in zip(log_beta, sums)]
            carry = [sm[:, T:] for sm in sums]
        else:
            live = [kb[sub] >= 0 for sub, hh in chains]
            old = [carry_ref[c] for c in range(len(chains))]
            w = [jnp.where(lv, jnp.exp(lb + (sm[:, :T] + cr)), 0.0) for lv, lb, sm, cr in zip(live, log_beta, sums, old)]
            carry = [jnp.where(lv, cr + sm[:, T:], NEG) for lv, sm, cr in zip(live, sums, old)]
        pv = [_dot(ww.astype(BF16), kv(v_ref, sub, hh)) for (sub, hh), ww in zip(chains, w)]
        for c in range(len(chains)):
            acc_ref[c] = pv[c] if first else acc_ref[c] + pv[c]
            carry_ref[c] = carry[c]
        top = carry[0]
        for cr in carry[1:]:
            top = jnp.maximum(top, cr)
        return jnp.max(top)

    def cond(st):
        t, mx = st
        return jnp.logical_and(t <= i * n_sub + n_sub - 1, mx > SB_EXIT)

    def body(st):
        return st[0] + 1, block(st[0], False)

    lax.while_loop(cond, body, (1, block(0, True)))

    lo_half = lane < HEAD_DIM
    for sub in range(n_sub):
        for p in range(2):
            c = sub * GROUP_HEADS + 2 * p
            o_ref[0, sub * T:(sub + 1) * T, p * SEG:(p + 1) * SEG] = jnp.where(
                lo_half, acc_ref[c], acc_ref[c + 1]).astype(o_ref.dtype)


def _sb_attn(q, k, v):
    B, S, _ = k.shape
    T = SB_T
    rows = SB_SUB * T
    n_chains = SB_SUB * GROUP_HEADS
    full = pl.BlockSpec((1, S, GROUP_WIDTH), lambda b, i: (b, 0, 0))
    return pl.pallas_call(
        _sb_kernel,
        grid=(B, S // rows),
        in_specs=[pl.BlockSpec((1, rows, Q_PAD), lambda b, i: (b, i, 0)), full, full],
        out_specs=pl.BlockSpec((1, rows, GROUP_WIDTH), lambda b, i: (b, i, 0)),
        out_shape=jax.ShapeDtypeStruct((B, S, GROUP_WIDTH), BF16),
        scratch_shapes=[pltpu.VMEM((n_chains, T, SEG), F32), pltpu.VMEM((n_chains, T, T), F32)],
        compiler_params=_cparams(("parallel", "arbitrary")),
        name="sb_attn",
    )(q, k, v)


def _softmax_update_all(acc_ref, m_ref, heads, scores, vt_augs):
    m_old = [m_ref[hh][0:1, :] for hh in heads]
    acc_old = [acc_ref[hh] for hh in heads]
    m_new = [jnp.maximum(mo, jnp.max(s, axis=0, keepdims=True)) for mo, s in zip(m_old, scores)]
    p = [jnp.exp2(s - mn).astype(BF16) for s, mn in zip(scores, m_new)]
    pv = [_dot(va, pp) for va, pp in zip(vt_augs, p)]
    for hh, mo, mn, r, a in zip(heads, m_old, m_new, pv, acc_old):
        acc_ref[hh] = jnp.exp2(mo - mn) * a + r
        m_ref[hh] = jnp.broadcast_to(mn, m_ref.shape[1:])


def _values_t(vt_ref, first_blk, nblk, hh):
    rows = slice(hh * HEAD_DIM, (hh + 1) * HEAD_DIM)
    v = jnp.concatenate([vt_ref[0, first_blk + r, rows, :] for r in range(nblk)], axis=1)
    return jnp.concatenate([v, jnp.ones((ONES_ROWS, v.shape[1]), v.dtype)], axis=0)


def _softmax_pipelined(n_steps, qk_pair, vt_ref, acc_ref, m_ref, s_ref, p_ref, a_ref):
    pairs = ((0, 1), (2, 3))
    s_ref[0] = jnp.full(s_ref.shape[1:], -jnp.inf, F32)
    p_ref[0] = jnp.zeros(p_ref.shape[1:], BF16)
    a_ref[0] = jnp.ones(a_ref.shape[1:], F32)

    def softmax(m_old, s):
        m_new = jnp.maximum(m_old, jnp.max(s, axis=0, keepdims=True))
        return jnp.exp2(s - m_new).astype(BF16), jnp.exp2(m_old - m_new), m_new

    def half_step(j, rd, wr, with_scores):
        prev = jnp.maximum(j - 1, 0) * MLA_WIDE
        m_old = [m_ref[hh][0:1, :] for hh in range(GROUP_HEADS)]
        acc_old = [acc_ref[hh] for hh in range(GROUP_HEADS)]
        s1_prev = [s_ref[rd, n] for n in range(2)]
        p0_prev = [p_ref[rd, n] for n in range(2)]
        a0_prev = [a_ref[rd, n][0:1, :] for n in range(2)]
        vt = [_values_t(vt_ref, prev, MLA_WIDE, hh) for hh in range(GROUP_HEADS)]

        acc = [a0_prev[n] * acc_old[hh] + _dot(vt[hh], p0_prev[n]) for n, hh in enumerate(pairs[0])]
        s0 = qk_pair(0, j * MLA_WIDE) if with_scores else None
        r1 = [softmax(m_old[hh], s1_prev[n]) for n, hh in enumerate(pairs[1])]
        s1 = qk_pair(1, j * MLA_WIDE) if with_scores else None
        acc += [r1[n][1] * acc_old[hh] + _dot(vt[hh], r1[n][0]) for n, hh in enumerate(pairs[1])]
        m_new = [None, None, r1[0][2], r1[1][2]]
        if with_scores:
            r0 = [softmax(m_old[hh], s0[n]) for n, hh in enumerate(pairs[0])]
            m_new[0:2] = [r0[0][2], r0[1][2]]

        for hh in range(GROUP_HEADS):
            acc_ref[hh] = acc[hh]
            if m_new[hh] is not None:
                m_ref[hh] = jnp.broadcast_to(m_new[hh], m_ref.shape[1:])
        if with_scores:
            for n in range(2):
                p_ref[wr, n] = r0[n][0]
                a_ref[wr, n] = jnp.broadcast_to(r0[n][1], a_ref.shape[2:])
                s_ref[wr, n] = s1[n]

    def body(u, carry):
        half_step(2 * u, 0, 1, True)
        half_step(2 * u + 1, 1, 0, True)
        return carry

    lax.fori_loop(0, n_steps // 2, body, 0)

    @pl.when(n_steps % 2 == 1)
    def _():
        half_step(n_steps - 1, 0, 1, True)
        half_step(n_steps, 1, None, False)

    @pl.when(n_steps % 2 == 0)
    def _():
        half_step(n_steps, 0, None, False)


def _softmax_finish(acc_ref, o_ref):
    for p in range(2):
        halves = [acc_ref[hh][0:HEAD_DIM, :] / acc_ref[hh][HEAD_DIM:HEAD_DIM + 1, :] for hh in (2 * p, 2 * p + 1)]
        o_ref[0, :, p * SEG:(p + 1) * SEG] = jnp.concatenate(halves, axis=0).T.astype(o_ref.dtype)


def _mla_kernel(q_ref, k_ref, vt_ref, o_ref, acc_ref, m_ref, s_ref, p_ref, a_ref):
    T = MLA_T
    tk = MLA_WIDE * MOBA_BLOCK
    i = pl.program_id(1)
    key = lax.broadcasted_iota(jnp.int32, (tk, T), 0)
    qry = lax.broadcasted_iota(jnp.int32, (tk, T), 1)
    heads = range(GROUP_HEADS)

    def scores(hh, first_blk):
        ks = pl.ds(pl.multiple_of(first_blk * MOBA_BLOCK, MOBA_BLOCK), tk)
        return _dot_nt(k_ref[0, ks, hh * SEG:(hh + 1) * SEG], q_ref[0, :, hh * SEG:(hh + 1) * SEG])

    m_ref[...] = jnp.full(m_ref.shape, NEG, F32)
    acc_ref[...] = jnp.zeros(acc_ref.shape, F32)
    _softmax_pipelined(i * (T // tk), lambda pair, blk: [scores(2 * pair + n, blk) for n in range(2)],
                       vt_ref, acc_ref, m_ref, s_ref, p_ref, a_ref)
    for r in range(T // tk):
        first_blk = (i * (T // tk) + r) * MLA_WIDE
        s = [jnp.where(key + r * tk <= qry, scores(hh, first_blk), NEG) for hh in heads]
        _softmax_update_all(acc_ref, m_ref, heads, s, [_values_t(vt_ref, first_blk, MLA_WIDE, hh) for hh in heads])
    _softmax_finish(acc_ref, o_ref)


def _softmax_scratch(T):
    tk = MLA_WIDE * MOBA_BLOCK
    assert T % tk == 0
    return [pltpu.VMEM((GROUP_HEADS, HEAD_DIM + ONES_ROWS, T), F32), pltpu.VMEM((GROUP_HEADS, 8, T), F32),
            pltpu.VMEM((2, 2, tk, T), F32), pltpu.VMEM((2, 2, tk, T), BF16), pltpu.VMEM((2, 2, 8, T), F32)]


def _mla_attn(q, k, vt):
    B, S, _ = k.shape
    T = MLA_T
    return pl.pallas_call(
        _mla_kernel,
        grid=(B, S // T),
        in_specs=[pl.BlockSpec((1, T, Q_PAD), lambda b, i: (b, i, 0)),
                  pl.BlockSpec((1, S, Q_PAD), lambda b, i: (b, 0, 0)),
                  pl.BlockSpec((1,) + vt.shape[1:], lambda b, i: (b, 0, 0, 0))],
        out_specs=pl.BlockSpec((1, T, GROUP_WIDTH), lambda b, i: (b, i, 0)),
        out_shape=jax.ShapeDtypeStruct((B, S, GROUP_WIDTH), BF16),
        scratch_shapes=_softmax_scratch(T),
        compiler_params=_cparams(("parallel", "arbitrary")),
        name="mla_attn",
    )(q, k, vt)


def _swa_kernel(sink_ref, q_ref, k_ref, v_ref, o_ref):
    W = WINDOW
    i = pl.program_id(1)
    lane = lax.broadcasted_iota(jnp.int32, (1, LANES), 1)
    lo_half = lane < HEAD_DIM
    heads = range(GROUP_HEADS)
    sinks = [sink_ref[hh] for hh in heads]
    chains, ks, vs, bands = [], [], [], []
    for b in range(q_ref.shape[1] // W):
        q0 = i * q_ref.shape[1] + b * W
        start = pl.multiple_of(jnp.maximum(q0 - W, 0), W)
        ks.append(k_ref[0, pl.ds(start, 2 * W), :])
        vs.append(v_ref[0, pl.ds(start, 2 * W), :])
        qpos = q0 + lax.broadcasted_iota(jnp.int32, (W, 2 * W), 0)
        kpos = start + lax.broadcasted_iota(jnp.int32, (W, 2 * W), 1)
        bands.append(jnp.logical_and(kpos <= qpos, qpos - kpos < W))
        chains += [(b, hh) for hh in heads]
    s = [jnp.where(bands[b], _dot_nt(q_ref[0, b * W:(b + 1) * W, hh * SEG:(hh + 1) * SEG], ks[b]), NEG)
         for b, hh in chains]
    m = [jnp.maximum(jnp.max(ss, axis=1, keepdims=True), sinks[hh]) for ss, (b, hh) in zip(s, chains)]
    p = [jnp.exp(ss - mm) for ss, mm in zip(s, m)]
    denom = [jnp.sum(pp, axis=1, keepdims=True) + jnp.exp(sinks[hh] - mm) for pp, mm, (b, hh) in zip(p, m, chains)]
    outs = [_dot(pp.astype(BF16), vs[b]) / dd for pp, dd, (b, hh) in zip(p, denom, chains)]
    for b in range(q_ref.shape[1] // W):
        o = outs[b * GROUP_HEADS:(b + 1) * GROUP_HEADS]
        rows = slice(b * W, (b + 1) * W)
        o_ref[0, rows, 0:SEG] = jnp.where(lo_half, o[0], pltpu.roll(o[1], HEAD_DIM, 1)).astype(o_ref.dtype)
        o_ref[0, rows, SEG:2 * SEG] = jnp.where(lo_half, pltpu.roll(o[2], HEAD_DIM, 1), o[3]).astype(o_ref.dtype)


def _swa_attn(q, k, v, sinks):
    B, S, _ = k.shape
    W = SWA_T
    full = pl.BlockSpec((1, S, 2 * HEAD_DIM), lambda b, i: (b, 0, 0))
    return pl.pallas_call(
        _swa_kernel,
        grid=(B, S // W),
        in_specs=[pl.BlockSpec(memory_space=pltpu.SMEM),
                  pl.BlockSpec((1, W, Q_PAD), lambda b, i: (b, i, 0)), full, full],
        out_specs=pl.BlockSpec((1, W, GROUP_WIDTH), lambda b, i: (b, i, 0)),
        out_shape=jax.ShapeDtypeStruct((B, S, GROUP_WIDTH), BF16),
        compiler_params=_cparams(("parallel", "parallel")),
        name="swa_attn",
    )(sinks, q, k, v)


def _moba_kernel(q_ref, q32_ref, k_ref, vt_ref, kmean_ref, o_ref, acc_ref, m_ref, s_ref, p_ref, a_ref, qaug_ref):
    T = q_ref.shape[1]
    BLK = MOBA_BLOCK
    tk = MLA_WIDE * BLK
    i = pl.program_id(1)
    nbr = kmean_ref.shape[1]
    key = lax.broadcasted_iota(jnp.int32, (tk, T), 0)
    qry = lax.broadcasted_iota(jnp.int32, (tk, T), 1)
    lo_half = lax.broadcasted_iota(jnp.int32, (1, LANES), 1) < HEAD_DIM
    blk = lax.broadcasted_iota(jnp.int32, (nbr, T), 0).astype(F32)
    kblk = lax.broadcasted_iota(jnp.int32, (BLK, LANES), 1)
    own = (i * (T // BLK) + lax.broadcasted_iota(jnp.int32, (1, T), 1) // BLK).astype(F32)

    m_ref[...] = jnp.full(m_ref.shape, NEG, F32)
    acc_ref[...] = jnp.zeros(acc_ref.shape, F32)
    heads = range(GROUP_HEADS)
    pair_sl = [slice((hh // 2) * SEG, (hh // 2 + 1) * SEG) for hh in heads]

    g = []
    for hh in heads:
        head_lanes = lo_half if hh % 2 == 0 else jnp.logical_not(lo_half)
        qh_hi, qh_lo = _split_bf16(jnp.where(head_lanes, q32_ref[0, :, pair_sl[hh]], 0.0))
        km_hi, km_lo = _split_bf16(kmean_ref[0, :, pair_sl[hh]])
        gate = _dot_nt(km_hi, qh_hi) + (_dot_nt(km_lo, qh_hi) + _dot_nt(km_hi, qh_lo))
        g.append(jnp.where(blk < own, gate, NEG))
    sel = [jnp.where(blk == own, 1.0, 0.0) for _ in heads]
    for t in range(MOBA_TOPK):
        mx = [jnp.max(gg, axis=0, keepdims=True) for gg in g]
        first = [jnp.min(jnp.where(gg == m, blk, float(nbr)), axis=0, keepdims=True)
                 for gg, m in zip(g, mx)]
        hit = [blk == f for f in first]
        sel = [jnp.where(jnp.logical_and(h, t < own), 1.0, sl) for h, sl in zip(hit, sel)]
        g = [jnp.where(h, -jnp.inf, gg) for h, gg in zip(hit, g)]
    for hh in heads:
        neg_t = jnp.concatenate([(1.0 - sel[hh]) * NEG, jnp.zeros((LANES - nbr, T), F32)], axis=0)
        qaug_ref[hh, :, 0:SEG] = q_ref[0, :, hh * SEG:(hh + 1) * SEG]
        qaug_ref[hh, :, SEG:2 * SEG] = neg_t.T.astype(BF16)

    def scores_pair(pair, n0):
        ks = pl.ds(pl.multiple_of(n0 * BLK, BLK), MLA_WIDE * BLK)
        onehot = jnp.concatenate([jnp.where(kblk == n0 + r, 1.0, 0.0) for r in range(MLA_WIDE)],
                                 axis=0).astype(BF16)
        k_aug = jnp.concatenate([k_ref[0, ks, pair * SEG:(pair + 1) * SEG], onehot], axis=1)
        return [_dot_nt(k_aug, qaug_ref[2 * pair + n]) for n in range(2)]

    _softmax_pipelined(i * (T // tk), scores_pair, vt_ref, acc_ref, m_ref, s_ref, p_ref, a_ref)
    for r in range(T // tk):
        n0 = (i * (T // tk) + r) * MLA_WIDE
        s = [jnp.where(key + r * tk <= qry, ss, NEG) for ss in scores_pair(0, n0) + scores_pair(1, n0)]
        _softmax_update_all(acc_ref, m_ref, heads, s, [_values_t(vt_ref, n0, MLA_WIDE, hh) for hh in heads])
    _softmax_finish(acc_ref, o_ref)


def _moba_attn(q, q32, k, vt, kmean):
    B, S, _ = k.shape
    T = MLA_T
    n_blocks = S // MOBA_BLOCK
    assert n_blocks <= LANES
    nbr = -(-n_blocks // 16) * 16
    kmean = jnp.pad(kmean.reshape(B, n_blocks, GROUP_WIDTH), ((0, 0), (0, nbr - n_blocks), (0, 0)))
    return pl.pallas_call(
        _moba_kernel,
        grid=(B, S // T),
        in_specs=[pl.BlockSpec((1, T, Q_PAD), lambda b, i: (b, i, 0)),
                  pl.BlockSpec((1, T, GROUP_WIDTH), lambda b, i: (b, i, 0)),
                  pl.BlockSpec((1, S, GROUP_WIDTH), lambda b, i: (b, 0, 0)),
                  pl.BlockSpec((1,) + vt.shape[1:], lambda b, i: (b, 0, 0, 0)),
                  pl.BlockSpec((1, nbr, GROUP_WIDTH), lambda b, i: (b, 0, 0))],
        out_specs=pl.BlockSpec((1, T, GROUP_WIDTH), lambda b, i: (b, i, 0)),
        out_shape=jax.ShapeDtypeStruct((B, S, GROUP_WIDTH), BF16),
        scratch_shapes=_softmax_scratch(T) + [pltpu.VMEM((GROUP_HEADS, T, 2 * SEG), BF16)],
        compiler_params=_cparams(("parallel", "arbitrary")),
        name="moba_attn",
    )(q, q32, k, vt, kmean)


def _mix_groups(x, group_refs, gn_ref, wo_ref):
    for g, ref in enumerate(group_refs):
        o = ref[0].astype(F32)
        ms = jnp.mean(o * o, axis=-1, keepdims=True)
        n = (o * lax.rsqrt(ms + EPS) * gn_ref[:, g * GROUP_WIDTH:(g + 1) * GROUP_WIDTH]).astype(BF16)
        x = x + _dot(n, wo_ref[g * GROUP_WIDTH:(g + 1) * GROUP_WIDTH, :])
    return x


def _ffn_kernel(x_ref, oa_ref, ob_ref, oc_ref, od_ref, p_ref, gn_ref, wo_ref,
                fn_ref, wup_ref, cw_ref, cb_ref, wdn_ref, pproj_ref, pgate_ref, y_ref,
                halo_ref, ubuf0_ref, ubuf1_ref, h_ref):
    ts = x_ref.shape[1]
    fc2 = wup_ref.shape[2]
    fc = fc2 // 2
    n_chunks = wup_ref.shape[0]
    i = pl.program_id(1)

    @pl.when(i == 0)
    def _():
        halo_ref[...] = jnp.zeros(halo_ref.shape, F32)

    x = _mix_groups(x_ref[0], (oa_ref, ob_ref, oc_ref, od_ref), gn_ref, wo_ref)
    ms = jnp.mean(x * x, axis=-1, keepdims=True)
    h_ref[...] = (x * lax.rsqrt(ms + EPS) * fn_ref[...]).astype(BF16)
    y_ref[0] = x

    def up(c, ubuf_ref):
        u = _dot(h_ref[...], wup_ref[c])
        ubuf_ref[0:HALO, :] = halo_ref[c]
        ubuf_ref[HALO:HALO + ts, :] = u
        halo_ref[c] = u[ts - HALO:ts, :]

    def down(c, ubuf_ref):
        cw = cw_ref[c]
        y = (cw[0:1, :] * ubuf_ref[HALO - 2:HALO - 2 + ts, :] + cw[1:2, :] * ubuf_ref[HALO - 1:HALO - 1 + ts, :]
             + cw[2:3, :] * ubuf_ref[HALO:HALO + ts, :] + cb_ref[c])
        ya = y[:, :fc]
        g = ya * (1.0 / (1.0 + jnp.exp(-ya))) * y[:, fc:]
        y_ref[0] += _dot(g.astype(BF16), wdn_ref[c])

    up(0, ubuf0_ref)

    def pair(cc, carry):
        up(2 * cc + 1, ubuf1_ref)
        down(2 * cc, ubuf0_ref)
        up(2 * cc + 2, ubuf0_ref)
        down(2 * cc + 1, ubuf1_ref)
        return carry

    assert n_chunks % 2 == 1
    lax.fori_loop(0, n_chunks // 2, pair, 0)
    down(n_chunks - 1, ubuf0_ref)
    x2 = y_ref[0]
    gate = _dot(x2.astype(BF16), pgate_ref[...])
    emb = _dot(p_ref[0, 0].astype(BF16), pproj_ref[...])
    y_ref[0] = x2 + emb * (1.0 / (1.0 + jnp.exp(-gate)))


def _ffn_weights(ffn_norm, w_up, conv_w, conv_b, w_down, ple_proj, ple_gate):
    fc = FFN_FC
    nc = D_FF // fc
    D = w_up.shape[0]
    pair = lambda a: jnp.concatenate([a[..., :D_FF].reshape(a.shape[:-1] + (nc, fc)),
                                      a[..., D_FF:].reshape(a.shape[:-1] + (nc, fc))], axis=-1)
    wup = jnp.moveaxis(pair(w_up), 1, 0).astype(BF16)
    cw = jnp.pad(jnp.moveaxis(pair(conv_w), 1, 0), ((0, 0), (0, 8 - CONV_WIDTH), (0, 0)))
    cb = jnp.moveaxis(pair(conv_b.reshape(1, -1)), 1, 0)
    wdn = w_down.reshape(nc, fc, D).astype(BF16)
    return ffn_norm.reshape(1, D), wup, cw, cb, wdn, ple_proj.astype(BF16), ple_gate.astype(BF16)


def _ffn(x, groups, p, layer, group_norm, w_o, weights):
    B, S, D = x.shape
    ts = FFN_TS
    fn, wup, cw, cb, wdn, pproj, pgate = weights
    consts = (group_norm.reshape(1, D), w_o) + tuple(weights)
    nc, _, fc2 = wup.shape
    tok = lambda w: pl.BlockSpec((1, ts, w), lambda b, i: (b, i, 0))
    once = lambda a: pl.BlockSpec(a.shape, lambda *_: (0,) * a.ndim, pipeline_mode=pl.Buffered(1))
    return pl.pallas_call(
        _ffn_kernel,
        grid=(B, S // ts),
        in_specs=([tok(D)] + [tok(GROUP_WIDTH)] * 4
                  + [pl.BlockSpec((1, 1, ts, PLE_DIM), lambda b, i: (layer, b, i, 0))] + [once(a) for a in consts]),
        out_specs=tok(D),
        out_shape=jax.ShapeDtypeStruct((B, S, D), F32),
        scratch_shapes=[pltpu.VMEM((nc, HALO, fc2), F32), pltpu.VMEM((ts + HALO, fc2), F32),
                        pltpu.VMEM((ts + HALO, fc2), F32), pltpu.VMEM((ts, D), BF16)],
        compiler_params=_cparams(("arbitrary", "arbitrary")),
        name="ffn_ple",
    )(x, *groups, p, *consts)


def kernel(x, p, positions, attn_norm, w_in, mla_q_norm, mla_w_uq, mla_kv_norm, mla_w_ukv, mla_q_gain, mla_k_gain, swa_q_gain, swa_k_gain, swa_sinks, moba_q_gain, moba_k_gain, group_norm, w_o, ffn_norm, w_up, conv_w, conv_b, w_down, ple_proj, ple_gate):
    B, S, D = x.shape
    depth = w_in.shape[0]
    assert D == D_MODEL and S % max(MLA_T, PREP_TS, FFN_TS, SWA_T, SB_SUB * SB_T) == 0
    tables = _rope_tables(positions)
    for i in range(depth):
        pw = _prep_weights(w_in[i], mla_q_norm[i], mla_w_uq[i], mla_kv_norm[i], mla_w_ukv[i], mla_q_gain[i],
                           mla_k_gain[i], swa_q_gain[i], swa_k_gain[i], moba_q_gain[i], moba_k_gain[i])
        (qa, ka, va, qm, km, vm, qc, kc, vc, qd, qd32, kd, vd, kmean) = _prep(x, attn_norm[i], tables, pw)
        o_a = _sb_attn(qa, ka, va)
        o_b = _mla_attn(qm, km, vm)
        o_c = _swa_attn(qc, kc, vc, swa_sinks[i])
        o_d = _moba_attn(qd, qd32, kd, vd, kmean)
        fw = _ffn_weights(ffn_norm[i], w_up[i], conv_w[i], conv_b[i], w_down[i], ple_proj[i], ple_gate[i])
        x = _ffn(x, (o_a, o_b, o_c, o_d), p, i, group_norm[i], w_o[i].astype(BF16), fw)
    return x
```

```python
import functools

import jax
import jax.numpy as jnp
from jax import lax
from jax.experimental import pallas as pl
from jax.experimental.pallas import tpu as pltpu

F32 = jnp.float32
BF16 = jnp.bfloat16

D_MODEL = 1024
HEAD_DIM = 64
GROUP_HEADS = 4
GROUP_WIDTH = GROUP_HEADS * HEAD_DIM
N_GROUPS = 4
ROPE_THETA = 10000.0
EPS = 1e-6
NEG = -1e30
LOG2E = 1.4426950408889634

MLA_Q_RANK = 192
MLA_KV_RANK = 128
MLA_NOPE = 64
MLA_ROPE = 32
MLA_V = 64
MLA_QK = MLA_NOPE + MLA_ROPE
SWA_KV_HEADS = 2
WINDOW = 128
MOBA_BLOCK = 256
MOBA_TOPK = 3
D_FF = 2816
CONV_WIDTH = 3
PLE_DIM = 256

SB_COLS = 3 * GROUP_WIDTH
MLA_COLS = MLA_Q_RANK + MLA_KV_RANK + MLA_ROPE
SWA_COLS = GROUP_WIDTH + 2 * SWA_KV_HEADS * HEAD_DIM
MOBA_COLS = 3 * GROUP_WIDTH
OFF_SB = 0
OFF_MLA = OFF_SB + SB_COLS
OFF_SWA = OFF_MLA + MLA_COLS
OFF_MOBA = OFF_SWA + SWA_COLS

LANES = 128
SEG = LANES
Q_PAD = GROUP_HEADS * SEG
CQ_PAD = 2 * LANES
KV_W = SWA_KV_HEADS * HEAD_DIM
COL_QA, COL_KA, COL_VA = 0, GROUP_WIDTH, 2 * GROUP_WIDTH
COL_CQ = COL_VA + GROUP_WIDTH
COL_CKV = COL_CQ + CQ_PAD
COL_KPE = COL_CKV + MLA_KV_RANK
COL_QC = COL_KPE + SEG
COL_KC = COL_QC + GROUP_WIDTH
COL_VC = COL_KC + KV_W
COL_QD = COL_VC + KV_W
COL_KD = COL_QD + GROUP_WIDTH
COL_VD = COL_KD + GROUP_WIDTH
N_IN_PAD = COL_VD + GROUP_WIDTH
VMEM_LIMIT = 56 * 1024 * 1024

PREP_TS = 512
FFN_TS = 512
FFN_FC = 256
SWA_T = 512
SB_T = 128
SB_SUB = 4
SB_EXIT = -45.0
MLA_WIDE = 2
MLA_T = 1024
HALO = 8
ONES_ROWS = 16


def _dot(a, b):
    return jnp.dot(a, b, preferred_element_type=F32)


def _dot_nt(a, b):
    return lax.dot_general(a, b, (((1,), (1,)), ((), ())), preferred_element_type=F32)


def _split_bf16(x):
    hi = x.astype(BF16)
    lo = (x - hi.astype(F32)).astype(BF16)
    return hi, lo


def _cparams(sem):
    return pltpu.CompilerParams(dimension_semantics=sem, vmem_limit_bytes=VMEM_LIMIT)


def _const_spec(shape):
    nd = len(shape)
    return pl.BlockSpec(shape, lambda *_: (0,) * nd)


def _rope_tables_kernel(pos_ref, inv_ref, c64_ref, s64_ref, cm_ref, sm_ref):
    pos = pos_ref[0].astype(F32)
    lane = lax.broadcasted_iota(jnp.int32, (1, LANES), 1)
    h64, hm = HEAD_DIM // 2, MLA_ROPE // 2
    ang = pos * inv_ref[0:1, :]
    c, s = jnp.cos(ang), jnp.sin(ang)
    c0, s0 = jnp.where(lane < h64, c, 0.0), jnp.where(lane < h64, s, 0.0)
    spread = lambda v: v + pltpu.roll(v, h64, 1) + pltpu.roll(v, 2 * h64, 1) + pltpu.roll(v, 3 * h64, 1)
    first = (lane & (HEAD_DIM - 1)) < h64
    c64_ref[0] = spread(c0)
    s64_ref[0] = jnp.where(first, -spread(s0), spread(s0))
    in_m = (lane >= h64) & (lane < h64 + hm)
    ca, sa = jnp.where(in_m, c, 0.0), jnp.where(in_m, s, 0.0)
    lo_m = (lane >= MLA_NOPE) & (lane < MLA_NOPE + hm)
    hi_m = (lane >= MLA_NOPE + hm) & (lane < MLA_QK)
    to_lo = lambda v: pltpu.roll(v, MLA_NOPE - h64, 1)
    to_hi = lambda v: pltpu.roll(v, MLA_NOPE - h64 + hm, 1)
    cm_ref[0] = jnp.where(lo_m, to_lo(ca), jnp.where(hi_m, to_hi(ca), 1.0))
    sm_ref[0] = jnp.where(lo_m, -to_lo(sa), jnp.where(hi_m, to_hi(sa), 0.0))


def _rope_tables(positions):
    B, S = positions.shape
    ts = 512
    half64 = HEAD_DIM // 2
    halfm = MLA_ROPE // 2
    inv64 = ROPE_THETA ** (-jnp.arange(half64, dtype=F32) / half64)
    invm = ROPE_THETA ** (-jnp.arange(halfm, dtype=F32) / halfm)
    row0 = jnp.concatenate([inv64, invm, jnp.zeros((LANES - half64 - halfm,), F32)])
    inv = jnp.zeros((8, LANES), F32).at[0].set(row0)
    tab = jax.ShapeDtypeStruct((B, S, LANES), F32)
    spec = pl.BlockSpec((1, ts, LANES), lambda b, i: (b, i, 0))
    return pl.pallas_call(
        _rope_tables_kernel,
        grid=(B, S // ts),
        in_specs=[pl.BlockSpec((1, ts, 1), lambda b, i: (b, i, 0)), _const_spec((8, LANES))],
        out_specs=[spec] * 4,
        out_shape=[tab] * 4,
        compiler_params=_cparams(("parallel", "parallel")),
        name="rope_tables",
    )(positions.reshape(B, S, 1), inv)


def _rope64(x, cos, sin_signed, lane):
    first = (lane & (HEAD_DIM - 1)) < (HEAD_DIM // 2)
    partner = jnp.where(first, pltpu.roll(x, LANES - HEAD_DIM // 2, 1), pltpu.roll(x, HEAD_DIM // 2, 1))
    return x * cos + partner * sin_signed


def _rope_mla(x, cos, sin_signed, lane):
    first = lane < MLA_NOPE + MLA_ROPE // 2
    partner = jnp.where(first, pltpu.roll(x, LANES - MLA_ROPE // 2, 1), pltpu.roll(x, MLA_ROPE // 2, 1))
    return x * cos + partner * sin_signed


def _ms64(x, bsel):
    return _dot(jnp.concatenate(_split_bf16(x * x), axis=1), bsel) * (1.0 / HEAD_DIM)


def _prep_kernel(x_ref, an_ref, win_ref, c64_ref, s64_ref, cm_ref, sm_ref,
                 qn_ref, wuq_ref, kvn_ref, wukv_ref, qg_ref, kg_ref, g64_ref,
                 qa_ref, ka_ref, va_ref, qm_ref, km_ref, vm_ref, qc_ref, kc_ref, vc_ref,
                 qd_ref, qd32_ref, kd_ref, vd_ref, kmean_ref):
    ts = x_ref.shape[1]
    subs = [slice(r0, r0 + MOBA_BLOCK) for r0 in range(0, ts, MOBA_BLOCK)]
    projs = []
    for rows in subs:
        x = x_ref[0, rows, :]
        ms = jnp.mean(x * x, axis=-1, keepdims=True)
        h = (x * lax.rsqrt(ms + EPS) * an_ref[...]).astype(BF16)
        projs.append(_dot(h, win_ref[...]))

    lane = lax.broadcasted_iota(jnp.int32, (1, LANES), 1)
    lo_half = lane < HEAD_DIM
    r = (lax.broadcasted_iota(jnp.int32, (2 * LANES, LANES), 0) & (LANES - 1)) // HEAD_DIM
    c = lax.broadcasted_iota(jnp.int32, (2 * LANES, LANES), 1) // HEAD_DIM
    bsel = jnp.where(r == c, 1.0, 0.0).astype(BF16)
    scale64 = HEAD_DIM ** -0.5
    scale_m = MLA_QK ** -0.5 * LOG2E
    for rows, proj in zip(subs, projs):
        _prep_finish(proj, rows, lane, lo_half, bsel, scale64, scale_m, c64_ref, s64_ref, cm_ref, sm_ref,
                     qn_ref, wuq_ref, kvn_ref, wukv_ref, qg_ref, kg_ref, g64_ref,
                     qa_ref, ka_ref, va_ref, qm_ref, km_ref, vm_ref, qc_ref, kc_ref, vc_ref,
                     qd_ref, qd32_ref, kd_ref, vd_ref, kmean_ref)


def _prep_finish(proj, rows, lane, lo_half, bsel, scale64, scale_m, c64_ref, s64_ref, cm_ref, sm_ref,
                 qn_ref, wuq_ref, kvn_ref, wukv_ref, qg_ref, kg_ref, g64_ref,
                 qa_ref, ka_ref, va_ref, qm_ref, km_ref, vm_ref, qc_ref, kc_ref, vc_ref,
                 qd_ref, qd32_ref, kd_ref, vd_ref, kmean_ref):
    blk = rows.start // MOBA_BLOCK
    c64, s64, cm, sm = c64_ref[0, rows, :], s64_ref[0, rows, :], cm_ref[0, rows, :], sm_ref[0, rows, :]

    def seg(off, j):
        return proj[:, off + j * SEG: off + (j + 1) * SEG]

    def store_pair(ref, val, seg_lo, seg_hi):
        ref[0, rows, seg_lo * SEG:(seg_lo + 1) * SEG] = jnp.where(lo_half, val, 0.0).astype(BF16)
        ref[0, rows, seg_hi * SEG:(seg_hi + 1) * SEG] = jnp.where(lo_half, 0.0, val).astype(BF16)

    for p in range(2):
        store_pair(qa_ref, seg(COL_QA, p) * scale64, 2 * p, 2 * p + 1)
    ka_ref[0, rows, :] = proj[:, COL_KA:COL_VA].astype(BF16)
    va_ref[0, rows, :] = proj[:, COL_VA:COL_CQ].astype(BF16)

    cq = proj[:, COL_CQ:COL_CKV]
    cqn = (cq * lax.rsqrt(jnp.sum(cq * cq, axis=-1, keepdims=True) * (1.0 / MLA_Q_RANK) + EPS)
           * qn_ref[...]).astype(BF16)
    qm_raw = _dot(cqn, wuq_ref[...])
    ckv = proj[:, COL_CKV:COL_KPE]
    ckvn = (ckv * lax.rsqrt(jnp.mean(ckv * ckv, axis=-1, keepdims=True) + EPS) * kvn_ref[...]).astype(BF16)
    kv = _dot(ckvn, wukv_ref[...])
    kpe = proj[:, COL_KPE:COL_QC]

    def store_transposed(ref, v):
        ref[0, blk] = v.T.astype(BF16)

    store_transposed(vm_ref, kv[:, Q_PAD:])
    for hh in range(GROUP_HEADS):
        sl = slice(hh * SEG, (hh + 1) * SEG)
        qs = qm_raw[:, sl]
        qs = qs * lax.rsqrt(jnp.sum(qs * qs, axis=-1, keepdims=True) * (1.0 / MLA_QK) + EPS) * qg_ref[...]
        qm_ref[0, rows, sl] = (_rope_mla(qs, cm, sm, lane) * scale_m).astype(BF16)
        ks = kv[:, sl] + kpe
        ks = ks * lax.rsqrt(jnp.sum(ks * ks, axis=-1, keepdims=True) * (1.0 / MLA_QK) + EPS) * kg_ref[...]
        km_ref[0, rows, sl] = _rope_mla(ks, cm, sm, lane).astype(BF16)

    def norm_rope64(v, gain_row):
        v = v * lax.rsqrt(_ms64(v, bsel) + EPS) * g64_ref[gain_row:gain_row + 1, :]
        return _rope64(v, c64, s64, lane)

    qa_pair = norm_rope64(seg(COL_QC, 0), 0) * scale64
    qb_pair = norm_rope64(seg(COL_QC, 1), 0) * scale64
    store_pair(qc_ref, qa_pair, 0, 2)
    store_pair(qc_ref, qb_pair, 1, 3)
    kc_ref[0, rows, :] = norm_rope64(seg(COL_KC, 0), 1).astype(BF16)
    vc_ref[0, rows, :] = seg(COL_VC, 0).astype(BF16)

    for p in range(2):
        qd = norm_rope64(seg(COL_QD, p), 2)
        qd32_ref[0, rows, p * SEG:(p + 1) * SEG] = qd
        store_pair(qd_ref, qd * (scale64 * LOG2E), 2 * p, 2 * p + 1)
        kd = norm_rope64(seg(COL_KD, p), 3)
        kd_ref[0, rows, p * SEG:(p + 1) * SEG] = kd.astype(BF16)
        kmean_ref[0, blk, :, p * SEG:(p + 1) * SEG] = jnp.mean(kd, axis=0, keepdims=True)
    store_transposed(vd_ref, proj[:, COL_VD:N_IN_PAD])


def _prep_weights(w_in, mla_q_norm, mla_w_uq, mla_kv_norm, mla_w_ukv, mla_q_gain, mla_k_gain,
                  swa_q_gain, swa_k_gain, moba_q_gain, moba_k_gain):
    D = w_in.shape[0]
    z = lambda n: jnp.zeros((D, n), F32)
    mla = w_in[:, OFF_MLA:OFF_MLA + MLA_COLS]
    swa = w_in[:, OFF_SWA:OFF_SWA + SWA_COLS]
    swa_q = swa[:, :GROUP_WIDTH].reshape(D, GROUP_HEADS, HEAD_DIM)[:, jnp.array([0, 2, 1, 3])].reshape(D, GROUP_WIDTH)
    win = jnp.concatenate([
        w_in[:, OFF_SB:OFF_SB + SB_COLS],
        mla[:, :MLA_Q_RANK], z(CQ_PAD - MLA_Q_RANK),
        mla[:, MLA_Q_RANK:MLA_Q_RANK + MLA_KV_RANK],
        z(MLA_NOPE), mla[:, MLA_Q_RANK + MLA_KV_RANK:], z(SEG - MLA_QK),
        swa_q, swa[:, GROUP_WIDTH:],
        w_in[:, OFF_MOBA:OFF_MOBA + MOBA_COLS],
    ], axis=1).astype(BF16)
    assert win.shape[1] == N_IN_PAD
    qn = jnp.pad(mla_q_norm, (0, CQ_PAD - MLA_Q_RANK)).reshape(1, CQ_PAD)
    wuq = mla_w_uq.reshape(MLA_Q_RANK, GROUP_HEADS, MLA_QK)
    wuq = jnp.pad(wuq, ((0, CQ_PAD - MLA_Q_RANK), (0, 0), (0, SEG - MLA_QK))).reshape(CQ_PAD, Q_PAD).astype(BF16)
    wukv = mla_w_ukv.reshape(MLA_KV_RANK, GROUP_HEADS, MLA_NOPE + MLA_V)
    wkn = jnp.pad(wukv[:, :, :MLA_NOPE], ((0, 0), (0, 0), (0, SEG - MLA_NOPE))).reshape(MLA_KV_RANK, Q_PAD)
    wv = wukv[:, :, MLA_NOPE:].reshape(MLA_KV_RANK, GROUP_WIDTH)
    wukv_r = jnp.concatenate([wkn, wv], axis=1).astype(BF16)
    qg = jnp.pad(mla_q_gain, (0, SEG - MLA_QK)).reshape(1, SEG)
    kg = jnp.pad(mla_k_gain, (0, SEG - MLA_QK)).reshape(1, SEG)
    g64 = jnp.zeros((8, LANES), F32)
    for row, g in enumerate((swa_q_gain, swa_k_gain, moba_q_gain, moba_k_gain)):
        g64 = g64.at[row].set(jnp.tile(g, LANES // HEAD_DIM))
    return win, qn, wuq, mla_kv_norm.reshape(1, MLA_KV_RANK), wukv_r, qg, kg, g64


def _prep(x, attn_norm, tables, weights):
    B, S, D = x.shape
    ts = PREP_TS
    win, qn, wuq, kvn, wukv, qg, kg, g64 = weights
    tok = lambda w: pl.BlockSpec((1, ts, w), lambda b, i: (b, i, 0))
    nb_t = ts // MOBA_BLOCK
    VT = "channel-major value blocks"
    GW = GROUP_WIDTH
    out_widths = [(Q_PAD, BF16), (GW, BF16), (GW, BF16),
                  (Q_PAD, BF16), (Q_PAD, BF16), (VT, BF16),
                  (Q_PAD, BF16), (KV_W, BF16), (KV_W, BF16),
                  (Q_PAD, BF16), (GW, F32), (GW, BF16), (VT, BF16)]
    vt_shape = jax.ShapeDtypeStruct((B, S // MOBA_BLOCK, GROUP_WIDTH, MOBA_BLOCK), BF16)
    vt_spec = pl.BlockSpec((1, nb_t, GROUP_WIDTH, MOBA_BLOCK), lambda b, i: (b, i, 0, 0))
    out_shape = [vt_shape if w is VT else jax.ShapeDtypeStruct((B, S, w), dt) for w, dt in out_widths]
    out_specs = [vt_spec if w is VT else tok(w) for w, _ in out_widths]
    out_shape.append(jax.ShapeDtypeStruct((B, S // MOBA_BLOCK, 1, GW), F32))
    out_specs.append(pl.BlockSpec((1, nb_t, 1, GW), lambda b, i: (b, i, 0, 0)))
    consts = [attn_norm.reshape(1, D), win]
    tail = [qn, wuq, kvn, wukv, qg, kg, g64]
    in_specs = ([tok(D)] + [_const_spec(a.shape) for a in consts] + [tok(LANES)] * 4
                + [_const_spec(a.shape) for a in tail])
    return pl.pallas_call(
        _prep_kernel,
        grid=(B, S // ts),
        in_specs=in_specs,
        out_specs=out_specs,
        out_shape=out_shape,
        compiler_params=_cparams(("parallel", "parallel")),
        name="prep",
    )(x, *consts, *tables, *tail)


def _sb_kernel(q_ref, k_ref, v_ref, o_ref, acc_ref, carry_ref):
    T = SB_T
    i = pl.program_id(1)
    row = lax.broadcasted_iota(jnp.int32, (T, T), 0)
    col = lax.broadcasted_iota(jnp.int32, (T, T), 1)
    upper = jnp.concatenate([jnp.where(row > col, 1.0, 0.0), jnp.ones((T, T), F32)], axis=1).astype(BF16)
    upper = jnp.concatenate([upper, upper], axis=0)
    causal = col < row
    lane = lax.broadcasted_iota(jnp.int32, (1, LANES), 1)

    n_sub = q_ref.shape[1] // T
    chains = [(sub, hh) for sub in range(n_sub) for hh in range(GROUP_HEADS)]

    def block(t, first):
        kb = [i * n_sub + sub - t for sub in range(n_sub)]
        ks = [pl.ds(pl.multiple_of(jnp.maximum(b, 0) * T, T), T) for b in kb]
        kv = lambda ref, sub, hh: ref[0, ks[sub], (hh // 2) * SEG:(hh // 2 + 1) * SEG]
        z = [_dot_nt(q_ref[0, sub * T:(sub + 1) * T, hh * SEG:(hh + 1) * SEG], kv(k_ref, sub, hh))
             for sub, hh in chains]
        log_beta = [jnp.minimum(zz, 0.0) - jnp.log(1.0 + jnp.exp(-jnp.abs(zz))) for zz in z]
        log_keep = [lb - zz for lb, zz in zip(log_beta, z)]
        if first:
            log_keep = [jnp.where(causal, lk, 0.0) for lk in log_keep]
        parts = [jnp.concatenate(_split_bf16(lk), axis=1) for lk in log_keep]
        sums = [_dot(hl, upper) for hl in parts]
        if first:
            w = [jnp.where(causal, jnp.exp(lb + sm[:, :T]), 0.0) for lb, sm in zip(log_beta, sums)]
            carry = [sm[:, T:] for sm in sums]
        else:
            live = [kb[sub] >= 0 for sub, hh in chains]
            old = [carry_ref[c] for c in range(len(chains))]
            w = [jnp.where(lv, jnp.exp(lb + (sm[:, :T] + cr)), 0.0) for lv, lb, sm, cr in zip(live, log_beta, sums, old)]
            carry = [jnp.where(lv, cr + sm[:, T:], NEG) for lv, sm, cr in zip(live, sums, old)]
        pv = [_dot(ww.astype(BF16), kv(v_ref, sub, hh)) for (sub, hh), ww in zip(chains, w)]
        for c in range(len(chains)):
            acc_ref[c] = pv[c] if first else acc_ref[c] + pv[c]
            carry_ref[c] = carry[c]
        top = carry[0]
        for cr in carry[1:]:
            top = jnp.maximum(top, cr)
        return jnp.max(top)

    def cond(st):
        t, mx = st
        return jnp.logical_and(t <= i * n_sub + n_sub - 1, mx > SB_EXIT)

    def body(st):
        return st[0] + 1, block(st[0], False)

    lax.while_loop(cond, body, (1, block(0, True)))

    lo_half = lane < HEAD_DIM
    for sub in range(n_sub):
        for p in range(2):
            c = sub * GROUP_HEADS + 2 * p
            o_ref[0, sub * T:(sub + 1) * T, p * SEG:(p + 1) * SEG] = jnp.where(
                lo_half, acc_ref[c], acc_ref[c + 1]).astype(o_ref.dtype)


def _sb_attn(q, k, v):
    B, S, _ = k.shape
    T = SB_T
    rows = SB_SUB * T
    n_chains = SB_SUB * GROUP_HEADS
    full = pl.BlockSpec((1, S, GROUP_WIDTH), lambda b, i: (b, 0, 0))
    return pl.pallas_call(
        _sb_kernel,
        grid=(B, S // rows),
        in_specs=[pl.BlockSpec((1, rows, Q_PAD), lambda b, i: (b, i, 0)), full, full],
        out_specs=pl.BlockSpec((1, rows, GROUP_WIDTH), lambda b, i: (b, i, 0)),
        out_shape=jax.ShapeDtypeStruct((B, S, GROUP_WIDTH), BF16),
        scratch_shapes=[pltpu.VMEM((n_chains, T, SEG), F32), pltpu.VMEM((n_chains, T, T), F32)],
        compiler_params=_cparams(("parallel", "arbitrary")),
        name="sb_attn",
    )(q, k, v)


def _softmax_update_all(acc_ref, m_ref, heads, scores, vt_augs):
    m_old = [m_ref[hh][0:1, :] for hh in heads]
    acc_old = [acc_ref[hh] for hh in heads]
    m_new = [jnp.maximum(mo, jnp.max(s, axis=0, keepdims=True)) for mo, s in zip(m_old, scores)]
    p = [jnp.exp2(s - mn).astype(BF16) for s, mn in zip(scores, m_new)]
    pv = [_dot(va, pp) for va, pp in zip(vt_augs, p)]
    for hh, mo, mn, r, a in zip(heads, m_old, m_new, pv, acc_old):
        acc_ref[hh] = jnp.exp2(mo - mn) * a + r
        m_ref[hh] = jnp.broadcast_to(mn, m_ref.shape[1:])


def _values_t(vt_ref, first_blk, nblk, hh):
    rows = slice(hh * HEAD_DIM, (hh + 1) * HEAD_DIM)
    v = jnp.concatenate([vt_ref[0, first_blk + r, rows, :] for r in range(nblk)], axis=1)
    return jnp.concatenate([v, jnp.ones((ONES_ROWS, v.shape[1]), v.dtype)], axis=0)


def _softmax_pipelined(n_steps, qk_pair, vt_ref, acc_ref, m_ref, s_ref, p_ref, a_ref):
    pairs = ((0, 1), (2, 3))
    s_ref[0] = jnp.full(s_ref.shape[1:], -jnp.inf, F32)
    p_ref[0] = jnp.zeros(p_ref.shape[1:], BF16)
    a_ref[0] = jnp.ones(a_ref.shape[1:], F32)

    def softmax(m_old, s):
        m_new = jnp.maximum(m_old, jnp.max(s, axis=0, keepdims=True))
        return jnp.exp2(s - m_new).astype(BF16), jnp.exp2(m_old - m_new), m_new

    def half_step(j, rd, wr, with_scores):
        prev = jnp.maximum(j - 1, 0) * MLA_WIDE
        m_old = [m_ref[hh][0:1, :] for hh in range(GROUP_HEADS)]
        acc_old = [acc_ref[hh] for hh in range(GROUP_HEADS)]
        s1_prev = [s_ref[rd, n] for n in range(2)]
        p0_prev = [p_ref[rd, n] for n in range(2)]
        a0_prev = [a_ref[rd, n][0:1, :] for n in range(2)]
        vt = [_values_t(vt_ref, prev, MLA_WIDE, hh) for hh in range(GROUP_HEADS)]

        acc = [a0_prev[n] * acc_old[hh] + _dot(vt[hh], p0_prev[n]) for n, hh in enumerate(pairs[0])]
        s0 = qk_pair(0, j * MLA_WIDE) if with_scores else None
        r1 = [softmax(m_old[hh], s1_prev[n]) for n, hh in enumerate(pairs[1])]
        s1 = qk_pair(1, j * MLA_WIDE) if with_scores else None
        acc += [r1[n][1] * acc_old[hh] + _dot(vt[hh], r1[n][0]) for n, hh in enumerate(pairs[1])]
        m_new = [None, None, r1[0][2], r1[1][2]]
        if with_scores:
            r0 = [softmax(m_old[hh], s0[n]) for n, hh in enumerate(pairs[0])]
            m_new[0:2] = [r0[0][2], r0[1][2]]

        for hh in range(GROUP_HEADS):
            acc_ref[hh] = acc[hh]
            if m_new[hh] is not None:
                m_ref[hh] = jnp.broadcast_to(m_new[hh], m_ref.shape[1:])
        if with_scores:
            for n in range(2):
                p_ref[wr, n] = r0[n][0]
                a_ref[wr, n] = jnp.broadcast_to(r0[n][1], a_ref.shape[2:])
                s_ref[wr, n] = s1[n]

    def body(u, carry):
        half_step(2 * u, 0, 1, True)
        half_step(2 * u + 1, 1, 0, True)
        return carry

    lax.fori_loop(0, n_steps // 2, body, 0)

    @pl.when(n_steps % 2 == 1)
    def _():
        half_step(n_steps - 1, 0, 1, True)
        half_step(n_steps, 1, None, False)

    @pl.when(n_steps % 2 == 0)
    def _():
        half_step(n_steps, 0, None, False)


def _softmax_finish(acc_ref, o_ref):
    for p in range(2):
        halves = [acc_ref[hh][0:HEAD_DIM, :] / acc_ref[hh][HEAD_DIM:HEAD_DIM + 1, :] for hh in (2 * p, 2 * p + 1)]
        o_ref[0, :, p * SEG:(p + 1) * SEG] = jnp.concatenate(halves, axis=0).T.astype(o_ref.dtype)


def _mla_kernel(q_ref, k_ref, vt_ref, o_ref, acc_ref, m_ref, s_ref, p_ref, a_ref):
    T = MLA_T
    tk = MLA_WIDE * MOBA_BLOCK
    i = pl.program_id(1)
    key = lax.broadcasted_iota(jnp.int32, (tk, T), 0)
    qry = lax.broadcasted_iota(jnp.int32, (tk, T), 1)
    heads = range(GROUP_HEADS)

    def scores(hh, first_blk):
        ks = pl.ds(pl.multiple_of(first_blk * MOBA_BLOCK, MOBA_BLOCK), tk)
        return _dot_nt(k_ref[0, ks, hh * SEG:(hh + 1) * SEG], q_ref[0, :, hh * SEG:(hh + 1) * SEG])

    m_ref[...] = jnp.full(m_ref.shape, NEG, F32)
    acc_ref[...] = jnp.zeros(acc_ref.shape, F32)
    _softmax_pipelined(i * (T // tk), lambda pair, blk: [scores(2 * pair + n, blk) for n in range(2)],
                       vt_ref, acc_ref, m_ref, s_ref, p_ref, a_ref)
    for r in range(T // tk):
        first_blk = (i * (T // tk) + r) * MLA_WIDE
        s = [jnp.where(key + r * tk <= qry, scores(hh, first_blk), NEG) for hh in heads]
        _softmax_update_all(acc_ref, m_ref, heads, s, [_values_t(vt_ref, first_blk, MLA_WIDE, hh) for hh in heads])
    _softmax_finish(acc_ref, o_ref)


def _softmax_scratch(T):
    tk = MLA_WIDE * MOBA_BLOCK
    assert T % tk == 0
    return [pltpu.VMEM((GROUP_HEADS, HEAD_DIM + ONES_ROWS, T), F32), pltpu.VMEM((GROUP_HEADS, 8, T), F32),
            pltpu.VMEM((2, 2, tk, T), F32), pltpu.VMEM((2, 2, tk, T), BF16), pltpu.VMEM((2, 2, 8, T), F32)]


def _mla_attn(q, k, vt):
    B, S, _ = k.shape
    T = MLA_T
    return pl.pallas_call(
        _mla_kernel,
        grid=(B, S // T),
        in_specs=[pl.BlockSpec((1, T, Q_PAD), lambda b, i: (b, i, 0)),
                  pl.BlockSpec((1, S, Q_PAD), lambda b, i: (b, 0, 0)),
                  pl.BlockSpec((1,) + vt.shape[1:], lambda b, i: (b, 0, 0, 0))],
        out_specs=pl.BlockSpec((1, T, GROUP_WIDTH), lambda b, i: (b, i, 0)),
        out_shape=jax.ShapeDtypeStruct((B, S, GROUP_WIDTH), BF16),
        scratch_shapes=_softmax_scratch(T),
        compiler_params=_cparams(("parallel", "arbitrary")),
        name="mla_attn",
    )(q, k, vt)


def _swa_kernel(sink_ref, q_ref, k_ref, v_ref, o_ref):
    W = WINDOW
    i = pl.program_id(1)
    lane = lax.broadcasted_iota(jnp.int32, (1, LANES), 1)
    lo_half = lane < HEAD_DIM
    heads = range(GROUP_HEADS)
    sinks = [sink_ref[hh] for hh in heads]
    chains, ks, vs, bands = [], [], [], []
    for b in range(q_ref.shape[1] // W):
        q0 = i * q_ref.shape[1] + b * W
        start = pl.multiple_of(jnp.maximum(q0 - W, 0), W)
        ks.append(k_ref[0, pl.ds(start, 2 * W), :])
        vs.append(v_ref[0, pl.ds(start, 2 * W), :])
        qpos = q0 + lax.broadcasted_iota(jnp.int32, (W, 2 * W), 0)
        kpos = start + lax.broadcasted_iota(jnp.int32, (W, 2 * W), 1)
        bands.append(jnp.logical_and(kpos <= qpos, qpos - kpos < W))
        chains += [(b, hh) for hh in heads]
    s = [jnp.where(bands[b], _dot_nt(q_ref[0, b * W:(b + 1) * W, hh * SEG:(hh + 1) * SEG], ks[b]), NEG)
         for b, hh in chains]
    m = [jnp.maximum(jnp.max(ss, axis=1, keepdims=True), sinks[hh]) for ss, (b, hh) in zip(s, chains)]
    p = [jnp.exp(ss - mm) for ss, mm in zip(s, m)]
    denom = [jnp.sum(pp, axis=1, keepdims=True) + jnp.exp(sinks[hh] - mm) for pp, mm, (b, hh) in zip(p, m, chains)]
    outs = [_dot(pp.astype(BF16), vs[b]) / dd for pp, dd, (b, hh) in zip(p, denom, chains)]
    for b in range(q_ref.shape[1] // W):
        o = outs[b * GROUP_HEADS:(b + 1) * GROUP_HEADS]
        rows = slice(b * W, (b + 1) * W)
        o_ref[0, rows, 0:SEG] = jnp.where(lo_half, o[0], pltpu.roll(o[1], HEAD_DIM, 1)).astype(o_ref.dtype)
        o_ref[0, rows, SEG:2 * SEG] = jnp.where(lo_half, pltpu.roll(o[2], HEAD_DIM, 1), o[3]).astype(o_ref.dtype)


def _swa_attn(q, k, v, sinks):
    B, S, _ = k.shape
    W = SWA_T
    full = pl.BlockSpec((1, S, 2 * HEAD_DIM), lambda b, i: (b, 0, 0))
    return pl.pallas_call(
        _swa_kernel,
        grid=(B, S // W),
        in_specs=[pl.BlockSpec(memory_space=pltpu.SMEM),
                  pl.BlockSpec((1, W, Q_PAD), lambda b, i: (b, i, 0)), full, full],
        out_specs=pl.BlockSpec((1, W, GROUP_WIDTH), lambda b, i: (b, i, 0)),
        out_shape=jax.ShapeDtypeStruct((B, S, GROUP_WIDTH), BF16),
        compiler_params=_cparams(("parallel", "parallel")),
        name="swa_attn",
    )(sinks, q, k, v)


def _moba_kernel(q_ref, q32_ref, k_ref, vt_ref, kmean_ref, o_ref, acc_ref, m_ref, s_ref, p_ref, a_ref, qaug_ref):
    T = q_ref.shape[1]
    BLK = MOBA_BLOCK
    tk = MLA_WIDE * BLK
    i = pl.program_id(1)
    nbr = kmean_ref.shape[1]
    key = lax.broadcasted_iota(jnp.int32, (tk, T), 0)
    qry = lax.broadcasted_iota(jnp.int32, (tk, T), 1)
    lo_half = lax.broadcasted_iota(jnp.int32, (1, LANES), 1) < HEAD_DIM
    blk = lax.broadcasted_iota(jnp.int32, (nbr, T), 0).astype(F32)
    kblk = lax.broadcasted_iota(jnp.int32, (BLK, LANES), 1)
    own = (i * (T // BLK) + lax.broadcasted_iota(jnp.int32, (1, T), 1) // BLK).astype(F32)

    m_ref[...] = jnp.full(m_ref.shape, NEG, F32)
    acc_ref[...] = jnp.zeros(acc_ref.shape, F32)
    heads = range(GROUP_HEADS)
    pair_sl = [slice((hh // 2) * SEG, (hh // 2 + 1) * SEG) for hh in heads]

    g = []
    for hh in heads:
        head_lanes = lo_half if hh % 2 == 0 else jnp.logical_not(lo_half)
        qh_hi, qh_lo = _split_bf16(jnp.where(head_lanes, q32_ref[0, :, pair_sl[hh]], 0.0))
        km_hi, km_lo = _split_bf16(kmean_ref[0, :, pair_sl[hh]])
        gate = _dot_nt(km_hi, qh_hi) + (_dot_nt(km_lo, qh_hi) + _dot_nt(km_hi, qh_lo))
        g.append(jnp.where(blk < own, gate, NEG))
    sel = [jnp.where(blk == own, 1.0, 0.0) for _ in heads]
    for t in range(MOBA_TOPK):
        mx = [jnp.max(gg, axis=0, keepdims=True) for gg in g]
        first = [jnp.min(jnp.where(gg == m, blk, float(nbr)), axis=0, keepdims=True)
                 for gg, m in zip(g, mx)]
        hit = [blk == f for f in first]
        sel = [jnp.where(jnp.logical_and(h, t < own), 1.0, sl) for h, sl in zip(hit, sel)]
        g = [jnp.where(h, -jnp.inf, gg) for h, gg in zip(hit, g)]
    for hh in heads:
        neg_t = jnp.concatenate([(1.0 - sel[hh]) * NEG, jnp.zeros((LANES - nbr, T), F32)], axis=0)
        qaug_ref[hh, :, 0:SEG] = q_ref[0, :, hh * SEG:(hh + 1) * SEG]
        qaug_ref[hh, :, SEG:2 * SEG] = neg_t.T.astype(BF16)

    def scores_pair(pair, n0):
        ks = pl.ds(pl.multiple_of(n0 * BLK, BLK), MLA_WIDE * BLK)
        onehot = jnp.concatenate([jnp.where(kblk == n0 + r, 1.0, 0.0) for r in range(MLA_WIDE)],
                                 axis=0).astype(BF16)
        k_aug = jnp.concatenate([k_ref[0, ks, pair * SEG:(pair + 1) * SEG], onehot], axis=1)
        return [_dot_nt(k_aug, qaug_ref[2 * pair + n]) for n in range(2)]

    _softmax_pipelined(i * (T // tk), scores_pair, vt_ref, acc_ref, m_ref, s_ref, p_ref, a_ref)
    for r in range(T // tk):
        n0 = (i * (T // tk) + r) * MLA_WIDE
        s = [jnp.where(key + r * tk <= qry, ss, NEG) for ss in scores_pair(0, n0) + scores_pair(1, n0)]
        _softmax_update_all(acc_ref, m_ref, heads, s, [_values_t(vt_ref, n0, MLA_WIDE, hh) for hh in heads])
    _softmax_finish(acc_ref, o_ref)


def _moba_attn(q, q32, k, vt, kmean):
    B, S, _ = k.shape
    T = MLA_T
    n_blocks = S // MOBA_BLOCK
    assert n_blocks <= LANES
    nbr = -(-n_blocks // 16) * 16
    kmean = jnp.pad(kmean.reshape(B, n_blocks, GROUP_WIDTH), ((0, 0), (0, nbr - n_blocks), (0, 0)))
    return pl.pallas_call(
        _moba_kernel,
        grid=(B, S // T),
        in_specs=[pl.BlockSpec((1, T, Q_PAD), lambda b, i: (b, i, 0)),
                  pl.BlockSpec((1, T, GROUP_WIDTH), lambda b, i: (b, i, 0)),
                  pl.BlockSpec((1, S, GROUP_WIDTH), lambda b, i: (b, 0, 0)),
                  pl.BlockSpec((1,) + vt.shape[1:], lambda b, i: (b, 0, 0, 0)),
                  pl.BlockSpec((1, nbr, GROUP_WIDTH), lambda b, i: (b, 0, 0))],
        out_specs=pl.BlockSpec((1, T, GROUP_WIDTH), lambda b, i: (b, i, 0)),
        out_shape=jax.ShapeDtypeStruct((B, S, GROUP_WIDTH), BF16),
        scratch_shapes=_softmax_scratch(T) + [pltpu.VMEM((GROUP_HEADS, T, 2 * SEG), BF16)],
        compiler_params=_cparams(("parallel", "arbitrary")),
        name="moba_attn",
    )(q, q32, k, vt, kmean)


def _mix_groups(x, group_refs, gn_ref, wo_ref):
    for g, ref in enumerate(group_refs):
        o = ref[0].astype(F32)
        ms = jnp.mean(o * o, axis=-1, keepdims=True)
        n = (o * lax.rsqrt(ms + EPS) * gn_ref[:, g * GROUP_WIDTH:(g + 1) * GROUP_WIDTH]).astype(BF16)
        x = x + _dot(n, wo_ref[g * GROUP_WIDTH:(g + 1) * GROUP_WIDTH, :])
    return x


def _ffn_kernel(x_ref, oa_ref, ob_ref, oc_ref, od_ref, p_ref, gn_ref, wo_ref,
                fn_ref, wup_ref, cw_ref, cb_ref, wdn_ref, pproj_ref, pgate_ref, y_ref,
                halo_ref, ubuf0_ref, ubuf1_ref, h_ref):
    ts = x_ref.shape[1]
    fc2 = wup_ref.shape[2]
    fc = fc2 // 2
    n_chunks = wup_ref.shape[0]
    i = pl.program_id(1)

    @pl.when(i == 0)
    def _():
        halo_ref[...] = jnp.zeros(halo_ref.shape, F32)

    x = _mix_groups(x_ref[0], (oa_ref, ob_ref, oc_ref, od_ref), gn_ref, wo_ref)
    ms = jnp.mean(x * x, axis=-1, keepdims=True)
    h_ref[...] = (x * lax.rsqrt(ms + EPS) * fn_ref[...]).astype(BF16)
    y_ref[0] = x

    def up(c, ubuf_ref):
        u = _dot(h_ref[...], wup_ref[c])
        ubuf_ref[0:HALO, :] = halo_ref[c]
        ubuf_ref[HALO:HALO + ts, :] = u
        halo_ref[c] = u[ts - HALO:ts, :]

    def down(c, ubuf_ref):
        cw = cw_ref[c]
        y = (cw[0:1, :] * ubuf_ref[HALO - 2:HALO - 2 + ts, :] + cw[1:2, :] * ubuf_ref[HALO - 1:HALO - 1 + ts, :]
             + cw[2:3, :] * ubuf_ref[HALO:HALO + ts, :] + cb_ref[c])
        ya = y[:, :fc]
        g = ya * (1.0 / (1.0 + jnp.exp(-ya))) * y[:, fc:]
        y_ref[0] += _dot(g.astype(BF16), wdn_ref[c])

    up(0, ubuf0_ref)

    def pair(cc, carry):
        up(2 * cc + 1, ubuf1_ref)
        down(2 * cc, ubuf0_ref)
        up(2 * cc + 2, ubuf0_ref)
        down(2 * cc + 1, ubuf1_ref)
        return carry

    assert n_chunks % 2 == 1
    lax.fori_loop(0, n_chunks // 2, pair, 0)
    down(n_chunks - 1, ubuf0_ref)
    x2 = y_ref[0]
    gate = _dot(x2.astype(BF16), pgate_ref[...])
    emb = _dot(p_ref[0, 0].astype(BF16), pproj_ref[...])
    y_ref[0] = x2 + emb * (1.0 / (1.0 + jnp.exp(-gate)))


def _ffn_weights(ffn_norm, w_up, conv_w, conv_b, w_down, ple_proj, ple_gate):
    fc = FFN_FC
    nc = D_FF // fc
    D = w_up.shape[0]
    pair = lambda a: jnp.concatenate([a[..., :D_FF].reshape(a.shape[:-1] + (nc, fc)),
                                      a[..., D_FF:].reshape(a.shape[:-1] + (nc, fc))], axis=-1)
    wup = jnp.moveaxis(pair(w_up), 1, 0).astype(BF16)
    cw = jnp.pad(jnp.moveaxis(pair(conv_w), 1, 0), ((0, 0), (0, 8 - CONV_WIDTH), (0, 0)))
    cb = jnp.moveaxis(pair(conv_b.reshape(1, -1)), 1, 0)
    wdn = w_down.reshape(nc, fc, D).astype(BF16)
    return ffn_norm.reshape(1, D), wup, cw, cb, wdn, ple_proj.astype(BF16), ple_gate.astype(BF16)


def _ffn(x, groups, p, layer, group_norm, w_o, weights):
    B, S, D = x.shape
    ts = FFN_TS
    fn, wup, cw, cb, wdn, pproj, pgate = weights
    consts = (group_norm.reshape(1, D), w_o) + tuple(weights)
    nc, _, fc2 = wup.shape
    tok = lambda w: pl.BlockSpec((1, ts, w), lambda b, i: (b, i, 0))
    once = lambda a: pl.BlockSpec(a.shape, lambda *_: (0,) * a.ndim, pipeline_mode=pl.Buffered(1))
    return pl.pallas_call(
        _ffn_kernel,
        grid=(B, S // ts),
        in_specs=([tok(D)] + [tok(GROUP_WIDTH)] * 4
                  + [pl.BlockSpec((1, 1, ts, PLE_DIM), lambda b, i: (layer, b, i, 0))] + [once(a) for a in consts]),
        out_specs=tok(D),
        out_shape=jax.ShapeDtypeStruct((B, S, D), F32),
        scratch_shapes=[pltpu.VMEM((nc, HALO, fc2), F32), pltpu.VMEM((ts + HALO, fc2), F32),
                        pltpu.VMEM((ts + HALO, fc2), F32), pltpu.VMEM((ts, D), BF16)],
        compiler_params=_cparams(("arbitrary", "arbitrary")),
        name="ffn_ple",
    )(x, *groups, p, *consts)


def kernel(x, p, positions, attn_norm, w_in, mla_q_norm, mla_w_uq, mla_kv_norm, mla_w_ukv, mla_q_gain, mla_k_gain, swa_q_gain, swa_k_gain, swa_sinks, moba_q_gain, moba_k_gain, group_norm, w_o, ffn_norm, w_up, conv_w, conv_b, w_down, ple_proj, ple_gate):
    B, S, D = x.shape
    depth = w_in.shape[0]
    assert D == D_MODEL and S % max(MLA_T, PREP_TS, FFN_TS, SWA_T, SB_SUB * SB_T) == 0
    tables = _rope_tables(positions)
    for i in range(depth):
        pw = _prep_weights(w_in[i], mla_q_norm[i], mla_w_uq[i], mla_kv_norm[i], mla_w_ukv[i], mla_q_gain[i],
                           mla_k_gain[i], swa_q_gain[i], swa_k_gain[i], moba_q_gain[i], moba_k_gain[i])
        (qa, ka, va, qm, km, vm, qc, kc, vc, qd, qd32, kd, vd, kmean) = _prep(x, attn_norm[i], tables, pw)
        o_a = _sb_attn(qa, ka, va)
        o_b = _mla_attn(qm, km, vm)
        o_c = _swa_attn(qc, kc, vc, swa_sinks[i])
        o_d = _moba_attn(qd, qd32, kd, vd, kmean)
        fw = _ffn_weights(ffn_norm[i], w_up[i], conv_w[i], conv_b[i], w_down[i], ple_proj[i], ple_gate[i])
        x = _ffn(x, (o_a, o_b, o_c, o_d), p, i, group_norm[i], w_o[i].astype(BF16), fw)
    return x
```

```python
import functools

import jax
import jax.numpy as jnp
from jax import lax
from jax.experimental import pallas as pl
from jax.experimental.pallas import tpu as pltpu

F32 = jnp.float32
BF16 = jnp.bfloat16

D_MODEL = 1024
HEAD_DIM = 64
GROUP_HEADS = 4
GROUP_WIDTH = GROUP_HEADS * HEAD_DIM
N_GROUPS = 4
ROPE_THETA = 10000.0
EPS = 1e-6
NEG = -1e30
LOG2E = 1.4426950408889634

MLA_Q_RANK = 192
MLA_KV_RANK = 128
MLA_NOPE = 64
MLA_ROPE = 32
MLA_V = 64
MLA_QK = MLA_NOPE + MLA_ROPE
SWA_KV_HEADS = 2
WINDOW = 128
MOBA_BLOCK = 256
MOBA_TOPK = 3
D_FF = 2816
CONV_WIDTH = 3
PLE_DIM = 256

SB_COLS = 3 * GROUP_WIDTH
MLA_COLS = MLA_Q_RANK + MLA_KV_RANK + MLA_ROPE
SWA_COLS = GROUP_WIDTH + 2 * SWA_KV_HEADS * HEAD_DIM
MOBA_COLS = 3 * GROUP_WIDTH
OFF_SB = 0
OFF_MLA = OFF_SB + SB_COLS
OFF_SWA = OFF_MLA + MLA_COLS
OFF_MOBA = OFF_SWA + SWA_COLS

LANES = 128
SEG = LANES
Q_PAD = GROUP_HEADS * SEG
CQ_PAD = 2 * LANES
KV_W = SWA_KV_HEADS * HEAD_DIM
COL_QA, COL_KA, COL_VA = 0, GROUP_WIDTH, 2 * GROUP_WIDTH
COL_CQ = COL_VA + GROUP_WIDTH
COL_CKV = COL_CQ + CQ_PAD
COL_KPE = COL_CKV + MLA_KV_RANK
COL_QC = COL_KPE + SEG
COL_KC = COL_QC + GROUP_WIDTH
COL_VC = COL_KC + KV_W
COL_QD = COL_VC + KV_W
COL_KD = COL_QD + GROUP_WIDTH
COL_VD = COL_KD + GROUP_WIDTH
N_IN_PAD = COL_VD + GROUP_WIDTH
VMEM_LIMIT = 56 * 1024 * 1024

PREP_TS = 512
FFN_TS = 512
FFN_FC = 256
SWA_T = 512
SB_T = 128
SB_SUB = 4
SB_EXIT = -45.0
MLA_WIDE = 2
MLA_T = 1024
HALO = 8
ONES_ROWS = 16


def _dot(a, b):
    return jnp.dot(a, b, preferred_element_type=F32)


def _dot_nt(a, b):
    return lax.dot_general(a, b, (((1,), (1,)), ((), ())), preferred_element_type=F32)


def _split_bf16(x):
    hi = x.astype(BF16)
    lo = (x - hi.astype(F32)).astype(BF16)
    return hi, lo


def _cparams(sem):
    return pltpu.CompilerParams(dimension_semantics=sem, vmem_limit_bytes=VMEM_LIMIT)


def _const_spec(shape):
    nd = len(shape)
    return pl.BlockSpec(shape, lambda *_: (0,) * nd)


def _rope_tables_kernel(pos_ref, inv_ref, c64_ref, s64_ref, cm_ref, sm_ref):
    pos = pos_ref[0].astype(F32)
    lane = lax.broadcasted_iota(jnp.int32, (1, LANES), 1)
    h64, hm = HEAD_DIM // 2, MLA_ROPE // 2
    ang = pos * inv_ref[0:1, :]
    c, s = jnp.cos(ang), jnp.sin(ang)
    c0, s0 = jnp.where(lane < h64, c, 0.0), jnp.where(lane < h64, s, 0.0)
    spread = lambda v: v + pltpu.roll(v, h64, 1) + pltpu.roll(v, 2 * h64, 1) + pltpu.roll(v, 3 * h64, 1)
    first = (lane & (HEAD_DIM - 1)) < h64
    c64_ref[0] = spread(c0)
    s64_ref[0] = jnp.where(first, -spread(s0), spread(s0))
    in_m = (lane >= h64) & (lane < h64 + hm)
    ca, sa = jnp.where(in_m, c, 0.0), jnp.where(in_m, s, 0.0)
    lo_m = (lane >= MLA_NOPE) & (lane < MLA_NOPE + hm)
    hi_m = (lane >= MLA_NOPE + hm) & (lane < MLA_QK)
    to_lo = lambda v: pltpu.roll(v, MLA_NOPE - h64, 1)
    to_hi = lambda v: pltpu.roll(v, MLA_NOPE - h64 + hm, 1)
    cm_ref[0] = jnp.where(lo_m, to_lo(ca), jnp.where(hi_m, to_hi(ca), 1.0))
    sm_ref[0] = jnp.where(lo_m, -to_lo(sa), jnp.where(hi_m, to_hi(sa), 0.0))


def _rope_tables(positions):
    B, S = positions.shape
    ts = 512
    half64 = HEAD_DIM // 2
    halfm = MLA_ROPE // 2
    inv64 = ROPE_THETA ** (-jnp.arange(half64, dtype=F32) / half64)
    invm = ROPE_THETA ** (-jnp.arange(halfm, dtype=F32) / halfm)
    row0 = jnp.concatenate([inv64, invm, jnp.zeros((LANES - half64 - halfm,), F32)])
    inv = jnp.zeros((8, LANES), F32).at[0].set(row0)
    tab = jax.ShapeDtypeStruct((B, S, LANES), F32)
    spec = pl.BlockSpec((1, ts, LANES), lambda b, i: (b, i, 0))
    return pl.pallas_call(
        _rope_tables_kernel,
        grid=(B, S // ts),
        in_specs=[pl.BlockSpec((1, ts, 1), lambda b, i: (b, i, 0)), _const_spec((8, LANES))],
        out_specs=[spec] * 4,
        out_shape=[tab] * 4,
        compiler_params=_cparams(("parallel", "parallel")),
        name="rope_tables",
    )(positions.reshape(B, S, 1), inv)


def _rope64(x, cos, sin_signed, lane):
    first = (lane & (HEAD_DIM - 1)) < (HEAD_DIM // 2)
    partner = jnp.where(first, pltpu.roll(x, LANES - HEAD_DIM // 2, 1), pltpu.roll(x, HEAD_DIM // 2, 1))
    return x * cos + partner * sin_signed


def _rope_mla(x, cos, sin_signed, lane):
    first = lane < MLA_NOPE + MLA_ROPE // 2
    partner = jnp.where(first, pltpu.roll(x, LANES - MLA_ROPE // 2, 1), pltpu.roll(x, MLA_ROPE // 2, 1))
    return x * cos + partner * sin_signed


def _ms64(x, bsel):
    return _dot(jnp.concatenate(_split_bf16(x * x), axis=1), bsel) * (1.0 / HEAD_DIM)


def _prep_kernel(x_ref, an_ref, win_ref, c64_ref, s64_ref, cm_ref, sm_ref,
                 qn_ref, wuq_ref, kvn_ref, wukv_ref, qg_ref, kg_ref, g64_ref,
                 qa_ref, ka_ref, va_ref, qm_ref, km_ref, vm_ref, qc_ref, kc_ref, vc_ref,
                 qd_ref, qd32_ref, kd_ref, vd_ref, kmean_ref):
    ts = x_ref.shape[1]
    subs = [slice(r0, r0 + MOBA_BLOCK) for r0 in range(0, ts, MOBA_BLOCK)]
    projs = []
    for rows in subs:
        x = x_ref[0, rows, :]
        ms = jnp.mean(x * x, axis=-1, keepdims=True)
        h = (x * lax.rsqrt(ms + EPS) * an_ref[...]).astype(BF16)
        projs.append(_dot(h, win_ref[...]))

    lane = lax.broadcasted_iota(jnp.int32, (1, LANES), 1)
    lo_half = lane < HEAD_DIM
    r = (lax.broadcasted_iota(jnp.int32, (2 * LANES, LANES), 0) & (LANES - 1)) // HEAD_DIM
    c = lax.broadcasted_iota(jnp.int32, (2 * LANES, LANES), 1) // HEAD_DIM
    bsel = jnp.where(r == c, 1.0, 0.0).astype(BF16)
    scale64 = HEAD_DIM ** -0.5
    scale_m = MLA_QK ** -0.5 * LOG2E
    for rows, proj in zip(subs, projs):
        _prep_finish(proj, rows, lane, lo_half, bsel, scale64, scale_m, c64_ref, s64_ref, cm_ref, sm_ref,
                     qn_ref, wuq_ref, kvn_ref, wukv_ref, qg_ref, kg_ref, g64_ref,
                     qa_ref, ka_ref, va_ref, qm_ref, km_ref, vm_ref, qc_ref, kc_ref, vc_ref,
                     qd_ref, qd32_ref, kd_ref, vd_ref, kmean_ref)


def _prep_finish(proj, rows, lane, lo_half, bsel, scale64, scale_m, c64_ref, s64_ref, cm_ref, sm_ref,
                 qn_ref, wuq_ref, kvn_ref, wukv_ref, qg_ref, kg_ref, g64_ref,
                 qa_ref, ka_ref, va_ref, qm_ref, km_ref, vm_ref, qc_ref, kc_ref, vc_ref,
                 qd_ref, qd32_ref, kd_ref, vd_ref, kmean_ref):
    blk = rows.start // MOBA_BLOCK
    c64, s64, cm, sm = c64_ref[0, rows, :], s64_ref[0, rows, :], cm_ref[0, rows, :], sm_ref[0, rows, :]

    def seg(off, j):
        return proj[:, off + j * SEG: off + (j + 1) * SEG]

    def store_pair(ref, val, seg_lo, seg_hi):
        ref[0, rows, seg_lo * SEG:(seg_lo + 1) * SEG] = jnp.where(lo_half, val, 0.0).astype(BF16)
        ref[0, rows, seg_hi * SEG:(seg_hi + 1) * SEG] = jnp.where(lo_half, 0.0, val).astype(BF16)

    for p in range(2):
        store_pair(qa_ref, seg(COL_QA, p) * scale64, 2 * p, 2 * p + 1)
    ka_ref[0, rows, :] = proj[:, COL_KA:COL_VA].astype(BF16)
    va_ref[0, rows, :] = proj[:, COL_VA:COL_CQ].astype(BF16)

    cq = proj[:, COL_CQ:COL_CKV]
    cqn = (cq * lax.rsqrt(jnp.sum(cq * cq, axis=-1, keepdims=True) * (1.0 / MLA_Q_RANK) + EPS)
           * qn_ref[...]).astype(BF16)
    qm_raw = _dot(cqn, wuq_ref[...])
    ckv = proj[:, COL_CKV:COL_KPE]
    ckvn = (ckv * lax.rsqrt(jnp.mean(ckv * ckv, axis=-1, keepdims=True) + EPS) * kvn_ref[...]).astype(BF16)
    kv = _dot(ckvn, wukv_ref[...])
    kpe = proj[:, COL_KPE:COL_QC]

    def store_transposed(ref, v):
        ref[0, blk] = v.T.astype(BF16)

    store_transposed(vm_ref, kv[:, Q_PAD:])
    for hh in range(GROUP_HEADS):
        sl = slice(hh * SEG, (hh + 1) * SEG)
        qs = qm_raw[:, sl]
        qs = qs * lax.rsqrt(jnp.sum(qs * qs, axis=-1, keepdims=True) * (1.0 / MLA_QK) + EPS) * qg_ref[...]
        qm_ref[0, rows, sl] = (_rope_mla(qs, cm, sm, lane) * scale_m).astype(BF16)
        ks = kv[:, sl] + kpe
        ks = ks * lax.rsqrt(jnp.sum(ks * ks, axis=-1, keepdims=True) * (1.0 / MLA_QK) + EPS) * kg_ref[...]
        km_ref[0, rows, sl] = _rope_mla(ks, cm, sm, lane).astype(BF16)

    def norm_rope64(v, gain_row):
        v = v * lax.rsqrt(_ms64(v, bsel) + EPS) * g64_ref[gain_row:gain_row + 1, :]
        return _rope64(v, c64, s64, lane)

    qa_pair = norm_rope64(seg(COL_QC, 0), 0) * scale64
    qb_pair = norm_rope64(seg(COL_QC, 1), 0) * scale64
    store_pair(qc_ref, qa_pair, 0, 2)
    store_pair(qc_ref, qb_pair, 1, 3)
    kc_ref[0, rows, :] = norm_rope64(seg(COL_KC, 0), 1).astype(BF16)
    vc_ref[0, rows, :] = seg(COL_VC, 0).astype(BF16)

    for p in range(2):
        qd = norm_rope64(seg(COL_QD, p), 2)
        qd32_ref[0, rows, p * SEG:(p + 1) * SEG] = qd
        store_pair(qd_ref, qd * (scale64 * LOG2E), 2 * p, 2 * p + 1)
        kd = norm_rope64(seg(COL_KD, p), 3)
        kd_ref[0, rows, p * SEG:(p + 1) * SEG] = kd.astype(BF16)
        kmean_ref[0, blk, :, p * SEG:(p + 1) * SEG] = jnp.mean(kd, axis=0, keepdims=True)
    store_transposed(vd_ref, proj[:, COL_VD:N_IN_PAD])


def _prep_weights(w_in, mla_q_norm, mla_w_uq, mla_kv_norm, mla_w_ukv, mla_q_gain, mla_k_gain,
                  swa_q_gain, swa_k_gain, moba_q_gain, moba_k_gain):
    D = w_in.shape[0]
    z = lambda n: jnp.zeros((D, n), F32)
    mla = w_in[:, OFF_MLA:OFF_MLA + MLA_COLS]
    swa = w_in[:, OFF_SWA:OFF_SWA + SWA_COLS]
    swa_q = swa[:, :GROUP_WIDTH].reshape(D, GROUP_HEADS, HEAD_DIM)[:, jnp.array([0, 2, 1, 3])].reshape(D, GROUP_WIDTH)
    win = jnp.concatenate([
        w_in[:, OFF_SB:OFF_SB + SB_COLS],
        mla[:, :MLA_Q_RANK], z(CQ_PAD - MLA_Q_RANK),
        mla[:, MLA_Q_RANK:MLA_Q_RANK + MLA_KV_RANK],
        z(MLA_NOPE), mla[:, MLA_Q_RANK + MLA_KV_RANK:], z(SEG - MLA_QK),
        swa_q, swa[:, GROUP_WIDTH:],
        w_in[:, OFF_MOBA:OFF_MOBA + MOBA_COLS],
    ], axis=1).astype(BF16)
    assert win.shape[1] == N_IN_PAD
    qn = jnp.pad(mla_q_norm, (0, CQ_PAD - MLA_Q_RANK)).reshape(1, CQ_PAD)
    wuq = mla_w_uq.reshape(MLA_Q_RANK, GROUP_HEADS, MLA_QK)
    wuq = jnp.pad(wuq, ((0, CQ_PAD - MLA_Q_RANK), (0, 0), (0, SEG - MLA_QK))).reshape(CQ_PAD, Q_PAD).astype(BF16)
    wukv = mla_w_ukv.reshape(MLA_KV_RANK, GROUP_HEADS, MLA_NOPE + MLA_V)
    wkn = jnp.pad(wukv[:, :, :MLA_NOPE], ((0, 0), (0, 0), (0, SEG - MLA_NOPE))).reshape(MLA_KV_RANK, Q_PAD)
    wv = wukv[:, :, MLA_NOPE:].reshape(MLA_KV_RANK, GROUP_WIDTH)
    wukv_r = jnp.concatenate([wkn, wv], axis=1).astype(BF16)
    qg = jnp.pad(mla_q_gain, (0, SEG - MLA_QK)).reshape(1, SEG)
    kg = jnp.pad(mla_k_gain, (0, SEG - MLA_QK)).reshape(1, SEG)
    g64 = jnp.zeros((8, LANES), F32)
    for row, g in enumerate((swa_q_gain, swa_k_gain, moba_q_gain, moba_k_gain)):
        g64 = g64.at[row].set(jnp.tile(g, LANES // HEAD_DIM))
    return win, qn, wuq, mla_kv_norm.reshape(1, MLA_KV_RANK), wukv_r, qg, kg, g64


def _prep(x, attn_norm, tables, weights):
    B, S, D = x.shape
    ts = PREP_TS
    win, qn, wuq, kvn, wukv, qg, kg, g64 = weights
    tok = lambda w: pl.BlockSpec((1, ts, w), lambda b, i: (b, i, 0))
    nb_t = ts // MOBA_BLOCK
    VT = "channel-major value blocks"
    GW = GROUP_WIDTH
    out_widths = [(Q_PAD, BF16), (GW, BF16), (GW, BF16),
                  (Q_PAD, BF16), (Q_PAD, BF16), (VT, BF16),
                  (Q_PAD, BF16), (KV_W, BF16), (KV_W, BF16),
                  (Q_PAD, BF16), (GW, F32), (GW, BF16), (VT, BF16)]
    vt_shape = jax.ShapeDtypeStruct((B, S // MOBA_BLOCK, GROUP_WIDTH, MOBA_BLOCK), BF16)
    vt_spec = pl.BlockSpec((1, nb_t, GROUP_WIDTH, MOBA_BLOCK), lambda b, i: (b, i, 0, 0))
    out_shape = [vt_shape if w is VT else jax.ShapeDtypeStruct((B, S, w), dt) for w, dt in out_widths]
    out_specs = [vt_spec if w is VT else tok(w) for w, _ in out_widths]
    out_shape.append(jax.ShapeDtypeStruct((B, S // MOBA_BLOCK, 1, GW), F32))
    out_specs.append(pl.BlockSpec((1, nb_t, 1, GW), lambda b, i: (b, i, 0, 0)))
    consts = [attn_norm.reshape(1, D), win]
    tail = [qn, wuq, kvn, wukv, qg, kg, g64]
    in_specs = ([tok(D)] + [_const_spec(a.shape) for a in consts] + [tok(LANES)] * 4
                + [_const_spec(a.shape) for a in tail])
    return pl.pallas_call(
        _prep_kernel,
        grid=(B, S // ts),
        in_specs=in_specs,
        out_specs=out_specs,
        out_shape=out_shape,
        compiler_params=_cparams(("parallel", "parallel")),
        name="prep",
    )(x, *consts, *tables, *tail)


def _sb_kernel(q_ref, k_ref, v_ref, o_ref, acc_ref, carry_ref):
    T = SB_T
    i = pl.program_id(1)
    row = lax.broadcasted_iota(jnp.int32, (T, T), 0)
    col = lax.broadcasted_iota(jnp.int32, (T, T), 1)
    upper = jnp.concatenate([jnp.where(row > col, 1.0, 0.0), jnp.ones((T, T), F32)], axis=1).astype(BF16)
    upper = jnp.concatenate([upper, upper], axis=0)
    causal = col < row
    lane = lax.broadcasted_iota(jnp.int32, (1, LANES), 1)

    n_sub = q_ref.shape[1] // T
    chains = [(sub, hh) for sub in range(n_sub) for hh in range(GROUP_HEADS)]

    def block(t, first):
        kb = [i * n_sub + sub - t for sub in range(n_sub)]
        ks = [pl.ds(pl.multiple_of(jnp.maximum(b, 0) * T, T), T) for b in kb]
        kv = lambda ref, sub, hh: ref[0, ks[sub], (hh // 2) * SEG:(hh // 2 + 1) * SEG]
        z = [_dot_nt(q_ref[0, sub * T:(sub + 1) * T, hh * SEG:(hh + 1) * SEG], kv(k_ref, sub, hh))
             for sub, hh in chains]
        log_beta = [jnp.minimum(zz, 0.0) - jnp.log(1.0 + jnp.exp(-jnp.abs(zz))) for zz in z]
        log_keep = [lb - zz for lb, zz in zip(log_beta, z)]
        if first:
            log_keep = [jnp.where(causal, lk, 0.0) for lk in log_keep]
        parts = [jnp.concatenate(_split_bf16(lk), axis=1) for lk in log_keep]
        sums = [_dot(hl, upper) for hl in parts]
        if first:
            w = [jnp.where(causal, jnp.exp(lb + sm[:, :T]), 0.0) for lb, sm in zip(log_beta, sums)]
            carry = [sm[:, T:] for sm in sums]
        else:
            live = [kb[sub] >= 0 for sub, hh in chains]
            old = [carry_ref[c] for c in range(len(chains))]
            w = [jnp.where(lv, jnp.exp(lb + (sm[:, :T] + cr)), 0.0) for lv, lb, sm, cr in zip(live, log_beta, sums, old)]
            carry = [jnp.where(lv, cr + sm[:, T:], NEG) for lv, sm, cr in zip(live, sums, old)]
        pv = [_dot(ww.astype(BF16), kv(v_ref, sub, hh)) for (sub, hh), ww in zip(chains, w)]
        for c in range(len(chains)):
            acc_ref[c] = pv[c] if first else acc_ref[c] + pv[c]
            carry_ref[c] = carry[c]
        top = carry[0]
        for cr in carry[1:]:
            top = jnp.maximum(top, cr)
        return jnp.max(top)

    def cond(st):
        t, mx = st
        return jnp.logical_and(t <= i * n_sub + n_sub - 1, mx > SB_EXIT)

    def body(st):
        return st[0] + 1, block(st[0], False)

    lax.while_loop(cond, body, (1, block(0, True)))

    lo_half = lane < HEAD_DIM
    for sub in range(n_sub):
        for p in range(2):
            c = sub * GROUP_HEADS + 2 * p
            o_ref[0, sub * T:(sub + 1) * T, p * SEG:(p + 1) * SEG] = jnp.where(
                lo_half, acc_ref[c], acc_ref[c + 1]).astype(o_ref.dtype)


def _sb_attn(q, k, v):
    B, S, _ = k.shape
    T = SB_T
    rows = SB_SUB * T
    n_chains = SB_SUB * GROUP_HEADS
    full = pl.BlockSpec((1, S, GROUP_WIDTH), lambda b, i: (b, 0, 0))
    return pl.pallas_call(
        _sb_kernel,
        grid=(B, S // rows),
        in_specs=[pl.BlockSpec((1, rows, Q_PAD), lambda b, i: (b, i, 0)), full, full],
        out_specs=pl.BlockSpec((1, rows, GROUP_WIDTH), lambda b, i: (b, i, 0)),
        out_shape=jax.ShapeDtypeStruct((B, S, GROUP_WIDTH), BF16),
        scratch_shapes=[pltpu.VMEM((n_chains, T, SEG), F32), pltpu.VMEM((n_chains, T, T), F32)],
        compiler_params=_cparams(("parallel", "arbitrary")),
        name="sb_attn",
    )(q, k, v)


def _softmax_update_all(acc_ref, m_ref, heads, scores, vt_augs):
    m_old = [m_ref[hh][0:1, :] for hh in heads]
    acc_old = [acc_ref[hh] for hh in heads]
    m_new = [jnp.maximum(mo, jnp.max(s, axis=0, keepdims=True)) for mo, s in zip(m_old, scores)]
    p = [jnp.exp2(s - mn).astype(BF16) for s, mn in zip(scores, m_new)]
    pv = [_dot(va, pp) for va, pp in zip(vt_augs, p)]
    for hh, mo, mn, r, a in zip(heads, m_old, m_new, pv, acc_old):
        acc_ref[hh] = jnp.exp2(mo - mn) * a + r
        m_ref[hh] = jnp.broadcast_to(mn, m_ref.shape[1:])


def _values_t(vt_ref, first_blk, nblk, hh):
    rows = slice(hh * HEAD_DIM, (hh + 1) * HEAD_DIM)
    v = jnp.concatenate([vt_ref[0, first_blk + r, rows, :] for r in range(nblk)], axis=1)
    return jnp.concatenate([v, jnp.ones((ONES_ROWS, v.shape[1]), v.dtype)], axis=0)


def _softmax_pipelined(n_steps, qk_pair, vt_ref, acc_ref, m_ref, s_ref, p_ref, a_ref):
    pairs = ((0, 1), (2, 3))
    s_ref[0] = jnp.full(s_ref.shape[1:], -jnp.inf, F32)
    p_ref[0] = jnp.zeros(p_ref.shape[1:], BF16)
    a_ref[0] = jnp.ones(a_ref.shape[1:], F32)

    def softmax(m_old, s):
        m_new = jnp.maximum(m_old, jnp.max(s, axis=0, keepdims=True))
        return jnp.exp2(s - m_new).astype(BF16), jnp.exp2(m_old - m_new), m_new

    def softmax_carried(m_old, slot, n):
        blocks = [pl.ds(r0, MOBA_BLOCK) for r0 in range(0, s_ref.shape[2], MOBA_BLOCK)]
        m_new = m_old
        for rows in blocks:
            m_new = jnp.maximum(m_new, jnp.max(s_ref[slot, n, rows, :], axis=0, keepdims=True))
        p = jnp.concatenate([jnp.exp2(s_ref[slot, n, rows, :] - m_new).astype(BF16) for rows in blocks], axis=0)
        return p, jnp.exp2(m_old - m_new), m_new

    def half_step(j, rd, wr, with_scores):
        prev = jnp.maximum(j - 1, 0) * MLA_WIDE
        m_old = [m_ref[hh][0:1, :] for hh in range(GROUP_HEADS)]
        acc_old = [acc_ref[hh] for hh in range(GROUP_HEADS)]
        p0_prev = [p_ref[rd, n] for n in range(2)]
        a0_prev = [a_ref[rd, n][0:1, :] for n in range(2)]
        vt = [_values_t(vt_ref, prev, MLA_WIDE, hh) for hh in range(GROUP_HEADS)]

        acc = [a0_prev[n] * acc_old[hh] + _dot(vt[hh], p0_prev[n]) for n, hh in enumerate(pairs[0])]
        s0 = qk_pair(0, j * MLA_WIDE) if with_scores else None
        r1 = [softmax_carried(m_old[hh], rd, n) for n, hh in enumerate(pairs[1])]
        s1 = qk_pair(1, j * MLA_WIDE) if with_scores else None
        acc += [r1[n][1] * acc_old[hh] + _dot(vt[hh], r1[n][0]) for n, hh in enumerate(pairs[1])]
        m_new = [None, None, r1[0][2], r1[1][2]]
        if with_scores:
            r0 = [softmax(m_old[hh], s0[n]) for n, hh in enumerate(pairs[0])]
            m_new[0:2] = [r0[0][2], r0[1][2]]

        for hh in range(GROUP_HEADS):
            acc_ref[hh] = acc[hh]
            if m_new[hh] is not None:
                m_ref[hh] = jnp.broadcast_to(m_new[hh], m_ref.shape[1:])
        if with_scores:
            for n in range(2):
                p_ref[wr, n] = r0[n][0]
                a_ref[wr, n] = jnp.broadcast_to(r0[n][1], a_ref.shape[2:])
                s_ref[wr, n] = s1[n]

    def body(u, carry):
        half_step(2 * u, 0, 1, True)
        half_step(2 * u + 1, 1, 0, True)
        return carry

    lax.fori_loop(0, n_steps // 2, body, 0)

    @pl.when(n_steps % 2 == 1)
    def _():
        half_step(n_steps - 1, 0, 1, True)
        half_step(n_steps, 1, None, False)

    @pl.when(n_steps % 2 == 0)
    def _():
        half_step(n_steps, 0, None, False)


def _softmax_finish(acc_ref, o_ref):
    for p in range(2):
        halves = [acc_ref[hh][0:HEAD_DIM, :] / acc_ref[hh][HEAD_DIM:HEAD_DIM + 1, :] for hh in (2 * p, 2 * p + 1)]
        o_ref[0, :, p * SEG:(p + 1) * SEG] = jnp.concatenate(halves, axis=0).T.astype(o_ref.dtype)


def _mla_kernel(q_ref, k_ref, vt_ref, o_ref, acc_ref, m_ref, s_ref, p_ref, a_ref):
    T = MLA_T
    tk = MLA_WIDE * MOBA_BLOCK
    i = pl.program_id(1)
    key = lax.broadcasted_iota(jnp.int32, (tk, T), 0)
    qry = lax.broadcasted_iota(jnp.int32, (tk, T), 1)
    heads = range(GROUP_HEADS)

    def scores(hh, first_blk):
        ks = pl.ds(pl.multiple_of(first_blk * MOBA_BLOCK, MOBA_BLOCK), tk)
        return _dot_nt(k_ref[0, ks, hh * SEG:(hh + 1) * SEG], q_ref[0, :, hh * SEG:(hh + 1) * SEG])

    m_ref[...] = jnp.full(m_ref.shape, NEG, F32)
    acc_ref[...] = jnp.zeros(acc_ref.shape, F32)
    _softmax_pipelined(i * (T // tk), lambda pair, blk: [scores(2 * pair + n, blk) for n in range(2)],
                       vt_ref, acc_ref, m_ref, s_ref, p_ref, a_ref)
    for r in range(T // tk):
        first_blk = (i * (T // tk) + r) * MLA_WIDE
        s = [jnp.where(key + r * tk <= qry, scores(hh, first_blk), NEG) for hh in heads]
        _softmax_update_all(acc_ref, m_ref, heads, s, [_values_t(vt_ref, first_blk, MLA_WIDE, hh) for hh in heads])
    _softmax_finish(acc_ref, o_ref)


def _softmax_scratch(T):
    tk = MLA_WIDE * MOBA_BLOCK
    assert T % tk == 0
    return [pltpu.VMEM((GROUP_HEADS, HEAD_DIM + ONES_ROWS, T), F32), pltpu.VMEM((GROUP_HEADS, 8, T), F32),
            pltpu.VMEM((2, 2, tk, T), F32), pltpu.VMEM((2, 2, tk, T), BF16), pltpu.VMEM((2, 2, 8, T), F32)]


def _mla_attn(q, k, vt):
    B, S, _ = k.shape
    T = MLA_T
    return pl.pallas_call(
        _mla_kernel,
        grid=(B, S // T),
        in_specs=[pl.BlockSpec((1, T, Q_PAD), lambda b, i: (b, i, 0)),
                  pl.BlockSpec((1, S, Q_PAD), lambda b, i: (b, 0, 0)),
                  pl.BlockSpec((1,) + vt.shape[1:], lambda b, i: (b, 0, 0, 0))],
        out_specs=pl.BlockSpec((1, T, GROUP_WIDTH), lambda b, i: (b, i, 0)),
        out_shape=jax.ShapeDtypeStruct((B, S, GROUP_WIDTH), BF16),
        scratch_shapes=_softmax_scratch(T),
        compiler_params=_cparams(("parallel", "arbitrary")),
        name="mla_attn",
    )(q, k, vt)


def _swa_kernel(sink_ref, q_ref, k_ref, v_ref, o_ref):
    W = WINDOW
    i = pl.program_id(1)
    lane = lax.broadcasted_iota(jnp.int32, (1, LANES), 1)
    lo_half = lane < HEAD_DIM
    heads = range(GROUP_HEADS)
    sinks = [sink_ref[hh] for hh in heads]
    chains, ks, vs, bands = [], [], [], []
    for b in range(q_ref.shape[1] // W):
        q0 = i * q_ref.shape[1] + b * W
        start = pl.multiple_of(jnp.maximum(q0 - W, 0), W)
        ks.append(k_ref[0, pl.ds(start, 2 * W), :])
        vs.append(v_ref[0, pl.ds(start, 2 * W), :])
        qpos = q0 + lax.broadcasted_iota(jnp.int32, (W, 2 * W), 0)
        kpos = start + lax.broadcasted_iota(jnp.int32, (W, 2 * W), 1)
        bands.append(jnp.logical_and(kpos <= qpos, qpos - kpos < W))
        chains += [(b, hh) for hh in heads]
    s = [jnp.where(bands[b], _dot_nt(q_ref[0, b * W:(b + 1) * W, hh * SEG:(hh + 1) * SEG], ks[b]), NEG)
         for b, hh in chains]
    m = [jnp.maximum(jnp.max(ss, axis=1, keepdims=True), sinks[hh]) for ss, (b, hh) in zip(s, chains)]
    p = [jnp.exp(ss - mm) for ss, mm in zip(s, m)]
    denom = [jnp.sum(pp, axis=1, keepdims=True) + jnp.exp(sinks[hh] - mm) for pp, mm, (b, hh) in zip(p, m, chains)]
    outs = [_dot(pp.astype(BF16), vs[b]) / dd for pp, dd, (b, hh) in zip(p, denom, chains)]
    for b in range(q_ref.shape[1] // W):
        o = outs[b * GROUP_HEADS:(b + 1) * GROUP_HEADS]
        rows = slice(b * W, (b + 1) * W)
        o_ref[0, rows, 0:SEG] = jnp.where(lo_half, o[0], pltpu.roll(o[1], HEAD_DIM, 1)).astype(o_ref.dtype)
        o_ref[0, rows, SEG:2 * SEG] = jnp.where(lo_half, pltpu.roll(o[2], HEAD_DIM, 1), o[3]).astype(o_ref.dtype)


def _swa_attn(q, k, v, sinks):
    B, S, _ = k.shape
    W = SWA_T
    full = pl.BlockSpec((1, S, 2 * HEAD_DIM), lambda b, i: (b, 0, 0))
    return pl.pallas_call(
        _swa_kernel,
        grid=(B, S // W),
        in_specs=[pl.BlockSpec(memory_space=pltpu.SMEM),
                  pl.BlockSpec((1, W, Q_PAD), lambda b, i: (b, i, 0)), full, full],
        out_specs=pl.BlockSpec((1, W, GROUP_WIDTH), lambda b, i: (b, i, 0)),
        out_shape=jax.ShapeDtypeStruct((B, S, GROUP_WIDTH), BF16),
        compiler_params=_cparams(("parallel", "parallel")),
        name="swa_attn",
    )(sinks, q, k, v)


def _moba_kernel(q_ref, q32_ref, k_ref, vt_ref, kmean_ref, o_ref, acc_ref, m_ref, s_ref, p_ref, a_ref, qaug_ref):
    T = q_ref.shape[1]
    BLK = MOBA_BLOCK
    tk = MLA_WIDE * BLK
    i = pl.program_id(1)
    nbr = kmean_ref.shape[1]
    key = lax.broadcasted_iota(jnp.int32, (tk, T), 0)
    qry = lax.broadcasted_iota(jnp.int32, (tk, T), 1)
    lo_half = lax.broadcasted_iota(jnp.int32, (1, LANES), 1) < HEAD_DIM
    blk = lax.broadcasted_iota(jnp.int32, (nbr, T), 0).astype(F32)
    kblk = lax.broadcasted_iota(jnp.int32, (BLK, LANES), 1)
    own = (i * (T // BLK) + lax.broadcasted_iota(jnp.int32, (1, T), 1) // BLK).astype(F32)

    m_ref[...] = jnp.full(m_ref.shape, NEG, F32)
    acc_ref[...] = jnp.zeros(acc_ref.shape, F32)
    heads = range(GROUP_HEADS)
    pair_sl = [slice((hh // 2) * SEG, (hh // 2 + 1) * SEG) for hh in heads]

    g = []
    for hh in heads:
        head_lanes = lo_half if hh % 2 == 0 else jnp.logical_not(lo_half)
        qh_hi, qh_lo = _split_bf16(jnp.where(head_lanes, q32_ref[0, :, pair_sl[hh]], 0.0))
        km_hi, km_lo = _split_bf16(kmean_ref[0, :, pair_sl[hh]])
        gate = _dot_nt(km_hi, qh_hi) + (_dot_nt(km_lo, qh_hi) + _dot_nt(km_hi, qh_lo))
        g.append(jnp.where(blk < own, gate, NEG))
    sel = [jnp.where(blk == own, 1.0, 0.0) for _ in heads]
    for t in range(MOBA_TOPK):
        mx = [jnp.max(gg, axis=0, keepdims=True) for gg in g]
        first = [jnp.min(jnp.where(gg == m, blk, float(nbr)), axis=0, keepdims=True)
                 for gg, m in zip(g, mx)]
        hit = [blk == f for f in first]
        sel = [jnp.where(jnp.logical_and(h, t < own), 1.0, sl) for h, sl in zip(hit, sel)]
        g = [jnp.where(h, -jnp.inf, gg) for h, gg in zip(hit, g)]
    for hh in heads:
        neg_t = jnp.concatenate([(1.0 - sel[hh]) * NEG, jnp.zeros((LANES - nbr, T), F32)], axis=0)
        qaug_ref[hh, :, 0:SEG] = q_ref[0, :, hh * SEG:(hh + 1) * SEG]
        qaug_ref[hh, :, SEG:2 * SEG] = neg_t.T.astype(BF16)

    def scores_pair(pair, n0):
        ks = pl.ds(pl.multiple_of(n0 * BLK, BLK), MLA_WIDE * BLK)
        onehot = jnp.concatenate([jnp.where(kblk == n0 + r, 1.0, 0.0) for r in range(MLA_WIDE)],
                                 axis=0).astype(BF16)
        k_aug = jnp.concatenate([k_ref[0, ks, pair * SEG:(pair + 1) * SEG], onehot], axis=1)
        return [_dot_nt(k_aug, qaug_ref[2 * pair + n]) for n in range(2)]

    _softmax_pipelined(i * (T // tk), scores_pair, vt_ref, acc_ref, m_ref, s_ref, p_ref, a_ref)
    for r in range(T // tk):
        n0 = (i * (T // tk) + r) * MLA_WIDE
        s = [jnp.where(key + r * tk <= qry, ss, NEG) for ss in scores_pair(0, n0) + scores_pair(1, n0)]
        _softmax_update_all(acc_ref, m_ref, heads, s, [_values_t(vt_ref, n0, MLA_WIDE, hh) for hh in heads])
    _softmax_finish(acc_ref, o_ref)


def _moba_attn(q, q32, k, vt, kmean):
    B, S, _ = k.shape
    T = MLA_T
    n_blocks = S // MOBA_BLOCK
    assert n_blocks <= LANES
    nbr = -(-n_blocks // 16) * 16
    kmean = jnp.pad(kmean.reshape(B, n_blocks, GROUP_WIDTH), ((0, 0), (0, nbr - n_blocks), (0, 0)))
    return pl.pallas_call(
        _moba_kernel,
        grid=(B, S // T),
        in_specs=[pl.BlockSpec((1, T, Q_PAD), lambda b, i: (b, i, 0)),
                  pl.BlockSpec((1, T, GROUP_WIDTH), lambda b, i: (b, i, 0)),
                  pl.BlockSpec((1, S, GROUP_WIDTH), lambda b, i: (b, 0, 0)),
                  pl.BlockSpec((1,) + vt.shape[1:], lambda b, i: (b, 0, 0, 0)),
                  pl.BlockSpec((1, nbr, GROUP_WIDTH), lambda b, i: (b, 0, 0))],
        out_specs=pl.BlockSpec((1, T, GROUP_WIDTH), lambda b, i: (b, i, 0)),
        out_shape=jax.ShapeDtypeStruct((B, S, GROUP_WIDTH), BF16),
        scratch_shapes=_softmax_scratch(T) + [pltpu.VMEM((GROUP_HEADS, T, 2 * SEG), BF16)],
        compiler_params=_cparams(("parallel", "arbitrary")),
        name="moba_attn",
    )(q, q32, k, vt, kmean)


def _mix_groups(x, group_refs, gn_ref, wo_ref):
    for g, ref in enumerate(group_refs):
        o = ref[0].astype(F32)
        ms = jnp.mean(o * o, axis=-1, keepdims=True)
        n = (o * lax.rsqrt(ms + EPS) * gn_ref[:, g * GROUP_WIDTH:(g + 1) * GROUP_WIDTH]).astype(BF16)
        x = x + _dot(n, wo_ref[g * GROUP_WIDTH:(g + 1) * GROUP_WIDTH, :])
    return x


def _ffn_kernel(x_ref, oa_ref, ob_ref, oc_ref, od_ref, p_ref, gn_ref, wo_ref,
                fn_ref, wup_ref, cw_ref, cb_ref, wdn_ref, pproj_ref, pgate_ref, y_ref,
                halo_ref, ubuf0_ref, ubuf1_ref, h_ref):
    ts = x_ref.shape[1]
    fc2 = wup_ref.shape[2]
    fc = fc2 // 2
    n_chunks = wup_ref.shape[0]
    i = pl.program_id(1)

    @pl.when(i == 0)
    def _():
        halo_ref[...] = jnp.zeros(halo_ref.shape, F32)

    x = _mix_groups(x_ref[0], (oa_ref, ob_ref, oc_ref, od_ref), gn_ref, wo_ref)
    ms = jnp.mean(x * x, axis=-1, keepdims=True)
    h_ref[...] = (x * lax.rsqrt(ms + EPS) * fn_ref[...]).astype(BF16)
    y_ref[0] = x

    def up(c, ubuf_ref):
        u = _dot(h_ref[...], wup_ref[c])
        ubuf_ref[0:HALO, :] = halo_ref[c]
        ubuf_ref[HALO:HALO + ts, :] = u
        halo_ref[c] = u[ts - HALO:ts, :]

    def down(c, ubuf_ref):
        cw = cw_ref[c]
        y = (cw[0:1, :] * ubuf_ref[HALO - 2:HALO - 2 + ts, :] + cw[1:2, :] * ubuf_ref[HALO - 1:HALO - 1 + ts, :]
             + cw[2:3, :] * ubuf_ref[HALO:HALO + ts, :] + cb_ref[c])
        ya = y[:, :fc]
        g = ya * (1.0 / (1.0 + jnp.exp(-ya))) * y[:, fc:]
        y_ref[0] += _dot(g.astype(BF16), wdn_ref[c])

    up(0, ubuf0_ref)

    def pair(cc, carry):
        up(2 * cc + 1, ubuf1_ref)
        down(2 * cc, ubuf0_ref)
        up(2 * cc + 2, ubuf0_ref)
        down(2 * cc + 1, ubuf1_ref)
        return carry

    assert n_chunks % 2 == 1
    lax.fori_loop(0, n_chunks // 2, pair, 0)
    down(n_chunks - 1, ubuf0_ref)
    x2 = y_ref[0]
    gate = _dot(x2.astype(BF16), pgate_ref[...])
    emb = _dot(p_ref[0, 0].astype(BF16), pproj_ref[...])
    y_ref[0] = x2 + emb * (1.0 / (1.0 + jnp.exp(-gate)))


def _ffn_weights(ffn_norm, w_up, conv_w, conv_b, w_down, ple_proj, ple_gate):
    fc = FFN_FC
    nc = D_FF // fc
    D = w_up.shape[0]
    pair = lambda a: jnp.concatenate([a[..., :D_FF].reshape(a.shape[:-1] + (nc, fc)),
                                      a[..., D_FF:].reshape(a.shape[:-1] + (nc, fc))], axis=-1)
    wup = jnp.moveaxis(pair(w_up), 1, 0).astype(BF16)
    cw = jnp.pad(jnp.moveaxis(pair(conv_w), 1, 0), ((0, 0), (0, 8 - CONV_WIDTH), (0, 0)))
    cb = jnp.moveaxis(pair(conv_b.reshape(1, -1)), 1, 0)
    wdn = w_down.reshape(nc, fc, D).astype(BF16)
    return ffn_norm.reshape(1, D), wup, cw, cb, wdn, ple_proj.astype(BF16), ple_gate.astype(BF16)


def _ffn(x, groups, p, layer, group_norm, w_o, weights):
    B, S, D = x.shape
    ts = FFN_TS
    fn, wup, cw, cb, wdn, pproj, pgate = weights
    consts = (group_norm.reshape(1, D), w_o) + tuple(weights)
    nc, _, fc2 = wup.shape
    tok = lambda w: pl.BlockSpec((1, ts, w), lambda b, i: (b, i, 0))
    once = lambda a: pl.BlockSpec(a.shape, lambda *_: (0,) * a.ndim, pipeline_mode=pl.Buffered(1))
    return pl.pallas_call(
        _ffn_kernel,
        grid=(B, S // ts),
        in_specs=([tok(D)] + [tok(GROUP_WIDTH)] * 4
                  + [pl.BlockSpec((1, 1, ts, PLE_DIM), lambda b, i: (layer, b, i, 0))] + [once(a) for a in consts]),
        out_specs=tok(D),
        out_shape=jax.ShapeDtypeStruct((B, S, D), F32),
        scratch_shapes=[pltpu.VMEM((nc, HALO, fc2), F32), pltpu.VMEM((ts + HALO, fc2), F32),
                        pltpu.VMEM((ts + HALO, fc2), F32), pltpu.VMEM((ts, D), BF16)],
        compiler_params=_cparams(("arbitrary", "arbitrary")),
        name="ffn_ple",
    )(x, *groups, p, *consts)


def kernel(x, p, positions, attn_norm, w_in, mla_q_norm, mla_w_uq, mla_kv_norm, mla_w_ukv, mla_q_gain, mla_k_gain, swa_q_gain, swa_k_gain, swa_sinks, moba_q_gain, moba_k_gain, group_norm, w_o, ffn_norm, w_up, conv_w, conv_b, w_down, ple_proj, ple_gate):
    B, S, D = x.shape
    depth = w_in.shape[0]
    assert D == D_MODEL and S % max(MLA_T, PREP_TS, FFN_TS, SWA_T, SB_SUB * SB_T) == 0
    tables = _rope_tables(positions)
    for i in range(depth):
        pw = _prep_weights(w_in[i], mla_q_norm[i], mla_w_uq[i], mla_kv_norm[i], mla_w_ukv[i], mla_q_gain[i],
                           mla_k_gain[i], swa_q_gain[i], swa_k_gain[i], moba_q_gain[i], moba_k_gain[i])
        (qa, ka, va, qm, km, vm, qc, kc, vc, qd, qd32, kd, vd, kmean) = _prep(x, attn_norm[i], tables, pw)
        o_a = _sb_attn(qa, ka, va)
        o_b = _mla_attn(qm, km, vm)
        o_c = _swa_attn(qc, kc, vc, swa_sinks[i])
        o_d = _moba_attn(qd, qd32, kd, vd, kmean)
        fw = _ffn_weights(ffn_norm[i], w_up[i], conv_w[i], conv_b[i], w_down[i], ple_proj[i], ple_gate[i])
        x = _ffn(x, (o_a, o_b, o_c, o_d), p, i, group_norm[i], w_o[i].astype(BF16), fw)
    return x
```

```python
import functools

import jax
import jax.numpy as jnp
from jax import lax
from jax.experimental import pallas as pl
from jax.experimental.pallas import tpu as pltpu

F32 = jnp.float32
BF16 = jnp.bfloat16

D_MODEL = 1024
HEAD_DIM = 64
GROUP_HEADS = 4
GROUP_WIDTH = GROUP_HEADS * HEAD_DIM
N_GROUPS = 4
ROPE_THETA = 10000.0
EPS = 1e-6
NEG = -1e30
LOG2E = 1.4426950408889634

MLA_Q_RANK = 192
MLA_KV_RANK = 128
MLA_NOPE = 64
MLA_ROPE = 32
MLA_V = 64
MLA_QK = MLA_NOPE + MLA_ROPE
SWA_KV_HEADS = 2
WINDOW = 128
MOBA_BLOCK = 256
MOBA_TOPK = 3
D_FF = 2816
CONV_WIDTH = 3
PLE_DIM = 256

SB_COLS = 3 * GROUP_WIDTH
MLA_COLS = MLA_Q_RANK + MLA_KV_RANK + MLA_ROPE
SWA_COLS = GROUP_WIDTH + 2 * SWA_KV_HEADS * HEAD_DIM
MOBA_COLS = 3 * GROUP_WIDTH
OFF_SB = 0
OFF_MLA = OFF_SB + SB_COLS
OFF_SWA = OFF_MLA + MLA_COLS
OFF_MOBA = OFF_SWA + SWA_COLS

LANES = 128
SEG = LANES
Q_PAD = GROUP_HEADS * SEG
CQ_PAD = 2 * LANES
KV_W = SWA_KV_HEADS * HEAD_DIM
COL_QA, COL_KA, COL_VA = 0, GROUP_WIDTH, 2 * GROUP_WIDTH
COL_CQ = COL_VA + GROUP_WIDTH
COL_CKV = COL_CQ + CQ_PAD
COL_KPE = COL_CKV + MLA_KV_RANK
COL_QC = COL_KPE + SEG
COL_KC = COL_QC + GROUP_WIDTH
COL_VC = COL_KC + KV_W
COL_QD = COL_VC + KV_W
COL_KD = COL_QD + GROUP_WIDTH
COL_VD = COL_KD + GROUP_WIDTH
N_IN_PAD = COL_VD + GROUP_WIDTH
VMEM_LIMIT = 56 * 1024 * 1024

PREP_TS = 512
FFN_TS = 512
FFN_FC = 256
SWA_T = 512
SB_T = 128
SB_SUB = 8
SB_EXIT = -45.0
MLA_WIDE = 2
MLA_T = 1024
HALO = 8
ONES_ROWS = 16


def _dot(a, b):
    return jnp.dot(a, b, preferred_element_type=F32)


def _dot_nt(a, b):
    return lax.dot_general(a, b, (((1,), (1,)), ((), ())), preferred_element_type=F32)


def _split_bf16(x):
    hi = x.astype(BF16)
    lo = (x - hi.astype(F32)).astype(BF16)
    return hi, lo


def _cparams(sem):
    return pltpu.CompilerParams(dimension_semantics=sem, vmem_limit_bytes=VMEM_LIMIT)


def _const_spec(shape):
    nd = len(shape)
    return pl.BlockSpec(shape, lambda *_: (0,) * nd)


def _rope_tables_kernel(pos_ref, inv_ref, c64_ref, s64_ref, cm_ref, sm_ref):
    pos = pos_ref[0].astype(F32)
    lane = lax.broadcasted_iota(jnp.int32, (1, LANES), 1)
    h64, hm = HEAD_DIM // 2, MLA_ROPE // 2
    ang = pos * inv_ref[0:1, :]
    c, s = jnp.cos(ang), jnp.sin(ang)
    c0, s0 = jnp.where(lane < h64, c, 0.0), jnp.where(lane < h64, s, 0.0)
    spread = lambda v: v + pltpu.roll(v, h64, 1) + pltpu.roll(v, 2 * h64, 1) + pltpu.roll(v, 3 * h64, 1)
    first = (lane & (HEAD_DIM - 1)) < h64
    c64_ref[0] = spread(c0)
    s64_ref[0] = jnp.where(first, -spread(s0), spread(s0))
    in_m = (lane >= h64) & (lane < h64 + hm)
    ca, sa = jnp.where(in_m, c, 0.0), jnp.where(in_m, s, 0.0)
    lo_m = (lane >= MLA_NOPE) & (lane < MLA_NOPE + hm)
    hi_m = (lane >= MLA_NOPE + hm) & (lane < MLA_QK)
    to_lo = lambda v: pltpu.roll(v, MLA_NOPE - h64, 1)
    to_hi = lambda v: pltpu.roll(v, MLA_NOPE - h64 + hm, 1)
    cm_ref[0] = jnp.where(lo_m, to_lo(ca), jnp.where(hi_m, to_hi(ca), 1.0))
    sm_ref[0] = jnp.where(lo_m, -to_lo(sa), jnp.where(hi_m, to_hi(sa), 0.0))


def _rope_tables(positions):
    B, S = positions.shape
    ts = 512
    half64 = HEAD_DIM // 2
    halfm = MLA_ROPE // 2
    inv64 = ROPE_THETA ** (-jnp.arange(half64, dtype=F32) / half64)
    invm = ROPE_THETA ** (-jnp.arange(halfm, dtype=F32) / halfm)
    row0 = jnp.concatenate([inv64, invm, jnp.zeros((LANES - half64 - halfm,), F32)])
    inv = jnp.zeros((8, LANES), F32).at[0].set(row0)
    tab = jax.ShapeDtypeStruct((B, S, LANES), F32)
    spec = pl.BlockSpec((1, ts, LANES), lambda b, i: (b, i, 0))
    return pl.pallas_call(
        _rope_tables_kernel,
        grid=(B, S // ts),
        in_specs=[pl.BlockSpec((1, ts, 1), lambda b, i: (b, i, 0)), _const_spec((8, LANES))],
        out_specs=[spec] * 4,
        out_shape=[tab] * 4,
        compiler_params=_cparams(("parallel", "parallel")),
        name="rope_tables",
    )(positions.reshape(B, S, 1), inv)


def _rope64(x, cos, sin_signed, lane):
    first = (lane & (HEAD_DIM - 1)) < (HEAD_DIM // 2)
    partner = jnp.where(first, pltpu.roll(x, LANES - HEAD_DIM // 2, 1), pltpu.roll(x, HEAD_DIM // 2, 1))
    return x * cos + partner * sin_signed


def _rope_mla(x, cos, sin_signed, lane):
    first = lane < MLA_NOPE + MLA_ROPE // 2
    partner = jnp.where(first, pltpu.roll(x, LANES - MLA_ROPE // 2, 1), pltpu.roll(x, MLA_ROPE // 2, 1))
    return x * cos + partner * sin_signed


def _ms64(x, bsel):
    return _dot(jnp.concatenate(_split_bf16(x * x), axis=1), bsel) * (1.0 / HEAD_DIM)


def _prep_kernel(x_ref, an_ref, win_ref, c64_ref, s64_ref, cm_ref, sm_ref,
                 qn_ref, wuq_ref, kvn_ref, wukv_ref, qg_ref, kg_ref, g64_ref,
                 qa_ref, ka_ref, va_ref, qm_ref, km_ref, vm_ref, qc_ref, kc_ref, vc_ref,
                 qd_ref, qd32_ref, kd_ref, vd_ref, kmean_ref):
    ts = x_ref.shape[1]
    subs = [slice(r0, r0 + MOBA_BLOCK) for r0 in range(0, ts, MOBA_BLOCK)]
    projs = []
    for rows in subs:
        x = x_ref[0, rows, :]
        ms = jnp.mean(x * x, axis=-1, keepdims=True)
        h = (x * lax.rsqrt(ms + EPS) * an_ref[...]).astype(BF16)
        projs.append(_dot(h, win_ref[...]))

    lane = lax.broadcasted_iota(jnp.int32, (1, LANES), 1)
    lo_half = lane < HEAD_DIM
    r = (lax.broadcasted_iota(jnp.int32, (2 * LANES, LANES), 0) & (LANES - 1)) // HEAD_DIM
    c = lax.broadcasted_iota(jnp.int32, (2 * LANES, LANES), 1) // HEAD_DIM
    bsel = jnp.where(r == c, 1.0, 0.0).astype(BF16)
    scale64 = HEAD_DIM ** -0.5
    scale_m = MLA_QK ** -0.5 * LOG2E
    for rows, proj in zip(subs, projs):
        _prep_finish(proj, rows, lane, lo_half, bsel, scale64, scale_m, c64_ref, s64_ref, cm_ref, sm_ref,
                     qn_ref, wuq_ref, kvn_ref, wukv_ref, qg_ref, kg_ref, g64_ref,
                     qa_ref, ka_ref, va_ref, qm_ref, km_ref, vm_ref, qc_ref, kc_ref, vc_ref,
                     qd_ref, qd32_ref, kd_ref, vd_ref, kmean_ref)


def _prep_finish(proj, rows, lane, lo_half, bsel, scale64, scale_m, c64_ref, s64_ref, cm_ref, sm_ref,
                 qn_ref, wuq_ref, kvn_ref, wukv_ref, qg_ref, kg_ref, g64_ref,
                 qa_ref, ka_ref, va_ref, qm_ref, km_ref, vm_ref, qc_ref, kc_ref, vc_ref,
                 qd_ref, qd32_ref, kd_ref, vd_ref, kmean_ref):
    blk = rows.start // MOBA_BLOCK
    c64, s64, cm, sm = c64_ref[0, rows, :], s64_ref[0, rows, :], cm_ref[0, rows, :], sm_ref[0, rows, :]

    def seg(off, j):
        return proj[:, off + j * SEG: off + (j + 1) * SEG]

    def store_pair(ref, val, seg_lo, seg_hi):
        ref[0, rows, seg_lo * SEG:(seg_lo + 1) * SEG] = jnp.where(lo_half, val, 0.0).astype(BF16)
        ref[0, rows, seg_hi * SEG:(seg_hi + 1) * SEG] = jnp.where(lo_half, 0.0, val).astype(BF16)

    for p in range(2):
        store_pair(qa_ref, seg(COL_QA, p) * scale64, 2 * p, 2 * p + 1)
    ka_ref[0, rows, :] = proj[:, COL_KA:COL_VA].astype(BF16)
    va_ref[0, rows, :] = proj[:, COL_VA:COL_CQ].astype(BF16)

    cq = proj[:, COL_CQ:COL_CKV]
    cqn = (cq * lax.rsqrt(jnp.sum(cq * cq, axis=-1, keepdims=True) * (1.0 / MLA_Q_RANK) + EPS)
           * qn_ref[...]).astype(BF16)
    qm_raw = _dot(cqn, wuq_ref[...])
    ckv = proj[:, COL_CKV:COL_KPE]
    ckvn = (ckv * lax.rsqrt(jnp.mean(ckv * ckv, axis=-1, keepdims=True) + EPS) * kvn_ref[...]).astype(BF16)
    kv = _dot(ckvn, wukv_ref[...])
    kpe = proj[:, COL_KPE:COL_QC]

    def store_transposed(ref, v):
        ref[0, blk] = v.T.astype(BF16)

    store_transposed(vm_ref, kv[:, Q_PAD:])
    for hh in range(GROUP_HEADS):
        sl = slice(hh * SEG, (hh + 1) * SEG)
        qs = qm_raw[:, sl]
        qs = qs * lax.rsqrt(jnp.sum(qs * qs, axis=-1, keepdims=True) * (1.0 / MLA_QK) + EPS) * qg_ref[...]
        qm_ref[0, rows, sl] = (_rope_mla(qs, cm, sm, lane) * scale_m).astype(BF16)
        ks = kv[:, sl] + kpe
        ks = ks * lax.rsqrt(jnp.sum(ks * ks, axis=-1, keepdims=True) * (1.0 / MLA_QK) + EPS) * kg_ref[...]
        km_ref[0, rows, sl] = _rope_mla(ks, cm, sm, lane).astype(BF16)

    def norm_rope64(v, gain_row):
        v = v * lax.rsqrt(_ms64(v, bsel) + EPS) * g64_ref[gain_row:gain_row + 1, :]
        return _rope64(v, c64, s64, lane)

    qa_pair = norm_rope64(seg(COL_QC, 0), 0) * scale64
    qb_pair = norm_rope64(seg(COL_QC, 1), 0) * scale64
    store_pair(qc_ref, qa_pair, 0, 2)
    store_pair(qc_ref, qb_pair, 1, 3)
    kc_ref[0, rows, :] = norm_rope64(seg(COL_KC, 0), 1).astype(BF16)
    vc_ref[0, rows, :] = seg(COL_VC, 0).astype(BF16)

    for p in range(2):
        qd = norm_rope64(seg(COL_QD, p), 2)
        qd32_ref[0, rows, p * SEG:(p + 1) * SEG] = qd
        store_pair(qd_ref, qd * (scale64 * LOG2E), 2 * p, 2 * p + 1)
        kd = norm_rope64(seg(COL_KD, p), 3)
        kd_ref[0, rows, p * SEG:(p + 1) * SEG] = kd.astype(BF16)
        kmean_ref[0, blk, :, p * SEG:(p + 1) * SEG] = jnp.mean(kd, axis=0, keepdims=True)
    store_transposed(vd_ref, proj[:, COL_VD:N_IN_PAD])


def _prep_weights(w_in, mla_q_norm, mla_w_uq, mla_kv_norm, mla_w_ukv, mla_q_gain, mla_k_gain,
                  swa_q_gain, swa_k_gain, moba_q_gain, moba_k_gain):
    D = w_in.shape[0]
    z = lambda n: jnp.zeros((D, n), F32)
    mla = w_in[:, OFF_MLA:OFF_MLA + MLA_COLS]
    swa = w_in[:, OFF_SWA:OFF_SWA + SWA_COLS]
    swa_q = swa[:, :GROUP_WIDTH].reshape(D, GROUP_HEADS, HEAD_DIM)[:, jnp.array([0, 2, 1, 3])].reshape(D, GROUP_WIDTH)
    win = jnp.concatenate([
        w_in[:, OFF_SB:OFF_SB + SB_COLS],
        mla[:, :MLA_Q_RANK], z(CQ_PAD - MLA_Q_RANK),
        mla[:, MLA_Q_RANK:MLA_Q_RANK + MLA_KV_RANK],
        z(MLA_NOPE), mla[:, MLA_Q_RANK + MLA_KV_RANK:], z(SEG - MLA_QK),
        swa_q, swa[:, GROUP_WIDTH:],
        w_in[:, OFF_MOBA:OFF_MOBA + MOBA_COLS],
    ], axis=1).astype(BF16)
    assert win.shape[1] == N_IN_PAD
    qn = jnp.pad(mla_q_norm, (0, CQ_PAD - MLA_Q_RANK)).reshape(1, CQ_PAD)
    wuq = mla_w_uq.reshape(MLA_Q_RANK, GROUP_HEADS, MLA_QK)
    wuq = jnp.pad(wuq, ((0, CQ_PAD - MLA_Q_RANK), (0, 0), (0, SEG - MLA_QK))).reshape(CQ_PAD, Q_PAD).astype(BF16)
    wukv = mla_w_ukv.reshape(MLA_KV_RANK, GROUP_HEADS, MLA_NOPE + MLA_V)
    wkn = jnp.pad(wukv[:, :, :MLA_NOPE], ((0, 0), (0, 0), (0, SEG - MLA_NOPE))).reshape(MLA_KV_RANK, Q_PAD)
    wv = wukv[:, :, MLA_NOPE:].reshape(MLA_KV_RANK, GROUP_WIDTH)
    wukv_r = jnp.concatenate([wkn, wv], axis=1).astype(BF16)
    qg = jnp.pad(mla_q_gain, (0, SEG - MLA_QK)).reshape(1, SEG)
    kg = jnp.pad(mla_k_gain, (0, SEG - MLA_QK)).reshape(1, SEG)
    g64 = jnp.zeros((8, LANES), F32)
    for row, g in enumerate((swa_q_gain, swa_k_gain, moba_q_gain, moba_k_gain)):
        g64 = g64.at[row].set(jnp.tile(g, LANES // HEAD_DIM))
    return win, qn, wuq, mla_kv_norm.reshape(1, MLA_KV_RANK), wukv_r, qg, kg, g64


def _prep(x, attn_norm, tables, weights):
    B, S, D = x.shape
    ts = PREP_TS
    win, qn, wuq, kvn, wukv, qg, kg, g64 = weights
    tok = lambda w: pl.BlockSpec((1, ts, w), lambda b, i: (b, i, 0))
    nb_t = ts // MOBA_BLOCK
    VT = "channel-major value blocks"
    GW = GROUP_WIDTH
    out_widths = [(Q_PAD, BF16), (GW, BF16), (GW, BF16),
                  (Q_PAD, BF16), (Q_PAD, BF16), (VT, BF16),
                  (Q_PAD, BF16), (KV_W, BF16), (KV_W, BF16),
                  (Q_PAD, BF16), (GW, F32), (GW, BF16), (VT, BF16)]
    vt_shape = jax.ShapeDtypeStruct((B, S // MOBA_BLOCK, GROUP_WIDTH, MOBA_BLOCK), BF16)
    vt_spec = pl.BlockSpec((1, nb_t, GROUP_WIDTH, MOBA_BLOCK), lambda b, i: (b, i, 0, 0))
    out_shape = [vt_shape if w is VT else jax.ShapeDtypeStruct((B, S, w), dt) for w, dt in out_widths]
    out_specs = [vt_spec if w is VT else tok(w) for w, _ in out_widths]
    out_shape.append(jax.ShapeDtypeStruct((B, S // MOBA_BLOCK, 1, GW), F32))
    out_specs.append(pl.BlockSpec((1, nb_t, 1, GW), lambda b, i: (b, i, 0, 0)))
    consts = [attn_norm.reshape(1, D), win]
    tail = [qn, wuq, kvn, wukv, qg, kg, g64]
    in_specs = ([tok(D)] + [_const_spec(a.shape) for a in consts] + [tok(LANES)] * 4
                + [_const_spec(a.shape) for a in tail])
    return pl.pallas_call(
        _prep_kernel,
        grid=(B, S // ts),
        in_specs=in_specs,
        out_specs=out_specs,
        out_shape=out_shape,
        compiler_params=_cparams(("parallel", "parallel")),
        name="prep",
    )(x, *consts, *tables, *tail)


def _sb_kernel(q_ref, k_ref, v_ref, o_ref, acc_ref, carry_ref):
    T = SB_T
    i = pl.program_id(1)
    row = lax.broadcasted_iota(jnp.int32, (T, T), 0)
    col = lax.broadcasted_iota(jnp.int32, (T, T), 1)
    upper = jnp.concatenate([jnp.where(row > col, 1.0, 0.0), jnp.ones((T, T), F32)], axis=1).astype(BF16)
    upper = jnp.concatenate([upper, upper], axis=0)
    causal = col < row
    lane = lax.broadcasted_iota(jnp.int32, (1, LANES), 1)

    n_sub = q_ref.shape[1] // T
    chains = [(sub, hh) for sub in range(n_sub) for hh in range(GROUP_HEADS)]

    def block(t, first):
        kb = [i * n_sub + sub - t for sub in range(n_sub)]
        ks = [pl.ds(pl.multiple_of(jnp.maximum(b, 0) * T, T), T) for b in kb]
        kv = lambda ref, sub, hh: ref[0, ks[sub], (hh // 2) * SEG:(hh // 2 + 1) * SEG]
        z = [_dot_nt(q_ref[0, sub * T:(sub + 1) * T, hh * SEG:(hh + 1) * SEG], kv(k_ref, sub, hh))
             for sub, hh in chains]
        log_beta = [jnp.minimum(zz, 0.0) - jnp.log(1.0 + jnp.exp(-jnp.abs(zz))) for zz in z]
        log_keep = [lb - zz for lb, zz in zip(log_beta, z)]
        if first:
            log_keep = [jnp.where(causal, lk, 0.0) for lk in log_keep]
        parts = [jnp.concatenate(_split_bf16(lk), axis=1) for lk in log_keep]
        sums = [_dot(hl, upper) for hl in parts]
        if first:
            w = [jnp.where(causal, jnp.exp(lb + sm[:, :T]), 0.0) for lb, sm in zip(log_beta, sums)]
            carry = [sm[:, T:] for sm in sums]
        else:
            live = [kb[sub] >= 0 for sub, hh in chains]
            old = [carry_ref[c] for c in range(len(chains))]
            w = [jnp.where(lv, jnp.exp(lb + (sm[:, :T] + cr)), 0.0) for lv, lb, sm, cr in zip(live, log_beta, sums, old)]
            carry = [jnp.where(lv, cr + sm[:, T:], NEG) for lv, sm, cr in zip(live, sums, old)]
        pv = [_dot(ww.astype(BF16), kv(v_ref, sub, hh)) for (sub, hh), ww in zip(chains, w)]
        for c in range(len(chains)):
            acc_ref[c] = pv[c] if first else acc_ref[c] + pv[c]
            carry_ref[c] = carry[c]
        top = carry[0]
        for cr in carry[1:]:
            top = jnp.maximum(top, cr)
        return jnp.max(top)

    def cond(st):
        t, mx = st
        return jnp.logical_and(t <= i * n_sub + n_sub - 1, mx > SB_EXIT)

    def body(st):
        return st[0] + 1, block(st[0], False)

    lax.while_loop(cond, body, (1, block(0, True)))

    lo_half = lane < HEAD_DIM
    for sub in range(n_sub):
        for p in range(2):
            c = sub * GROUP_HEADS + 2 * p
            o_ref[0, sub * T:(sub + 1) * T, p * SEG:(p + 1) * SEG] = jnp.where(
                lo_half, acc_ref[c], acc_ref[c + 1]).astype(o_ref.dtype)


def _sb_attn(q, k, v):
    B, S, _ = k.shape
    T = SB_T
    rows = SB_SUB * T
    n_chains = SB_SUB * GROUP_HEADS
    full = pl.BlockSpec((1, S, GROUP_WIDTH), lambda b, i: (b, 0, 0))
    return pl.pallas_call(
        _sb_kernel,
        grid=(B, S // rows),
        in_specs=[pl.BlockSpec((1, rows, Q_PAD), lambda b, i: (b, i, 0)), full, full],
        out_specs=pl.BlockSpec((1, rows, GROUP_WIDTH), lambda b, i: (b, i, 0)),
        out_shape=jax.ShapeDtypeStruct((B, S, GROUP_WIDTH), BF16),
        scratch_shapes=[pltpu.VMEM((n_chains, T, SEG), F32), pltpu.VMEM((n_chains, T, T), F32)],
        compiler_params=_cparams(("parallel", "arbitrary")),
        name="sb_attn",
    )(q, k, v)


def _softmax_update_all(acc_ref, m_ref, heads, scores, vt_augs):
    m_old = [m_ref[hh][0:1, :] for hh in heads]
    acc_old = [acc_ref[hh] for hh in heads]
    m_new = [jnp.maximum(mo, jnp.max(s, axis=0, keepdims=True)) for mo, s in zip(m_old, scores)]
    p = [jnp.exp2(s - mn).astype(BF16) for s, mn in zip(scores, m_new)]
    pv = [_dot(va, pp) for va, pp in zip(vt_augs, p)]
    for hh, mo, mn, r, a in zip(heads, m_old, m_new, pv, acc_old):
        acc_ref[hh] = jnp.exp2(mo - mn) * a + r
        m_ref[hh] = jnp.broadcast_to(mn, m_ref.shape[1:])


def _values_t(vt_ref, first_blk, nblk, hh):
    rows = slice(hh * HEAD_DIM, (hh + 1) * HEAD_DIM)
    v = jnp.concatenate([vt_ref[0, first_blk + r, rows, :] for r in range(nblk)], axis=1)
    return jnp.concatenate([v, jnp.ones((ONES_ROWS, v.shape[1]), v.dtype)], axis=0)


def _softmax_pipelined(n_steps, qk_pair, vt_ref, acc_ref, m_ref, s_ref, p_ref, a_ref):
    pairs = ((0, 1), (2, 3))
    s_ref[0] = jnp.full(s_ref.shape[1:], -jnp.inf, F32)
    p_ref[0] = jnp.zeros(p_ref.shape[1:], BF16)
    a_ref[0] = jnp.ones(a_ref.shape[1:], F32)

    def softmax(m_old, s):
        m_new = jnp.maximum(m_old, jnp.max(s, axis=0, keepdims=True))
        return jnp.exp2(s - m_new).astype(BF16), jnp.exp2(m_old - m_new), m_new

    def half_step(j, rd, wr, with_scores):
        prev = jnp.maximum(j - 1, 0) * MLA_WIDE
        m_old = [m_ref[hh][0:1, :] for hh in range(GROUP_HEADS)]
        acc_old = [acc_ref[hh] for hh in range(GROUP_HEADS)]
        s1_prev = [s_ref[rd, n] for n in range(2)]
        p0_prev = [p_ref[rd, n] for n in range(2)]
        a0_prev = [a_ref[rd, n][0:1, :] for n in range(2)]
        vt = [_values_t(vt_ref, prev, MLA_WIDE, hh) for hh in range(GROUP_HEADS)]

        acc = [a0_prev[n] * acc_old[hh] + _dot(vt[hh], p0_prev[n]) for n, hh in enumerate(pairs[0])]
        s0 = qk_pair(0, j * MLA_WIDE) if with_scores else None
        r1 = [softmax(m_old[hh], s1_prev[n]) for n, hh in enumerate(pairs[1])]
        s1 = qk_pair(1, j * MLA_WIDE) if with_scores else None
        acc += [r1[n][1] * acc_old[hh] + _dot(vt[hh], r1[n][0]) for n, hh in enumerate(pairs[1])]
        m_new = [None, None, r1[0][2], r1[1][2]]
        if with_scores:
            r0 = [softmax(m_old[hh], s0[n]) for n, hh in enumerate(pairs[0])]
            m_new[0:2] = [r0[0][2], r0[1][2]]

        for hh in range(GROUP_HEADS):
            acc_ref[hh] = acc[hh]
            if m_new[hh] is not None:
                m_ref[hh] = jnp.broadcast_to(m_new[hh], m_ref.shape[1:])
        if with_scores:
            for n in range(2):
                p_ref[wr, n] = r0[n][0]
                a_ref[wr, n] = jnp.broadcast_to(r0[n][1], a_ref.shape[2:])
                s_ref[wr, n] = s1[n]

    def body(u, carry):
        half_step(2 * u, 0, 1, True)
        half_step(2 * u + 1, 1, 0, True)
        return carry

    lax.fori_loop(0, n_steps // 2, body, 0)

    @pl.when(n_steps % 2 == 1)
    def _():
        half_step(n_steps - 1, 0, 1, True)
        half_step(n_steps, 1, None, False)

    @pl.when(n_steps % 2 == 0)
    def _():
        half_step(n_steps, 0, None, False)


def _softmax_finish(acc_ref, o_ref):
    for p in range(2):
        halves = [acc_ref[hh][0:HEAD_DIM, :] / acc_ref[hh][HEAD_DIM:HEAD_DIM + 1, :] for hh in (2 * p, 2 * p + 1)]
        o_ref[0, :, p * SEG:(p + 1) * SEG] = jnp.concatenate(halves, axis=0).T.astype(o_ref.dtype)


def _mla_kernel(q_ref, k_ref, vt_ref, o_ref, acc_ref, m_ref, s_ref, p_ref, a_ref):
    T = MLA_T
    tk = MLA_WIDE * MOBA_BLOCK
    i = pl.program_id(1)
    key = lax.broadcasted_iota(jnp.int32, (tk, T), 0)
    qry = lax.broadcasted_iota(jnp.int32, (tk, T), 1)
    heads = range(GROUP_HEADS)

    def scores(hh, first_blk):
        ks = pl.ds(pl.multiple_of(first_blk * MOBA_BLOCK, MOBA_BLOCK), tk)
        return _dot_nt(k_ref[0, ks, hh * SEG:(hh + 1) * SEG], q_ref[0, :, hh * SEG:(hh + 1) * SEG])

    m_ref[...] = jnp.full(m_ref.shape, NEG, F32)
    acc_ref[...] = jnp.zeros(acc_ref.shape, F32)
    _softmax_pipelined(i * (T // tk), lambda pair, blk: [scores(2 * pair + n, blk) for n in range(2)],
                       vt_ref, acc_ref, m_ref, s_ref, p_ref, a_ref)
    for r in range(T // tk):
        first_blk = (i * (T // tk) + r) * MLA_WIDE
        s = [jnp.where(key + r * tk <= qry, scores(hh, first_blk), NEG) for hh in heads]
        _softmax_update_all(acc_ref, m_ref, heads, s, [_values_t(vt_ref, first_blk, MLA_WIDE, hh) for hh in heads])
    _softmax_finish(acc_ref, o_ref)


def _softmax_scratch(T):
    tk = MLA_WIDE * MOBA_BLOCK
    assert T % tk == 0
    return [pltpu.VMEM((GROUP_HEADS, HEAD_DIM + ONES_ROWS, T), F32), pltpu.VMEM((GROUP_HEADS, 8, T), F32),
            pltpu.VMEM((2, 2, tk, T), F32), pltpu.VMEM((2, 2, tk, T), BF16), pltpu.VMEM((2, 2, 8, T), F32)]


def _mla_attn(q, k, vt):
    B, S, _ = k.shape
    T = MLA_T
    return pl.pallas_call(
        _mla_kernel,
        grid=(B, S // T),
        in_specs=[pl.BlockSpec((1, T, Q_PAD), lambda b, i: (b, i, 0)),
                  pl.BlockSpec((1, S, Q_PAD), lambda b, i: (b, 0, 0)),
                  pl.BlockSpec((1,) + vt.shape[1:], lambda b, i: (b, 0, 0, 0))],
        out_specs=pl.BlockSpec((1, T, GROUP_WIDTH), lambda b, i: (b, i, 0)),
        out_shape=jax.ShapeDtypeStruct((B, S, GROUP_WIDTH), BF16),
        scratch_shapes=_softmax_scratch(T),
        compiler_params=_cparams(("parallel", "arbitrary")),
        name="mla_attn",
    )(q, k, vt)


def _swa_kernel(sink_ref, q_ref, k_ref, v_ref, o_ref):
    W = WINDOW
    i = pl.program_id(1)
    lane = lax.broadcasted_iota(jnp.int32, (1, LANES), 1)
    lo_half = lane < HEAD_DIM
    heads = range(GROUP_HEADS)
    sinks = [sink_ref[hh] for hh in heads]
    chains, ks, vs, bands = [], [], [], []
    for b in range(q_ref.shape[1] // W):
        q0 = i * q_ref.shape[1] + b * W
        start = pl.multiple_of(jnp.maximum(q0 - W, 0), W)
        ks.append(k_ref[0, pl.ds(start, 2 * W), :])
        vs.append(v_ref[0, pl.ds(start, 2 * W), :])
        qpos = q0 + lax.broadcasted_iota(jnp.int32, (W, 2 * W), 0)
        kpos = start + lax.broadcasted_iota(jnp.int32, (W, 2 * W), 1)
        bands.append(jnp.logical_and(kpos <= qpos, qpos - kpos < W))
        chains += [(b, hh) for hh in heads]
    s = [jnp.where(bands[b], _dot_nt(q_ref[0, b * W:(b + 1) * W, hh * SEG:(hh + 1) * SEG], ks[b]), NEG)
         for b, hh in chains]
    m = [jnp.maximum(jnp.max(ss, axis=1, keepdims=True), sinks[hh]) for ss, (b, hh) in zip(s, chains)]
    p = [jnp.exp(ss - mm) for ss, mm in zip(s, m)]
    denom = [jnp.sum(pp, axis=1, keepdims=True) + jnp.exp(sinks[hh] - mm) for pp, mm, (b, hh) in zip(p, m, chains)]
    outs = [_dot(pp.astype(BF16), vs[b]) / dd for pp, dd, (b, hh) in zip(p, denom, chains)]
    for b in range(q_ref.shape[1] // W):
        o = outs[b * GROUP_HEADS:(b + 1) * GROUP_HEADS]
        rows = slice(b * W, (b + 1) * W)
        o_ref[0, rows, 0:SEG] = jnp.where(lo_half, o[0], pltpu.roll(o[1], HEAD_DIM, 1)).astype(o_ref.dtype)
        o_ref[0, rows, SEG:2 * SEG] = jnp.where(lo_half, pltpu.roll(o[2], HEAD_DIM, 1), o[3]).astype(o_ref.dtype)


def _swa_attn(q, k, v, sinks):
    B, S, _ = k.shape
    W = SWA_T
    full = pl.BlockSpec((1, S, 2 * HEAD_DIM), lambda b, i: (b, 0, 0))
    return pl.pallas_call(
        _swa_kernel,
        grid=(B, S // W),
        in_specs=[pl.BlockSpec(memory_space=pltpu.SMEM),
                  pl.BlockSpec((1, W, Q_PAD), lambda b, i: (b, i, 0)), full, full],
        out_specs=pl.BlockSpec((1, W, GROUP_WIDTH), lambda b, i: (b, i, 0)),
        out_shape=jax.ShapeDtypeStruct((B, S, GROUP_WIDTH), BF16),
        compiler_params=_cparams(("parallel", "parallel")),
        name="swa_attn",
    )(sinks, q, k, v)


def _moba_kernel(q_ref, q32_ref, k_ref, vt_ref, kmean_ref, o_ref, acc_ref, m_ref, s_ref, p_ref, a_ref, qaug_ref):
    T = q_ref.shape[1]
    BLK = MOBA_BLOCK
    tk = MLA_WIDE * BLK
    i = pl.program_id(1)
    nbr = kmean_ref.shape[1]
    key = lax.broadcasted_iota(jnp.int32, (tk, T), 0)
    qry = lax.broadcasted_iota(jnp.int32, (tk, T), 1)
    lo_half = lax.broadcasted_iota(jnp.int32, (1, LANES), 1) < HEAD_DIM
    blk = lax.broadcasted_iota(jnp.int32, (nbr, T), 0).astype(F32)
    kblk = lax.broadcasted_iota(jnp.int32, (BLK, LANES), 1)
    own = (i * (T // BLK) + lax.broadcasted_iota(jnp.int32, (1, T), 1) // BLK).astype(F32)

    m_ref[...] = jnp.full(m_ref.shape, NEG, F32)
    acc_ref[...] = jnp.zeros(acc_ref.shape, F32)
    heads = range(GROUP_HEADS)
    pair_sl = [slice((hh // 2) * SEG, (hh // 2 + 1) * SEG) for hh in heads]

    g = []
    for hh in heads:
        head_lanes = lo_half if hh % 2 == 0 else jnp.logical_not(lo_half)
        qh_hi, qh_lo = _split_bf16(jnp.where(head_lanes, q32_ref[0, :, pair_sl[hh]], 0.0))
        km_hi, km_lo = _split_bf16(kmean_ref[0, :, pair_sl[hh]])
        gate = _dot_nt(km_hi, qh_hi) + (_dot_nt(km_lo, qh_hi) + _dot_nt(km_hi, qh_lo))
        g.append(jnp.where(blk < own, gate, NEG))
    sel = [jnp.where(blk == own, 1.0, 0.0) for _ in heads]
    for t in range(MOBA_TOPK):
        mx = [jnp.max(gg, axis=0, keepdims=True) for gg in g]
        first = [jnp.min(jnp.where(gg == m, blk, float(nbr)), axis=0, keepdims=True)
                 for gg, m in zip(g, mx)]
        hit = [blk == f for f in first]
        sel = [jnp.where(jnp.logical_and(h, t < own), 1.0, sl) for h, sl in zip(hit, sel)]
        g = [jnp.where(h, -jnp.inf, gg) for h, gg in zip(hit, g)]
    for hh in heads:
        neg_t = jnp.concatenate([(1.0 - sel[hh]) * NEG, jnp.zeros((LANES - nbr, T), F32)], axis=0)
        qaug_ref[hh, :, 0:SEG] = q_ref[0, :, hh * SEG:(hh + 1) * SEG]
        qaug_ref[hh, :, SEG:2 * SEG] = neg_t.T.astype(BF16)

    def scores_pair(pair, n0):
        ks = pl.ds(pl.multiple_of(n0 * BLK, BLK), MLA_WIDE * BLK)
        onehot = jnp.concatenate([jnp.where(kblk == n0 + r, 1.0, 0.0) for r in range(MLA_WIDE)],
                                 axis=0).astype(BF16)
        k_aug = jnp.concatenate([k_ref[0, ks, pair * SEG:(pair + 1) * SEG], onehot], axis=1)
        return [_dot_nt(k_aug, qaug_ref[2 * pair + n]) for n in range(2)]

    _softmax_pipelined(i * (T // tk), scores_pair, vt_ref, acc_ref, m_ref, s_ref, p_ref, a_ref)
    for r in range(T // tk):
        n0 = (i * (T // tk) + r) * MLA_WIDE
        s = [jnp.where(key + r * tk <= qry, ss, NEG) for ss in scores_pair(0, n0) + scores_pair(1, n0)]
        _softmax_update_all(acc_ref, m_ref, heads, s, [_values_t(vt_ref, n0, MLA_WIDE, hh) for hh in heads])
    _softmax_finish(acc_ref, o_ref)


def _moba_attn(q, q32, k, vt, kmean):
    B, S, _ = k.shape
    T = MLA_T
    n_blocks = S // MOBA_BLOCK
    assert n_blocks <= LANES
    nbr = -(-n_blocks // 16) * 16
    kmean = jnp.pad(kmean.reshape(B, n_blocks, GROUP_WIDTH), ((0, 0), (0, nbr - n_blocks), (0, 0)))
    return pl.pallas_call(
        _moba_kernel,
        grid=(B, S // T),
        in_specs=[pl.BlockSpec((1, T, Q_PAD), lambda b, i: (b, i, 0)),
                  pl.BlockSpec((1, T, GROUP_WIDTH), lambda b, i: (b, i, 0)),
                  pl.BlockSpec((1, S, GROUP_WIDTH), lambda b, i: (b, 0, 0)),
                  pl.BlockSpec((1,) + vt.shape[1:], lambda b, i: (b, 0, 0, 0)),
                  pl.BlockSpec((1, nbr, GROUP_WIDTH), lambda b, i: (b, 0, 0))],
        out_specs=pl.BlockSpec((1, T, GROUP_WIDTH), lambda b, i: (b, i, 0)),
        out_shape=jax.ShapeDtypeStruct((B, S, GROUP_WIDTH), BF16),
        scratch_shapes=_softmax_scratch(T) + [pltpu.VMEM((GROUP_HEADS, T, 2 * SEG), BF16)],
        compiler_params=_cparams(("parallel", "arbitrary")),
        name="moba_attn",
    )(q, q32, k, vt, kmean)


def _mix_groups(x, group_refs, gn_ref, wo_ref):
    for g, ref in enumerate(group_refs):
        o = ref[0].astype(F32)
        ms = jnp.mean(o * o, axis=-1, keepdims=True)
        n = (o * lax.rsqrt(ms + EPS) * gn_ref[:, g * GROUP_WIDTH:(g + 1) * GROUP_WIDTH]).astype(BF16)
        x = x + _dot(n, wo_ref[g * GROUP_WIDTH:(g + 1) * GROUP_WIDTH, :])
    return x


def _ffn_kernel(x_ref, oa_ref, ob_ref, oc_ref, od_ref, p_ref, gn_ref, wo_ref,
                fn_ref, wup_ref, cw_ref, cb_ref, wdn_ref, pproj_ref, pgate_ref, y_ref,
                halo_ref, ubuf0_ref, ubuf1_ref, h_ref):
    ts = x_ref.shape[1]
    fc2 = wup_ref.shape[2]
    fc = fc2 // 2
    n_chunks = wup_ref.shape[0]
    i = pl.program_id(1)

    @pl.when(i == 0)
    def _():
        halo_ref[...] = jnp.zeros(halo_ref.shape, F32)

    x = _mix_groups(x_ref[0], (oa_ref, ob_ref, oc_ref, od_ref), gn_ref, wo_ref)
    ms = jnp.mean(x * x, axis=-1, keepdims=True)
    h_ref[...] = (x * lax.rsqrt(ms + EPS) * fn_ref[...]).astype(BF16)
    y_ref[0] = x

    def up(c, ubuf_ref):
        u = _dot(h_ref[...], wup_ref[c])
        ubuf_ref[0:HALO, :] = halo_ref[c]
        ubuf_ref[HALO:HALO + ts, :] = u
        halo_ref[c] = u[ts - HALO:ts, :]

    def down(c, ubuf_ref):
        cw = cw_ref[c]
        y = (cw[0:1, :] * ubuf_ref[HALO - 2:HALO - 2 + ts, :] + cw[1:2, :] * ubuf_ref[HALO - 1:HALO - 1 + ts, :]
             + cw[2:3, :] * ubuf_ref[HALO:HALO + ts, :] + cb_ref[c])
        ya = y[:, :fc]
        g = ya * (1.0 / (1.0 + jnp.exp(-ya))) * y[:, fc:]
        y_ref[0] += _dot(g.astype(BF16), wdn_ref[c])

    up(0, ubuf0_ref)

    def pair(cc, carry):
        up(2 * cc + 1, ubuf1_ref)
        down(2 * cc, ubuf0_ref)
        up(2 * cc + 2, ubuf0_ref)
        down(2 * cc + 1, ubuf1_ref)
        return carry

    assert n_chunks % 2 == 1
    lax.fori_loop(0, n_chunks // 2, pair, 0)
    down(n_chunks - 1, ubuf0_ref)
    x2 = y_ref[0]
    gate = _dot(x2.astype(BF16), pgate_ref[...])
    emb = _dot(p_ref[0, 0].astype(BF16), pproj_ref[...])
    y_ref[0] = x2 + emb * (1.0 / (1.0 + jnp.exp(-gate)))


def _ffn_weights(ffn_norm, w_up, conv_w, conv_b, w_down, ple_proj, ple_gate):
    fc = FFN_FC
    nc = D_FF // fc
    D = w_up.shape[0]
    pair = lambda a: jnp.concatenate([a[..., :D_FF].reshape(a.shape[:-1] + (nc, fc)),
                                      a[..., D_FF:].reshape(a.shape[:-1] + (nc, fc))], axis=-1)
    wup = jnp.moveaxis(pair(w_up), 1, 0).astype(BF16)
    cw = jnp.pad(jnp.moveaxis(pair(conv_w), 1, 0), ((0, 0), (0, 8 - CONV_WIDTH), (0, 0)))
    cb = jnp.moveaxis(pair(conv_b.reshape(1, -1)), 1, 0)
    wdn = w_down.reshape(nc, fc, D).astype(BF16)
    return ffn_norm.reshape(1, D), wup, cw, cb, wdn, ple_proj.astype(BF16), ple_gate.astype(BF16)


def _ffn(x, groups, p, layer, group_norm, w_o, weights):
    B, S, D = x.shape
    ts = FFN_TS
    fn, wup, cw, cb, wdn, pproj, pgate = weights
    consts = (group_norm.reshape(1, D), w_o) + tuple(weights)
    nc, _, fc2 = wup.shape
    tok = lambda w: pl.BlockSpec((1, ts, w), lambda b, i: (b, i, 0))
    once = lambda a: pl.BlockSpec(a.shape, lambda *_: (0,) * a.ndim, pipeline_mode=pl.Buffered(1))
    return pl.pallas_call(
        _ffn_kernel,
        grid=(B, S // ts),
        in_specs=([tok(D)] + [tok(GROUP_WIDTH)] * 4
                  + [pl.BlockSpec((1, 1, ts, PLE_DIM), lambda b, i: (layer, b, i, 0))] + [once(a) for a in consts]),
        out_specs=tok(D),
        out_shape=jax.ShapeDtypeStruct((B, S, D), F32),
        scratch_shapes=[pltpu.VMEM((nc, HALO, fc2), F32), pltpu.VMEM((ts + HALO, fc2), F32),
                        pltpu.VMEM((ts + HALO, fc2), F32), pltpu.VMEM((ts, D), BF16)],
        compiler_params=_cparams(("arbitrary", "arbitrary")),
        name="ffn_ple",
    )(x, *groups, p, *consts)


def kernel(x, p, positions, attn_norm, w_in, mla_q_norm, mla_w_uq, mla_kv_norm, mla_w_ukv, mla_q_gain, mla_k_gain, swa_q_gain, swa_k_gain, swa_sinks, moba_q_gain, moba_k_gain, group_norm, w_o, ffn_norm, w_up, conv_w, conv_b, w_down, ple_proj, ple_gate):
    B, S, D = x.shape
    depth = w_in.shape[0]
    assert D == D_MODEL and S % max(MLA_T, PREP_TS, FFN_TS, SWA_T, SB_SUB * SB_T) == 0
    tables = _rope_tables(positions)
    for i in range(depth):
        pw = _prep_weights(w_in[i], mla_q_norm[i], mla_w_uq[i], mla_kv_norm[i], mla_w_ukv[i], mla_q_gain[i],
                           mla_k_gain[i], swa_q_gain[i], swa_k_gain[i], moba_q_gain[i], moba_k_gain[i])
        (qa, ka, va, qm, km, vm, qc, kc, vc, qd, qd32, kd, vd, kmean) = _prep(x, attn_norm[i], tables, pw)
        o_a = _sb_attn(qa, ka, va)
        o_b = _mla_attn(qm, km, vm)
        o_c = _swa_attn(qc, kc, vc, swa_sinks[i])
        o_d = _moba_attn(qd, qd32, kd, vd, kmean)
        fw = _ffn_weights(ffn_norm[i], w_up[i], conv_w[i], conv_b[i], w_down[i], ple_proj[i], ple_gate[i])
        x = _ffn(x, (o_a, o_b, o_c, o_d), p, i, group_norm[i], w_o[i].astype(BF16), fw)
    return x
```
